```python
import math
import jax
import jax.numpy as jnp
from jax import lax
import numpy as np

D_MODEL = 1024
BATCH = 8
SEQ = 2048
DEPTH = 4
DEC_BATCH = 128
DEC_SEQ = 1
PAST_LEN = 2048
PAGE_SIZE = 128

N_MIXERS = 3
N_NSA = (DEPTH + 2) // N_MIXERS
N_DIFF = (DEPTH + 1) // N_MIXERS
N_SSD = DEPTH // N_MIXERS

RMS_EPS = 1e-6
ROPE_THETA = 10000.0
Q_BLOCK = 128
NEG_INF = -1e30
HEAD_DIM = 64

NSA_HEADS = D_MODEL // HEAD_DIM
NSA_KV_HEADS = 4
NSA_GROUP = NSA_HEADS // NSA_KV_HEADS
CMP_BLOCK = 32
SEL_BLOCK = 64
N_SEL = 8
WINDOW = 512
CMP_HIDDEN = 4 * HEAD_DIM
FORCED_SCORE = 1e4
NSA_IN = NSA_HEADS * HEAD_DIM + 6 * NSA_KV_HEADS * HEAD_DIM + 3 * NSA_HEADS

DIFF_HEADS = D_MODEL // (2 * HEAD_DIM)

SSD_D_INNER = 2 * D_MODEL
SSD_HEADDIM = 64
SSD_HEADS = SSD_D_INNER // SSD_HEADDIM
SSD_GROUPS = 4
SSD_D_STATE = 128
SSD_CONV_W = 4
SSD_CHUNK = 128
SSD_CONV_DIM = SSD_D_INNER + 2 * SSD_GROUPS * SSD_D_STATE
SSD_IN = SSD_D_INNER + SSD_CONV_DIM + SSD_HEADS

D_FF = 2816
FFN_CONV_W = 3

kernel_name = 'hybrid_nsa_diff_ssd_convffn_step'


def rms_norm(x, g):
    xf = x.astype(jnp.float32)
    y = xf * lax.rsqrt(jnp.mean(xf * xf, axis=-1, keepdims=True) + RMS_EPS)
    return (y * g.astype(jnp.float32)).astype(x.dtype)


def rope(x, pos):
    half = x.shape[-1] // 2
    inv_freq = ROPE_THETA ** (-jnp.arange(half, dtype=jnp.float32) / half)
    ang = pos.astype(jnp.float32)[:, None] * inv_freq[None, :]
    shape = (1, pos.shape[0]) + (1,) * (x.ndim - 3) + (half,)
    cos = jnp.cos(ang).reshape(shape)
    sin = jnp.sin(ang).reshape(shape)
    xf = x.astype(jnp.float32)
    x1, x2 = xf[..., :half], xf[..., half:]
    return jnp.concatenate([x1 * cos - x2 * sin, x2 * cos + x1 * sin], axis=-1).astype(x.dtype)


def causal_dwconv(x, prev, w):
    width, t = w.shape[0], x.shape[1]
    xe = jnp.concatenate([prev.astype(x.dtype), x], axis=1)
    y = sum(xe[:, k:k + t] * w[k] for k in range(width))
    return y, xe[:, -(width - 1):]


def gather_pages(pool, page_table):
    g = pool[page_table]
    return g.reshape((g.shape[0], g.shape[1] * g.shape[2]) + g.shape[3:])


def compress_blocks(k, pos_emb, w1, w2):
    b, t, G, d = k.shape
    n = t // CMP_BLOCK
    kb = k.reshape(b, n, CMP_BLOCK, G, d) + pos_emb[None, None, :, None, :]
    kb = kb.transpose(0, 1, 3, 2, 4).reshape(b, n, G, CMP_BLOCK * d)
    return jax.nn.gelu(kb @ w1) @ w2


def nsa_attend(q, pos_q, k_cmp, v_cmp, k_slc, v_slc, k_win, v_win, pos_win, gates,
               cmp_pos, cmp_w1, cmp_w2):
    b, tq = q.shape[:2]
    tk = k_cmp.shape[1]
    G, R, d = NSA_KV_HEADS, NSA_GROUP, HEAD_DIM
    scale = d ** -0.5
    n_cmp = tk // CMP_BLOCK
    kc = compress_blocks(k_cmp[:, :n_cmp * CMP_BLOCK], cmp_pos[0], cmp_w1[0], cmp_w2[0])
    vc = compress_blocks(v_cmp[:, :n_cmp * CMP_BLOCK], cmp_pos[1], cmp_w1[1], cmp_w2[1])
    cmp_end = (jnp.arange(n_cmp) + 1) * CMP_BLOCK - 1
    ratio = SEL_BLOCK // CMP_BLOCK
    n_sb = -(-tk // SEL_BLOCK)
    pad = n_sb * SEL_BLOCK - tk

    def to_blocks(a):
        a = jnp.pad(a, ((0, 0), (0, pad), (0, 0), (0, 0)))
        return a.reshape(b, n_sb, SEL_BLOCK, G, d).transpose(0, 3, 1, 2, 4)

    ks_blocks, vs_blocks = to_blocks(k_slc), to_blocks(v_slc)
    k_sel = min(N_SEL, n_sb)
    blk = jnp.arange(n_sb)
    b_idx = jnp.arange(b)[:, None, None, None]
    g_idx = jnp.arange(G)[None, :, None, None]
    qb = math.gcd(tq, Q_BLOCK)
    n_qb = tq // qb
    n_win = k_win.shape[1] - tq + qb

    def block(i):
        start = i * qb
        qg = lax.dynamic_slice_in_dim(q, start, qb, axis=1).reshape(b, qb, G, R, d)
        pq = lax.dynamic_slice_in_dim(pos_q, start, qb)
        s_c = jnp.einsum('bqgrd,bngd->bgrqn', qg, kc).astype(jnp.float32) * scale
        vis = cmp_end[None, :] <= pq[:, None]
        p_c = jax.nn.softmax(jnp.where(vis, s_c, NEG_INF), axis=-1) * vis
        o_c = jnp.einsum('bgrqn,bngd->bqgrd', p_c.astype(vc.dtype), vc)
        imp = jnp.pad(p_c.sum(axis=2), ((0, 0), (0, 0), (0, 0), (0, n_sb * ratio - n_cmp)))
        imp = imp.reshape(b, G, qb, n_sb, ratio).sum(axis=-1)
        cur = pq // SEL_BLOCK
        forced = (blk[None, :] == cur[:, None]) | (blk[None, :] == 0)
        score = jnp.where(forced, FORCED_SCORE, imp)
        score = jnp.where(blk[None, :] > cur[:, None], NEG_INF, score)
        _, idx = lax.top_k(score, k_sel)
        n_s = k_sel * SEL_BLOCK
        ks = ks_blocks[b_idx, g_idx, idx].reshape(b, G, qb, n_s, d)
        vs = vs_blocks[b_idx, g_idx, idx].reshape(b, G, qb, n_s, d)
        kpos = (idx[..., None] * SEL_BLOCK + jnp.arange(SEL_BLOCK)).reshape(b, G, qb, n_s)
        s_s = jnp.einsum('bqgrd,bgqsd->bgrqs', qg, ks).astype(jnp.float32) * scale
        m_s = (kpos <= pq[None, None, :, None])[:, :, None]
        p_s = jax.nn.softmax(jnp.where(m_s, s_s, NEG_INF), axis=-1)
        o_s = jnp.einsum('bgrqs,bgqsd->bqgrd', p_s.astype(vs.dtype), vs)
        kw = lax.dynamic_slice_in_dim(k_win, start, n_win, axis=1)
        vw = lax.dynamic_slice_in_dim(v_win, start, n_win, axis=1)
        pw = lax.dynamic_slice_in_dim(pos_win, start, n_win)
        s_w = jnp.einsum('bqgrd,bsgd->bgrqs', qg, kw).astype(jnp.float32) * scale
        dist = pq[:, None] - pw[None, :]
        m_w = (dist >= 0) & (dist < WINDOW) & (pw[None, :] >= 0)
        p_w = jax.nn.softmax(jnp.where(m_w, s_w, NEG_INF), axis=-1)
        o_w = jnp.einsum('bgrqs,bsgd->bqgrd', p_w.astype(vw.dtype), vw)
        gb = lax.dynamic_slice_in_dim(gates, start, qb, axis=1).reshape(b, qb, G, R, 3)
        o = gb[..., 0:1] * o_c + gb[..., 1:2] * o_s + gb[..., 2:3] * o_w
        return o.reshape(b, qb, G * R * d)

    out = lax.map(block, jnp.arange(n_qb))
    return out.transpose(1, 0, 2, 3).reshape(b, tq, G * R * d)


def nsa_layer(h, pos, past_rows, win_k_past, win_v_past, win_pos,
              w_in, q_norm, k_norm, cmp_pos, cmp_w1, cmp_w2, w_out):
    b, t, _ = h.shape
    G, d = NSA_KV_HEADS, HEAD_DIM
    q, kv, g = jnp.split(h @ w_in, [NSA_HEADS * d, NSA_HEADS * d + 6 * G * d], axis=-1)
    q = rope(rms_norm(q.reshape(b, t, NSA_HEADS, d), q_norm), pos)
    kv = kv.reshape(b, t, 3, 2, G, d)
    k = rope(rms_norm(kv[:, :, :, 0], k_norm[:, None, :]), pos)
    v = kv[:, :, :, 1]
    gates = jax.nn.sigmoid(g.reshape(b, t, NSA_HEADS, 3))
    rows = (k[:, :, 0], v[:, :, 0], k[:, :, 1], v[:, :, 1])
    full = [jnp.concatenate([pr, r], axis=1) for pr, r in zip(past_rows, rows)]
    kw = jnp.concatenate([win_k_past, k[:, :, 2]], axis=1)
    vw = jnp.concatenate([win_v_past, v[:, :, 2]], axis=1)
    o = nsa_attend(q, pos, full[0], full[1], full[2], full[3], kw, vw, win_pos, gates,
                   cmp_pos, cmp_w1, cmp_w2)
    return o @ w_out, rows, kw, vw


def diff_attend(q, pos_q, k, v, lam):
    b, tq = q.shape[:2]
    tk = k.shape[1]
    scale = HEAD_DIM ** -0.5
    kpos = jnp.arange(tk)
    qb = math.gcd(tq, Q_BLOCK)
    n_qb = tq // qb

    def block(i):
        start = i * qb
        qs = lax.dynamic_slice_in_dim(q, start, qb, axis=1)
        pq = lax.dynamic_slice_in_dim(pos_q, start, qb)
        s = jnp.einsum('bqhcd,bkhcd->bhcqk', qs, k).astype(jnp.float32) * scale
        m = kpos[None, :] <= pq[:, None]
        p = jax.nn.softmax(jnp.where(m, s, NEG_INF), axis=-1)
        pd = p[:, :, 0] - lam * p[:, :, 1]
        return jnp.einsum('bhqk,bkhe->bqhe', pd.astype(v.dtype), v)

    out = lax.map(block, jnp.arange(n_qb))
    return out.transpose(1, 0, 2, 3, 4).reshape(b, tq, DIFF_HEADS, 2 * HEAD_DIM)


def diff_layer(h, pos, k_past, v_past, w_in, q_norm, k_norm, lam_vec, sub_norm, w_out, lam_init):
    b, t, _ = h.shape
    q, k, v = jnp.split(h @ w_in, 3, axis=-1)
    q = rope(rms_norm(q.reshape(b, t, 2 * DIFF_HEADS, HEAD_DIM), q_norm), pos).reshape(b, t, DIFF_HEADS, 2, HEAD_DIM)
    k = rope(rms_norm(k.reshape(b, t, 2 * DIFF_HEADS, HEAD_DIM), k_norm), pos).reshape(b, t, DIFF_HEADS, 2, HEAD_DIM)
    v = v.reshape(b, t, DIFF_HEADS, 2 * HEAD_DIM)
    lv = lam_vec.astype(jnp.float32)
    lam = jnp.exp(jnp.sum(lv[0] * lv[1])) - jnp.exp(jnp.sum(lv[2] * lv[3])) + lam_init
    o = diff_attend(q, pos, jnp.concatenate([k_past, k], axis=1), jnp.concatenate([v_past, v], axis=1), lam)
    o = rms_norm(o, sub_norm) * (1.0 - lam_init)
    return o.reshape(b, t, D_MODEL) @ w_out, k, v


def ssd_scan(x, dt, A, B, C, s0):
    b, t, H, P = x.shape
    G, N = B.shape[2], B.shape[3]
    R = H // G
    l = math.gcd(t, SSD_CHUNK)
    nc = t // l
    f32 = jnp.float32
    xdt = (x.astype(f32) * dt[..., None]).reshape(b, nc, l, G, R, P)
    a = (dt * A).reshape(b, nc, l, G, R)
    acs = jnp.cumsum(a, axis=2).transpose(0, 1, 3, 4, 2)
    Bc = B.astype(f32).reshape(b, nc, l, G, N)
    Cc = C.astype(f32).reshape(b, nc, l, G, N)
    seg = acs[..., :, None] - acs[..., None, :]
    tril = jnp.tril(jnp.ones((l, l), dtype=bool))
    L = jnp.exp(jnp.where(tril, seg, -jnp.inf))
    cb = jnp.einsum('bclgn,bcsgn->bcgls', Cc, Bc)
    y_diag = jnp.einsum('bcgls,bcgrls,bcsgrp->bclgrp', cb, L, xdt)
    decay = jnp.exp(acs[..., -1:] - acs)
    chunk_states = jnp.einsum('bclgn,bcgrl,bclgrp->bcgrpn', Bc, decay, xdt)
    chunk_decay = jnp.exp(acs[..., -1])

    def step(s, inp):
        cs, cd = inp
        return s * cd[..., None, None] + cs, s

    s_final, s_starts = lax.scan(step, s0.astype(f32).reshape(b, G, R, P, N),
                                 (jnp.moveaxis(chunk_states, 1, 0), jnp.moveaxis(chunk_decay, 1, 0)))
    s_starts = jnp.moveaxis(s_starts, 0, 1)
    y_off = jnp.einsum('bclgn,bcgrpn,bcgrl->bclgrp', Cc, s_starts, jnp.exp(acs))
    y = (y_diag + y_off).reshape(b, t, H, P)
    return y, s_final.reshape(b, H, P, N)


def ssd_layer(h, conv_prev, ssm_prev, w_in, conv_w, conv_b, dt_bias, a_log, d_skip, norm_g, w_out):
    b, t, _ = h.shape
    z, xbc, dt = jnp.split(h @ w_in, [SSD_D_INNER, SSD_D_INNER + SSD_CONV_DIM], axis=-1)
    xbc, conv_new = causal_dwconv(xbc, conv_prev, conv_w)
    xbc = jax.nn.silu(xbc + conv_b)
    x, B, C = jnp.split(xbc, [SSD_D_INNER, SSD_D_INNER + SSD_GROUPS * SSD_D_STATE], axis=-1)
    x = x.reshape(b, t, SSD_HEADS, SSD_HEADDIM)
    dt = jax.nn.softplus(dt.astype(jnp.float32) + dt_bias.astype(jnp.float32))
    A = -jnp.exp(a_log.astype(jnp.float32))
    y, ssm_new = ssd_scan(x, dt, A, B.reshape(b, t, SSD_GROUPS, SSD_D_STATE),
                          C.reshape(b, t, SSD_GROUPS, SSD_D_STATE), ssm_prev)
    y = (y + d_skip.astype(jnp.float32)[:, None] * x.astype(jnp.float32)).astype(h.dtype)
    y = y.reshape(b, t, SSD_D_INNER) * jax.nn.silu(z)
    y = rms_norm(y.reshape(b, t, SSD_GROUPS, SSD_D_INNER // SSD_GROUPS),
                 norm_g.reshape(SSD_GROUPS, SSD_D_INNER // SSD_GROUPS)).reshape(b, t, SSD_D_INNER)
    return y @ w_out, conv_new, ssm_new.astype(ssm_prev.dtype)


def conv_ffn(h, prev, w_up, conv_w, w_down):
    up, new_prev = causal_dwconv(h @ w_up, prev, conv_w)
    u, g = jnp.split(up, 2, axis=-1)
    return (jax.nn.silu(g) * u) @ w_down, new_prev


def setup_inputs(seed: int = 0) -> dict:
    key = jax.random.key(seed)
    keys = iter(jax.random.split(key, 64))

    def nrm(shape, scale=1.0):
        return jax.random.normal(next(keys), shape, jnp.float32) * scale

    def gain(shape):
        return 1.0 + nrm(shape, 0.05)

    n_pages = PAST_LEN // PAGE_SIZE
    n_used = DEC_BATCH * n_pages
    n_phys = n_used + n_used // 4
    page_table = jax.random.permutation(next(keys), n_phys)[:n_used].reshape(DEC_BATCH, n_pages).astype(jnp.int32)
    buf_len = min(WINDOW, PAST_LEN)
    kvh, hd = NSA_KV_HEADS, HEAD_DIM
    dt0 = jnp.exp(jax.random.uniform(next(keys), (N_SSD, SSD_HEADS), jnp.float32, math.log(1e-3), math.log(1e-1)))
    a0 = jax.random.uniform(next(keys), (N_SSD, SSD_HEADS), jnp.float32, 1.0, 16.0)
    out_scale = 0.5
    return {
        'x_prompt': nrm((BATCH, SEQ, D_MODEL)),
        'x_sample': nrm((DEC_BATCH, DEC_SEQ, D_MODEL)),
        'cache_nsa_cmp_k': nrm((N_NSA, n_phys, PAGE_SIZE, kvh, hd)),
        'cache_nsa_cmp_v': nrm((N_NSA, n_phys, PAGE_SIZE, kvh, hd)),
        'cache_nsa_slc_k': nrm((N_NSA, n_phys, PAGE_SIZE, kvh, hd)),
        'cache_nsa_slc_v': nrm((N_NSA, n_phys, PAGE_SIZE, kvh, hd)),
        'state_nsa_win_k': nrm((N_NSA, DEC_BATCH, buf_len, kvh, hd)),
        'state_nsa_win_v': nrm((N_NSA, DEC_BATCH, buf_len, kvh, hd)),
        'cache_diff_k': nrm((N_DIFF, n_phys, PAGE_SIZE, DIFF_HEADS, 2, hd)),
        'cache_diff_v': nrm((N_DIFF, n_phys, PAGE_SIZE, DIFF_HEADS, 2 * hd)),
        'state_ssd_conv': nrm((N_SSD, DEC_BATCH, SSD_CONV_W - 1, SSD_CONV_DIM)),
        'state_ssd_ssm': nrm((N_SSD, DEC_BATCH, SSD_HEADS, SSD_HEADDIM, SSD_D_STATE), 0.5),
        'state_ffn_conv': nrm((DEPTH, DEC_BATCH, FFN_CONV_W - 1, 2 * D_FF)),
        'page_table': page_table,
        'norm_mix': gain((DEPTH, D_MODEL)),
        'norm_ffn': gain((DEPTH, D_MODEL)),
        'nsa_w_in': nrm((N_NSA, D_MODEL, NSA_IN), D_MODEL ** -0.5),
        'nsa_q_norm': gain((N_NSA, hd)),
        'nsa_k_norm': gain((N_NSA, 3, hd)),
        'nsa_cmp_pos': nrm((N_NSA, 2, CMP_BLOCK, hd), 0.02),
        'nsa_cmp_w1': nrm((N_NSA, 2, CMP_BLOCK * hd, CMP_HIDDEN), (CMP_BLOCK * hd) ** -0.5),
        'nsa_cmp_w2': nrm((N_NSA, 2, CMP_HIDDEN, hd), CMP_HIDDEN ** -0.5),
        'nsa_w_out': nrm((N_NSA, NSA_HEADS * hd, D_MODEL), out_scale * (NSA_HEADS * hd) ** -0.5),
        'diff_w_in': nrm((N_DIFF, D_MODEL, 3 * D_MODEL), D_MODEL ** -0.5),
        'diff_q_norm': gain((N_DIFF, hd)),
        'diff_k_norm': gain((N_DIFF, hd)),
        'diff_lambda': nrm((N_DIFF, 4, hd), 0.1),
        'diff_sub_norm': gain((N_DIFF, 2 * hd)),
        'diff_w_out': nrm((N_DIFF, D_MODEL, D_MODEL), out_scale * D_MODEL ** -0.5),
        'ssd_w_in': nrm((N_SSD, D_MODEL, SSD_IN), D_MODEL ** -0.5),
        'ssd_conv_w': nrm((N_SSD, SSD_CONV_W, SSD_CONV_DIM), SSD_CONV_W ** -0.5),
        'ssd_conv_b': nrm((N_SSD, SSD_CONV_DIM), 0.02),
        'ssd_dt_bias': dt0 + jnp.log(-jnp.expm1(-dt0)),
        'ssd_a_log': jnp.log(a0),
        'ssd_d': gain((N_SSD, SSD_HEADS)),
        'ssd_norm': gain((N_SSD, SSD_D_INNER)),
        'ssd_w_out': nrm((N_SSD, SSD_D_INNER, D_MODEL), out_scale * SSD_D_INNER ** -0.5),
        'ffn_w_up': nrm((DEPTH, D_MODEL, 2 * D_FF), D_MODEL ** -0.5),
        'ffn_conv_w': nrm((DEPTH, FFN_CONV_W, 2 * D_FF), FFN_CONV_W ** -0.5),
        'ffn_w_down': nrm((DEPTH, D_FF, D_MODEL), out_scale * D_FF ** -0.5),
    }


def reference(x_prompt, x_sample, cache_nsa_cmp_k, cache_nsa_cmp_v, cache_nsa_slc_k, cache_nsa_slc_v,
              state_nsa_win_k, state_nsa_win_v, cache_diff_k, cache_diff_v, state_ssd_conv, state_ssd_ssm,
              state_ffn_conv, page_table, norm_mix, norm_ffn,
              nsa_w_in, nsa_q_norm, nsa_k_norm, nsa_cmp_pos, nsa_cmp_w1, nsa_cmp_w2, nsa_w_out,
              diff_w_in, diff_q_norm, diff_k_norm, diff_lambda, diff_sub_norm, diff_w_out,
              ssd_w_in, ssd_conv_w, ssd_conv_b, ssd_dt_bias, ssd_a_log, ssd_d, ssd_norm, ssd_w_out,
              ffn_w_up, ffn_conv_w, ffn_w_down):
    bp, tp, _ = x_prompt.shape
    bs, ts, _ = x_sample.shape
    past = page_table.shape[1] * PAGE_SIZE
    pos_p = jnp.arange(tp, dtype=jnp.int32)
    pos_s = past + jnp.arange(ts, dtype=jnp.int32)
    buf_len = state_nsa_win_k.shape[2]
    dt_ = x_prompt.dtype
    xp, xs = x_prompt, x_sample
    nsa_p, nsa_s, diff_p, diff_s, ssd_p, ssd_s, ffn_p, ffn_s = [], [], [], [], [], [], [], []
    for i in range(DEPTH):
        kind = i % N_MIXERS
        j = i // N_MIXERS
        hp = rms_norm(xp, norm_mix[i])
        hs = rms_norm(xs, norm_mix[i])
        if kind == 0:
            w = (nsa_w_in[j], nsa_q_norm[j], nsa_k_norm[j], nsa_cmp_pos[j], nsa_cmp_w1[j], nsa_cmp_w2[j], nsa_w_out[j])
            empty = jnp.zeros((bp, 0, NSA_KV_HEADS, HEAD_DIM), dt_)
            win0 = jnp.zeros((bp, WINDOW, NSA_KV_HEADS, HEAD_DIM), dt_)
            op, rows_p, kwp, vwp = nsa_layer(hp, pos_p, (empty, empty, empty, empty), win0, win0,
                                             jnp.arange(-WINDOW, tp, dtype=jnp.int32), *w)
            past_rows = (gather_pages(cache_nsa_cmp_k[j], page_table), gather_pages(cache_nsa_cmp_v[j], page_table),
                         gather_pages(cache_nsa_slc_k[j], page_table), gather_pages(cache_nsa_slc_v[j], page_table))
            os_, rows_s, kws, vws = nsa_layer(hs, pos_s, past_rows, state_nsa_win_k[j], state_nsa_win_v[j],
                                              jnp.arange(past - buf_len, past + ts, dtype=jnp.int32), *w)
            keep_p = min(WINDOW, tp)
            nsa_p.append(rows_p + (kwp[:, -keep_p:], vwp[:, -keep_p:]))
            nsa_s.append(rows_s + (kws[:, -buf_len:], vws[:, -buf_len:]))
        elif kind == 1:
            lam_init = 0.8 - 0.6 * math.exp(-0.3 * i)
            w = (diff_w_in[j], diff_q_norm[j], diff_k_norm[j], diff_lambda[j], diff_sub_norm[j], diff_w_out[j], lam_init)
            op, kp, vp = diff_layer(hp, pos_p, jnp.zeros((bp, 0, DIFF_HEADS, 2, HEAD_DIM), dt_),
                                    jnp.zeros((bp, 0, DIFF_HEADS, 2 * HEAD_DIM), dt_), *w)
            os_, ks_, vs_ = diff_layer(hs, pos_s, gather_pages(cache_diff_k[j], page_table),
                                       gather_pages(cache_diff_v[j], page_table), *w)
            diff_p.append((kp, vp))
            diff_s.append((ks_, vs_))
        else:
            w = (ssd_w_in[j], ssd_conv_w[j], ssd_conv_b[j], ssd_dt_bias[j], ssd_a_log[j], ssd_d[j], ssd_norm[j], ssd_w_out[j])
            op, cp, sp = ssd_layer(hp, jnp.zeros((bp, SSD_CONV_W - 1, SSD_CONV_DIM), dt_),
                                   jnp.zeros((bp, SSD_HEADS, SSD_HEADDIM, SSD_D_STATE), dt_), *w)
            os_, cs_, ss_ = ssd_layer(hs, state_ssd_conv[j], state_ssd_ssm[j], *w)
            ssd_p.append((cp, sp))
            ssd_s.append((cs_, ss_))
        xp = xp + op
        xs = xs + os_
        fp, fcp = conv_ffn(rms_norm(xp, norm_ffn[i]), jnp.zeros((bp, FFN_CONV_W - 1, 2 * D_FF), dt_),
                           ffn_w_up[i], ffn_conv_w[i], ffn_w_down[i])
        fs, fcs = conv_ffn(rms_norm(xs, norm_ffn[i]), state_ffn_conv[i], ffn_w_up[i], ffn_conv_w[i], ffn_w_down[i])
        ffn_p.append(fcp)
        ffn_s.append(fcs)
        xp = xp + fp
        xs = xs + fs
    y_prompt, y_sample = xp, xs
    nsa_cmp_k_p = jnp.stack([r[0] for r in nsa_p])
    nsa_cmp_k_s = jnp.stack([r[0] for r in nsa_s])
    nsa_cmp_v_p = jnp.stack([r[1] for r in nsa_p])
    nsa_cmp_v_s = jnp.stack([r[1] for r in nsa_s])
    nsa_slc_k_p = jnp.stack([r[2] for r in nsa_p])
    nsa_slc_k_s = jnp.stack([r[2] for r in nsa_s])
    nsa_slc_v_p = jnp.stack([r[3] for r in nsa_p])
    nsa_slc_v_s = jnp.stack([r[3] for r in nsa_s])
    nsa_win_k_p = jnp.stack([r[4] for r in nsa_p])
    nsa_win_k_s = jnp.stack([r[4] for r in nsa_s])
    nsa_win_v_p = jnp.stack([r[5] for r in nsa_p])
    nsa_win_v_s = jnp.stack([r[5] for r in nsa_s])
    diff_k_p = jnp.stack([r[0] for r in diff_p])
    diff_k_s = jnp.stack([r[0] for r in diff_s])
    diff_v_p = jnp.stack([r[1] for r in diff_p])
    diff_v_s = jnp.stack([r[1] for r in diff_s])
    ssd_conv_p = jnp.stack([r[0] for r in ssd_p])
    ssd_conv_s = jnp.stack([r[0] for r in ssd_s])
    ssd_ssm_p = jnp.stack([r[1] for r in ssd_p])
    ssd_ssm_s = jnp.stack([r[1] for r in ssd_s])
    ffn_conv_p = jnp.stack(ffn_p)
    ffn_conv_s = jnp.stack(ffn_s)
    return (y_prompt, y_sample,
            nsa_cmp_k_p, nsa_cmp_k_s, nsa_cmp_v_p, nsa_cmp_v_s,
            nsa_slc_k_p, nsa_slc_k_s, nsa_slc_v_p, nsa_slc_v_s,
            nsa_win_k_p, nsa_win_k_s, nsa_win_v_p, nsa_win_v_s,
            diff_k_p, diff_k_s, diff_v_p, diff_v_s,
            ssd_conv_p, ssd_conv_s, ssd_ssm_p, ssd_ssm_s,
            ffn_conv_p, ffn_conv_s)
```

```python
import functools
import math

import jax
import jax.numpy as jnp
from jax import lax
from jax.experimental import pallas as pl
from jax.experimental.pallas import tpu as pltpu

F32 = jnp.float32
BF16 = jnp.bfloat16

D_MODEL = 1024
DEPTH = 4
PAGE_SIZE = 128
N_MIXERS = 3
RMS_EPS = 1e-6
ROPE_THETA = 10000.0
Q_BLOCK = 128
NEG_INF = -1e30
HEAD_DIM = 64
SCALE = HEAD_DIM ** -0.5

NSA_HEADS = D_MODEL // HEAD_DIM
NSA_KV_HEADS = 4
NSA_GROUP = NSA_HEADS // NSA_KV_HEADS
CMP_BLOCK = 32
SEL_BLOCK = 64
N_SEL = 8
WINDOW = 512
FORCED_SCORE = 1e4
NSA_KV = NSA_KV_HEADS * HEAD_DIM
NSA_QKV = NSA_HEADS * HEAD_DIM + 6 * NSA_KV
NSA_IN_PAD = NSA_QKV + 128

DIFF_HEADS = D_MODEL // (2 * HEAD_DIM)

SSD_D_INNER = 2 * D_MODEL
SSD_HEADDIM = 64
SSD_HEADS = SSD_D_INNER // SSD_HEADDIM
SSD_GROUPS = 4
SSD_D_STATE = 128
SSD_CONV_W = 4
SSD_CHUNK = 128
SSD_CONV_DIM = SSD_D_INNER + 2 * SSD_GROUPS * SSD_D_STATE
SSD_IN_PAD = SSD_D_INNER + SSD_CONV_DIM + 128
SSD_GROUP_W = SSD_D_INNER // SSD_GROUPS
SSD_HEADS_PER_GROUP = SSD_HEADS // SSD_GROUPS

D_FF = 2816
FFN_CONV_W = 3
FFN_CHUNK = 256

LANE = 128
SUBLANE = 8
VMEM_LIMIT = 56 * 1024 * 1024
REMOVED = -3.4e38
INVALID = -3.2e38

NT_DIMS = (((1,), (1,)), ((), ()))


def _params(*sem):
    return pltpu.CompilerParams(dimension_semantics=sem, vmem_limit_bytes=VMEM_LIMIT)


def _dot(a, b):
    return jnp.dot(a, b, preferred_element_type=F32)


def _dot_nt(a, b):
    return lax.dot_general(a, b, NT_DIMS, preferred_element_type=F32)


def _split2(x):
    hi = x.astype(BF16)
    lo = (x - hi.astype(F32)).astype(BF16)
    return hi, lo


def _split3(x):
    hi = x.astype(BF16)
    r = x - hi.astype(F32)
    mid = r.astype(BF16)
    lo = (r - mid.astype(F32)).astype(BF16)
    return hi, mid, lo


def _dot_exact_rhs(x, m):
    hi, mid, lo = _split3(x)
    return _dot(hi, m) + _dot(mid, m) + _dot(lo, m)


def _lhs_exact_dot(m, x):
    hi, mid, lo = _split3(x)
    return _dot(m, hi) + _dot(m, mid) + _dot(m, lo)


def _sigmoid(x):
    return 1.0 / (1.0 + jnp.exp(-x))


def _silu(x):
    return x * _sigmoid(x)


def _softplus(x):
    return jnp.maximum(x, 0.0) + jnp.log1p(jnp.exp(-jnp.abs(x)))


def _gelu_tanh(x):
    return 0.5 * x * (1.0 + jnp.tanh(math.sqrt(2.0 / math.pi) * (x + 0.044715 * (x * x * x))))


def _rms_rows(x, g):
    return x * lax.rsqrt(jnp.mean(x * x, axis=-1, keepdims=True) + RMS_EPS) * g


def _norm_matmul_kernel(x_ref, g_ref, w_ref, o_ref, *, chunk):
    h = _rms_rows(x_ref[...], g_ref[...]).astype(BF16)
    n = o_ref.shape[1]
    for c in range(0, n, chunk):
        w = min(chunk, n - c)
        o_ref[:, c:c + w] = _dot(h, w_ref[:, c:c + w])


def norm_matmul(x, g, w, tm):
    m, k = x.shape
    n = w.shape[1]
    return pl.pallas_call(
        functools.partial(_norm_matmul_kernel, chunk=512),
        grid=(m // tm,),
        in_specs=[pl.BlockSpec((tm, k), lambda i: (i, 0)),
                  pl.BlockSpec((1, k), lambda i: (0, 0)),
                  pl.BlockSpec((k, n), lambda i: (0, 0))],
        out_specs=pl.BlockSpec((tm, n), lambda i: (i, 0)),
        out_shape=jax.ShapeDtypeStruct((m, n), F32),
        compiler_params=_params("parallel"),
        name="norm_matmul",
    )(x, g.reshape(1, k), w)


def _matmul_res_kernel(a_ref, w_ref, r_ref, o_ref):
    o_ref[...] = r_ref[...] + _dot(a_ref[...].astype(BF16), w_ref[...])


def matmul_res(a, w, res, tm):
    m, k = a.shape
    n = w.shape[1]
    return pl.pallas_call(
        _matmul_res_kernel,
        grid=(m // tm,),
        in_specs=[pl.BlockSpec((tm, k), lambda i: (i, 0)),
                  pl.BlockSpec((k, n), lambda i: (0, 0)),
                  pl.BlockSpec((tm, n), lambda i: (i, 0))],
        out_specs=pl.BlockSpec((tm, n), lambda i: (i, 0)),
        out_shape=jax.ShapeDtypeStruct((m, n), F32),
        compiler_params=_params("parallel"),
        name="matmul_res",
    )(a, w, res)


def _head_norm_rope(x, gain, cos, sin_signed, bd):
    hi, lo = _split2(x * x)
    ss = _dot(hi, bd) + _dot(lo, bd)
    y = x * lax.rsqrt(ss * (1.0 / HEAD_DIM) + RMS_EPS) * gain
    lane = lax.broadcasted_iota(jnp.int32, y.shape, 1)
    half = HEAD_DIM // 2
    width = y.shape[1]
    partner = jnp.where((lane & half) != 0, pltpu.roll(y, half, 1), pltpu.roll(y, width - half, 1))
    return y * cos + partner * sin_signed


def _post_kernel(p_ref, gain_ref, cos_ref, sin_ref, bd_ref, *out_refs, plan):
    cos = cos_ref[...]
    sin = sin_ref[...]
    bd = bd_ref[...]
    for mode, src, width, oi, oc in plan:
        x = p_ref[:, src:src + width]
        if mode == "rope":
            y = _head_norm_rope(x, gain_ref[:, src:src + width], cos, sin, bd)
        elif mode == "sigmoid":
            y = _sigmoid(x)
        else:
            y = x
        out_refs[oi][:, oc:oc + width] = y


def post_proj(proj, gain_row, cos, sin, bd, plan, out_widths, tm):
    m, n = proj.shape
    n_tab = cos.shape[0] // tm
    return pl.pallas_call(
        functools.partial(_post_kernel, plan=plan),
        grid=(m // tm,),
        in_specs=[pl.BlockSpec((tm, n), lambda i: (i, 0)),
                  pl.BlockSpec(gain_row.shape, lambda i: (0, 0)),
                  pl.BlockSpec((tm, 256), lambda i: (i % n_tab, 0)),
                  pl.BlockSpec((tm, 256), lambda i: (i % n_tab, 0)),
                  pl.BlockSpec((256, 256), lambda i: (0, 0))],
        out_specs=[pl.BlockSpec((tm, w), lambda i: (i, 0)) for w in out_widths],
        out_shape=[jax.ShapeDtypeStruct((m, w), F32) for w in out_widths],
        compiler_params=_params("parallel"),
        name="post_proj",
    )(proj, gain_row, cos, sin, bd)


NSA_PLAN = tuple(
    [("rope", c * 256, 256, 0, c * 256) for c in range(4)]
    + [("rope", 1024, 256, 1, 0), ("copy", 1280, 256, 2, 0),
       ("rope", 1536, 256, 3, 0), ("copy", 1792, 256, 4, 0),
       ("rope", 2048, 256, 5, 0), ("copy", 2304, 256, 6, 0),
       ("sigmoid", 2560, 128, 7, 0)])
NSA_OUT_W = (1024, 256, 256, 256, 256, 256, 256, 128)

DIFF_PLAN = tuple(
    [("rope", c * 256, 256, 0, c * 256) for c in range(4)]
    + [("rope", 1024 + c * 256, 256, 1, c * 256) for c in range(4)]
    + [("copy", 2048 + c * 256, 256, 2, c * 256) for c in range(4)])
DIFF_OUT_W = (1024, 1024, 1024)


def _compress_kernel(*refs, n_in):
    x_refs = refs[:n_in]
    pos_ref, w1_ref, w2_ref, o_ref = refs[n_in:]
    xs = [r[0] for r in x_refs]
    xr = xs[0] if n_in == 1 else jnp.concatenate(xs, axis=0)
    xr = xr + pos_ref[...]
    mats = []
    for g in range(NSA_KV_HEADS):
        cols = [xr[:, j * NSA_KV + g * HEAD_DIM: j * NSA_KV + (g + 1) * HEAD_DIM] for j in range(CMP_BLOCK)]
        mats.append(jnp.concatenate(cols, axis=1))
    xg = jnp.concatenate(mats, axis=0).astype(BF16)
    h = _gelu_tanh(_dot(xg, w1_ref[...]))
    o_ref[0] = _dot(h.astype(BF16), w2_ref[...])


def _compress_pages_kernel(pt_ref, *refs, n_in):
    del pt_ref
    _compress_kernel(*refs, n_in=n_in)


def compress_rows(rows, pos_row, w1, w2, n_blocks):
    b = rows.shape[0]
    wide = CMP_BLOCK * NSA_KV
    return pl.pallas_call(
        functools.partial(_compress_kernel, n_in=1),
        grid=(b,),
        in_specs=[pl.BlockSpec((1, n_blocks, wide), lambda i: (i, 0, 0)),
                  pl.BlockSpec((1, wide), lambda i: (0, 0)),
                  pl.BlockSpec(w1.shape, lambda i: (0, 0)),
                  pl.BlockSpec(w2.shape, lambda i: (0, 0))],
        out_specs=pl.BlockSpec((1, NSA_KV_HEADS * n_blocks, HEAD_DIM), lambda i: (i, 0, 0)),
        out_shape=jax.ShapeDtypeStruct((b, NSA_KV_HEADS * n_blocks, HEAD_DIM), F32),
        compiler_params=_params("parallel"),
        name="compress_rows",
    )(rows, pos_row, w1, w2)


def _page_map(bi, pt, *, j, base):
    return (base + pt[bi, j], 0, 0)


def compress_pages(pool, page_table, base, pos_row, w1, w2):
    b, n_pages = page_table.shape
    per_page = PAGE_SIZE // CMP_BLOCK
    wide = CMP_BLOCK * NSA_KV
    n_blocks = n_pages * per_page
    page_specs = [pl.BlockSpec((1, per_page, wide), functools.partial(_page_map, j=j, base=base))
                  for j in range(n_pages)]
    return pl.pallas_call(
        functools.partial(_compress_pages_kernel, n_in=n_pages),
        grid_spec=pltpu.PrefetchScalarGridSpec(
            num_scalar_prefetch=1,
            grid=(b,),
            in_specs=page_specs + [pl.BlockSpec((1, wide), lambda i, pt: (0, 0)),
                                   pl.BlockSpec(w1.shape, lambda i, pt: (0, 0)),
                                   pl.BlockSpec(w2.shape, lambda i, pt: (0, 0))],
            out_specs=pl.BlockSpec((1, NSA_KV_HEADS * n_blocks, HEAD_DIM), lambda i, pt: (i, 0, 0))),
        out_shape=jax.ShapeDtypeStruct((b, NSA_KV_HEADS * n_blocks, HEAD_DIM), F32),
        compiler_params=_params("parallel"),
        name="compress_pages",
    )(page_table, *([pool] * n_pages), pos_row, w1, w2)


def _select_blocks(imp_b, cur, n_sb, k_sel):
    lane = lax.broadcasted_iota(jnp.int32, imp_b.shape, 1)
    forced = (lane == cur) | (lane == 0)
    score = jnp.where(forced, FORCED_SCORE, imp_b)
    score = jnp.where(lane > cur, NEG_INF, score)
    score = jnp.where(lane >= n_sb, INVALID, score)
    sel = jnp.zeros(imp_b.shape, F32)
    for _ in range(k_sel):
        mx = jnp.max(score, axis=-1, keepdims=True)
        idx = jnp.min(jnp.where(score == mx, lane, 1 << 20), axis=-1, keepdims=True)
        hit = lane == idx
        sel = jnp.where(hit, 1.0, sel)
        score = jnp.where(hit, REMOVED, score)
    return sel


def _masked_softmax_rows(s, vis):
    sm = jnp.where(vis, s, NEG_INF)
    m = jnp.max(sm, axis=-1, keepdims=True)
    e = jnp.where(vis, jnp.exp(sm - m), 0.0)
    den = jnp.sum(e, axis=-1, keepdims=True)
    return e / jnp.where(den > 0.0, den, 1.0)


def _nsa_prompt_kernel(q_ref, gt_ref, kc_ref, vc_ref, ks_ref, vs_ref, kw_ref, vw_ref,
                       pool_ref, expand_ref, o_ref, mask_ref, *, n_cmp, n_sb, k_sel):
    qi = pl.program_id(1)
    qb = Q_BLOCK
    rep = NSA_GROUP
    row = lax.broadcasted_iota(jnp.int32, (qb, 1), 0)
    pq = qi * qb + row
    pq_rep = jnp.concatenate([pq] * rep, axis=0)
    lane = lax.broadcasted_iota(jnp.int32, (qb, LANE), 1)
    pool = pool_ref[...]

    def flash(k_ref, v_ref, g, qg, lo, hi, valid_fn):
        def body(kt, carry):
            m, l, acc = carry
            off = pl.multiple_of(kt * LANE, LANE)
            k = k_ref[pl.ds(off, LANE), g * HEAD_DIM:(g + 1) * HEAD_DIM].astype(BF16)
            v = v_ref[pl.ds(off, LANE), g * HEAD_DIM:(g + 1) * HEAD_DIM].astype(BF16)
            s = _dot_nt(qg, k)
            valid = valid_fn(off)
            valid = jnp.concatenate([valid] * rep, axis=0)
            s = jnp.where(valid > 0.5, s, NEG_INF)
            m_new = jnp.maximum(m, jnp.max(s, axis=-1, keepdims=True))
            alpha = jnp.exp(m - m_new)
            p = jnp.exp(s - m_new) * valid
            l = alpha * l + jnp.sum(p, axis=-1, keepdims=True)
            acc = alpha * acc + _dot(p.astype(BF16), v)
            return m_new, l, acc

        init = (jnp.full((rep * qb, 1), NEG_INF, F32), jnp.zeros((rep * qb, 1), F32),
                jnp.zeros((rep * qb, HEAD_DIM), F32))
        _, l, acc = lax.fori_loop(lo, hi, body, init)
        return acc / l

    for g in range(NSA_KV_HEADS):
        qg = jnp.concatenate(
            [q_ref[:, (rep * g + r) * HEAD_DIM:(rep * g + r + 1) * HEAD_DIM] for r in range(rep)], axis=0)
        qg = (qg * SCALE).astype(BF16)

        kc = kc_ref[0, g * n_cmp:(g + 1) * n_cmp, :].astype(BF16)
        vc = vc_ref[0, g * n_cmp:(g + 1) * n_cmp, :].astype(BF16)
        s_c = _dot_nt(qg, kc)
        ncol = lax.broadcasted_iota(jnp.int32, s_c.shape, 1)
        vis = ((ncol + 1) * CMP_BLOCK - 1) <= pq_rep
        p_c = _masked_softmax_rows(s_c, vis)
        o_c = _dot(p_c.astype(BF16), vc)
        imp = p_c[0:qb]
        for r in range(1, rep):
            imp = imp + p_c[r * qb:(r + 1) * qb]
        hi_, lo_ = _split2(imp)
        imp_b = _dot(hi_, pool) + _dot(lo_, pool)
        sel = _select_blocks(imp_b, pq // SEL_BLOCK, n_sb, k_sel)
        mask_ref[...] = _dot(sel.astype(BF16), expand_ref[...])

        def sel_valid(off):
            kpos = off + lane
            return jnp.where((mask_ref[:, pl.ds(off, LANE)] > 0.5) & (kpos <= pq), 1.0, 0.0)

        def win_valid(off):
            dist = pq - (off + lane)
            return jnp.where((dist >= 0) & (dist < WINDOW), 1.0, 0.0)

        o_s = flash(ks_ref, vs_ref, g, qg, 0, qi + 1, sel_valid)
        o_w = flash(kw_ref, vw_ref, g, qg, jnp.maximum(qi - WINDOW // LANE, 0), qi + 1, win_valid)

        for r in range(rep):
            h = rep * g + r
            rows = slice(r * qb, (r + 1) * qb)
            o_ref[:, h * HEAD_DIM:(h + 1) * HEAD_DIM] = (
                gt_ref[:, 3 * h:3 * h + 1] * o_c[rows]
                + gt_ref[:, 3 * h + 1:3 * h + 2] * o_s[rows]
                + gt_ref[:, 3 * h + 2:3 * h + 3] * o_w[rows])


def nsa_prompt_attn(q, gates, kc, vc, ks, vs, kw, vw, b, t):
    nq = t // Q_BLOCK
    n_cmp = t // CMP_BLOCK
    n_sb = -(-t // SEL_BLOCK)
    k_sel = min(N_SEL, n_sb)
    ratio = SEL_BLOCK // CMP_BLOCK
    pool = (jnp.arange(n_cmp)[:, None] // ratio == jnp.arange(LANE)[None, :]).astype(BF16)
    expand = (jnp.arange(LANE)[:, None] == jnp.arange(t)[None, :] // SEL_BLOCK).astype(BF16)
    tok = lambda bi, qi: (bi * nq + qi, 0)
    seq = lambda bi, qi: (bi, 0)
    return pl.pallas_call(
        functools.partial(_nsa_prompt_kernel, n_cmp=n_cmp, n_sb=n_sb, k_sel=k_sel),
        grid=(b, nq),
        in_specs=[pl.BlockSpec((Q_BLOCK, NSA_HEADS * HEAD_DIM), tok),
                  pl.BlockSpec((Q_BLOCK, LANE), tok),
                  pl.BlockSpec((1, NSA_KV_HEADS * n_cmp, HEAD_DIM), lambda bi, qi: (bi, 0, 0)),
                  pl.BlockSpec((1, NSA_KV_HEADS * n_cmp, HEAD_DIM), lambda bi, qi: (bi, 0, 0)),
                  pl.BlockSpec((t, NSA_KV), seq), pl.BlockSpec((t, NSA_KV), seq),
                  pl.BlockSpec((t, NSA_KV), seq), pl.BlockSpec((t, NSA_KV), seq),
                  pl.BlockSpec(pool.shape, lambda bi, qi: (0, 0)),
                  pl.BlockSpec(expand.shape, lambda bi, qi: (0, 0))],
        out_specs=pl.BlockSpec((Q_BLOCK, NSA_HEADS * HEAD_DIM), tok),
        out_shape=jax.ShapeDtypeStruct((b * t, NSA_HEADS * HEAD_DIM), F32),
        scratch_shapes=[pltpu.VMEM((Q_BLOCK, t), F32)],
        compiler_params=_params("parallel", "arbitrary"),
        name="nsa_prompt_attn",
    )(q, gates, kc, vc, ks, vs, kw, vw, pool, expand)


def _fold_groups(x, rowg, width):
    out = jnp.where(rowg == 0, x[:, 0:width], 0.0)
    for g in range(1, NSA_KV_HEADS):
        out = out + jnp.where(rowg == g, x[:, g * width:(g + 1) * width], 0.0)
    return out


def _place_groups(x, rowg):
    return jnp.concatenate([jnp.where(rowg == g, x, 0.0) for g in range(NSA_KV_HEADS)], axis=1)


def _nsa_sample_kernel(pt_ref, *refs, n_pages, past, buf_len, n_cmp, n_sb, k_sel):
    del pt_ref
    ksp = refs[:n_pages]
    vsp = refs[n_pages:2 * n_pages]
    (q_ref, gt_ref, kc_ref, vc_ref, ksn_ref, vsn_ref, kwn_ref, vwn_ref, wk_ref, wv_ref,
     pool_ref, expand_ref, gsel_ref, gselt_ref, o_ref, wko_ref, wvo_ref) = refs[2 * n_pages:]
    nh = NSA_HEADS
    rowg = lax.broadcasted_iota(jnp.int32, (nh, 1), 0) // NSA_GROUP
    q16 = q_ref[0] * SCALE
    qmat_f = _place_groups(q16, rowg)
    qmat = qmat_f.astype(BF16)

    s_all = _dot_nt(q16.astype(BF16), kc_ref[0].astype(BF16))
    s_c = _fold_groups(s_all, rowg, n_cmp)
    ncol = lax.broadcasted_iota(jnp.int32, s_c.shape, 1)
    vis = ((ncol + 1) * CMP_BLOCK - 1) <= past
    p_c = _masked_softmax_rows(s_c, vis)
    o_c = _dot(_place_groups(p_c, rowg).astype(BF16), vc_ref[0].astype(BF16))

    gsel = gsel_ref[...]
    hi_, lo_ = _split2(p_c)
    imp = _dot(gsel, hi_) + _dot(gsel, lo_)
    hi_, lo_ = _split2(imp)
    imp_b = _dot(hi_, pool_ref[...]) + _dot(lo_, pool_ref[...])
    sel = _select_blocks(imp_b, past // SEL_BLOCK, n_sb, k_sel)
    sel16 = _dot(gselt_ref[...], sel.astype(BF16))
    maskfull = _dot(sel16.astype(BF16), expand_ref[...])

    def attend(scores, valids, s_new, valid_new, values, v_new):
        m = s_new if valid_new is None else jnp.where(valid_new > 0.5, s_new, NEG_INF)
        for s, vd in zip(scores, valids):
            m = jnp.maximum(m, jnp.max(jnp.where(vd > 0.5, s, NEG_INF), axis=-1, keepdims=True))
        p_new = jnp.exp(s_new - m)
        if valid_new is not None:
            p_new = p_new * valid_new
        l = p_new
        acc = p_new * v_new
        for s, vd, v in zip(scores, valids, values):
            p = jnp.exp(jnp.where(vd > 0.5, s, NEG_INF) - m) * vd
            l = l + jnp.sum(p, axis=-1, keepdims=True)
            acc = acc + _dot(p.astype(BF16), v)
        return acc / l

    scores, valids, values = [], [], []
    for j in range(n_pages):
        scores.append(_dot_nt(qmat, ksp[j][0].astype(BF16)))
        valids.append(maskfull[:, j * PAGE_SIZE:(j + 1) * PAGE_SIZE])
        values.append(vsp[j][0].astype(BF16))
    s_new = jnp.sum(qmat_f * ksn_ref[0], axis=-1, keepdims=True)
    o_s = attend(scores, valids, s_new, maskfull[:, past:past + 1], values, vsn_ref[0])
    o_s = _fold_groups(o_s, rowg, HEAD_DIM)

    wk = wk_ref[0]
    wv = wv_ref[0]
    s_w = _dot_nt(qmat, wk.astype(BF16))
    wcol = lax.broadcasted_iota(jnp.int32, s_w.shape, 1)
    pos_w = past - buf_len + wcol
    valid_w = jnp.where((past - pos_w < WINDOW) & (pos_w >= 0), 1.0, 0.0)
    s_new = jnp.sum(qmat_f * kwn_ref[0], axis=-1, keepdims=True)
    o_w = attend([s_w], [valid_w], s_new, None, [wv.astype(BF16)], vwn_ref[0])
    o_w = _fold_groups(o_w, rowg, HEAD_DIM)

    gt = gt_ref[0]
    o_ref[0] = gt[:, 0:1] * o_c + gt[:, 1:2] * o_s + gt[:, 2:3] * o_w

    wrow = lax.broadcasted_iota(jnp.int32, (buf_len, 1), 0)
    wko_ref[0] = jnp.where(wrow == buf_len - 1, kwn_ref[0], pltpu.roll(wk, buf_len - 1, 0))
    wvo_ref[0] = jnp.where(wrow == buf_len - 1, vwn_ref[0], pltpu.roll(wv, buf_len - 1, 0))


def nsa_sample_attn(page_table, pool_k, pool_v, base, q16, gates, kc, vc, ks_new, vs_new,
                    kw_new, vw_new, win_k, win_v, win_base):
    b, n_pages = page_table.shape
    past = n_pages * PAGE_SIZE
    buf_len = win_k.shape[1]
    tk = past + 1
    n_cmp = tk // CMP_BLOCK
    n_sb = -(-tk // SEL_BLOCK)
    k_sel = min(N_SEL, n_sb)
    ratio = SEL_BLOCK // CMP_BLOCK
    pool = (jnp.arange(n_cmp)[:, None] // ratio == jnp.arange(LANE)[None, :]).astype(BF16)
    expand = (jnp.arange(LANE)[:, None] == jnp.arange(past + LANE)[None, :] // SEL_BLOCK).astype(BF16)
    gsel = (jnp.arange(SUBLANE)[:, None] == jnp.arange(NSA_HEADS)[None, :] // NSA_GROUP).astype(BF16)
    gselt = gsel.T
    page = lambda j: pl.BlockSpec((1, PAGE_SIZE, NSA_KV), functools.partial(_page_map, j=j, base=base))
    per_b = lambda shape: pl.BlockSpec((1,) + shape, lambda i, pt: (i, 0, 0))
    const = lambda a: pl.BlockSpec(a.shape, lambda i, pt: (0, 0))
    win = pl.BlockSpec((1, buf_len, NSA_KV), lambda i, pt: (win_base + i, 0, 0))
    return pl.pallas_call(
        functools.partial(_nsa_sample_kernel, n_pages=n_pages, past=past, buf_len=buf_len,
                          n_cmp=n_cmp, n_sb=n_sb, k_sel=k_sel),
        grid_spec=pltpu.PrefetchScalarGridSpec(
            num_scalar_prefetch=1,
            grid=(b,),
            in_specs=([page(j) for j in range(n_pages)] + [page(j) for j in range(n_pages)]
                      + [per_b((NSA_HEADS, HEAD_DIM)), per_b((NSA_HEADS, 3)),
                         per_b((NSA_KV_HEADS * n_cmp, HEAD_DIM)), per_b((NSA_KV_HEADS * n_cmp, HEAD_DIM)),
                         per_b((1, NSA_KV)), per_b((1, NSA_KV)), per_b((1, NSA_KV)), per_b((1, NSA_KV)),
                         win, win, const(pool), const(expand), const(gsel), const(gselt)]),
            out_specs=[per_b((NSA_HEADS, HEAD_DIM)), per_b((buf_len, NSA_KV)), per_b((buf_len, NSA_KV))]),
        out_shape=[jax.ShapeDtypeStruct((b, NSA_HEADS, HEAD_DIM), F32),
                   jax.ShapeDtypeStruct((b, buf_len, NSA_KV), F32),
                   jax.ShapeDtypeStruct((b, buf_len, NSA_KV), F32)],
        compiler_params=_params("parallel"),
        name="nsa_sample_attn",
    )(page_table, *([pool_k] * n_pages), *([pool_v] * n_pages), q16, gates, kc, vc,
      ks_new, vs_new, kw_new, vw_new, win_k, win_v, pool, expand, gsel, gselt)


def _diff_lambda(lam_ref, lam_init):
    lv = lam_ref[...]
    a = jnp.sum(lv[0:1] * lv[1:2], axis=-1, keepdims=True)
    c = jnp.sum(lv[2:3] * lv[3:4], axis=-1, keepdims=True)
    return jnp.exp(a) - jnp.exp(c) + lam_init


def _diff_prompt_kernel(q_ref, k_ref, v_ref, lam_ref, sub_ref, o_ref, *, lam_init):
    qi = pl.program_id(1)
    qb = Q_BLOCK
    row = lax.broadcasted_iota(jnp.int32, (qb, 1), 0)
    pq = qi * qb + row
    lane = lax.broadcasted_iota(jnp.int32, (qb, LANE), 1)
    lam = _diff_lambda(lam_ref, lam_init)
    vw = 2 * HEAD_DIM
    for h in range(DIFF_HEADS):
        q0 = (q_ref[:, (2 * h) * HEAD_DIM:(2 * h + 1) * HEAD_DIM] * SCALE).astype(BF16)
        q1 = (q_ref[:, (2 * h + 1) * HEAD_DIM:(2 * h + 2) * HEAD_DIM] * SCALE).astype(BF16)

        def body(kt, carry):
            m0, l0, a0, m1, l1, a1 = carry
            off = pl.multiple_of(kt * LANE, LANE)
            kk = k_ref[pl.ds(off, LANE), (2 * h) * HEAD_DIM:(2 * h + 2) * HEAD_DIM].astype(BF16)
            v = v_ref[pl.ds(off, LANE), h * vw:(h + 1) * vw].astype(BF16)
            valid = (off + lane) <= pq

            def one(q, k, m, l, a):
                s = jnp.where(valid, _dot_nt(q, k), NEG_INF)
                m_new = jnp.maximum(m, jnp.max(s, axis=-1, keepdims=True))
                alpha = jnp.exp(m - m_new)
                p = jnp.exp(s - m_new)
                return m_new, alpha * l + jnp.sum(p, axis=-1, keepdims=True), alpha * a + _dot(p.astype(BF16), v)

            m0, l0, a0 = one(q0, kk[:, 0:HEAD_DIM], m0, l0, a0)
            m1, l1, a1 = one(q1, kk[:, HEAD_DIM:2 * HEAD_DIM], m1, l1, a1)
            return m0, l0, a0, m1, l1, a1

        neg = jnp.full((qb, 1), NEG_INF, F32)
        zero = jnp.zeros((qb, 1), F32)
        zacc = jnp.zeros((qb, vw), F32)
        _, l0, a0, _, l1, a1 = lax.fori_loop(0, qi + 1, body, (neg, zero, zacc, neg, zero, zacc))
        o = a0 / l0 - lam * (a1 / l1)
        o = _rms_rows(o, sub_ref[...]) * (1.0 - lam_init)
        o_ref[:, h * vw:(h + 1) * vw] = o


def diff_prompt_attn(q, k, v, lam_vec, sub_norm, lam_init, b, t):
    nq = t // Q_BLOCK
    tok = lambda bi, qi: (bi * nq + qi, 0)
    seq = lambda bi, qi: (bi, 0)
    return pl.pallas_call(
        functools.partial(_diff_prompt_kernel, lam_init=lam_init),
        grid=(b, nq),
        in_specs=[pl.BlockSpec((Q_BLOCK, D_MODEL), tok),
                  pl.BlockSpec((t, D_MODEL), seq), pl.BlockSpec((t, D_MODEL), seq),
                  pl.BlockSpec(lam_vec.shape, lambda bi, qi: (0, 0)),
                  pl.BlockSpec((1, 2 * HEAD_DIM), lambda bi, qi: (0, 0))],
        out_specs=pl.BlockSpec((Q_BLOCK, D_MODEL), tok),
        out_shape=jax.ShapeDtypeStruct((b * t, D_MODEL), F32),
        compiler_params=_params("parallel", "arbitrary"),
        name="diff_prompt_attn",
    )(q, k, v, lam_vec, sub_norm.reshape(1, 2 * HEAD_DIM))


def _diff_sample_kernel(pt_ref, *refs, n_pages, lam_init):
    del pt_ref
    kp = refs[:n_pages]
    vp = refs[n_pages:2 * n_pages]
    q_ref, kn_ref, vn_ref, lam_ref, sub_ref, o_ref = refs[2 * n_pages:]
    nmap = 2 * DIFF_HEADS
    vw = 2 * HEAD_DIM
    lam = _diff_lambda(lam_ref, lam_init)
    r = lax.broadcasted_iota(jnp.int32, (nmap, D_MODEL), 0)
    col = lax.broadcasted_iota(jnp.int32, (nmap, D_MODEL), 1)
    own = jnp.where(r < DIFF_HEADS, 2 * r, 2 * (r - DIFF_HEADS) + 1)
    qmat_f = jnp.where(col // HEAD_DIM == own, q_ref[0] * SCALE, 0.0)
    qmat = qmat_f.astype(BF16)
    scores = [_dot_nt(qmat, kp[j][0].astype(BF16)) for j in range(n_pages)]
    s_new = jnp.sum(qmat_f * kn_ref[0], axis=-1, keepdims=True)
    m = s_new
    for s in scores:
        m = jnp.maximum(m, jnp.max(s, axis=-1, keepdims=True))
    p_new = jnp.exp(s_new - m)
    ps = [jnp.exp(s - m) for s in scores]
    l = p_new
    for p in ps:
        l = l + jnp.sum(p, axis=-1, keepdims=True)
    inv = 1.0 / l
    pd_new = (p_new * inv)[0:DIFF_HEADS] - lam * (p_new * inv)[DIFF_HEADS:nmap]
    acc = pd_new * vn_ref[0]
    for j in range(n_pages):
        pn = ps[j] * inv
        pd = pn[0:DIFF_HEADS] - lam * pn[DIFF_HEADS:nmap]
        acc = acc + _dot(pd.astype(BF16), vp[j][0].astype(BF16))
    rr = lax.broadcasted_iota(jnp.int32, (DIFF_HEADS, D_MODEL), 0)
    cc = lax.broadcasted_iota(jnp.int32, (DIFF_HEADS, D_MODEL), 1)
    om = jnp.where(cc // vw == rr, acc, 0.0)
    ss = jnp.sum(om * om, axis=-1, keepdims=True) * (1.0 / vw)
    on = om * lax.rsqrt(ss + RMS_EPS)
    o_ref[0] = jnp.sum(on, axis=0, keepdims=True) * sub_ref[...] * (1.0 - lam_init)


def diff_sample_attn(page_table, pool_k, pool_v, base, q, k_new, v_new, lam_vec, sub_row, lam_init):
    b, n_pages = page_table.shape
    page = lambda j: pl.BlockSpec((1, PAGE_SIZE, D_MODEL), functools.partial(_page_map, j=j, base=base))
    per_b = pl.BlockSpec((1, 1, D_MODEL), lambda i, pt: (i, 0, 0))
    return pl.pallas_call(
        functools.partial(_diff_sample_kernel, n_pages=n_pages, lam_init=lam_init),
        grid_spec=pltpu.PrefetchScalarGridSpec(
            num_scalar_prefetch=1,
            grid=(b,),
            in_specs=([page(j) for j in range(n_pages)] + [page(j) for j in range(n_pages)]
                      + [per_b, per_b, per_b,
                         pl.BlockSpec(lam_vec.shape, lambda i, pt: (0, 0)),
                         pl.BlockSpec((1, D_MODEL), lambda i, pt: (0, 0))]),
            out_specs=per_b),
        out_shape=jax.ShapeDtypeStruct((b, 1, D_MODEL), F32),
        compiler_params=_params("parallel"),
        name="diff_sample_attn",
    )(page_table, *([pool_k] * n_pages), *([pool_v] * n_pages), q, k_new, v_new, lam_vec, sub_row)


def _shift_rows(x, s, carry, row):
    r = pltpu.roll(x, s, 0)
    for i in range(s):
        r = jnp.where(row == i, carry[SUBLANE - s + i:SUBLANE - s + i + 1, :], r)
    return r


def _conv_silu_seq_kernel(x_ref, w_ref, b_ref, o_ref, carry_ref, *, width):
    @pl.when(pl.program_id(2) == 0)
    def _():
        carry_ref[...] = jnp.zeros(carry_ref.shape, F32)

    x = x_ref[...]
    tm = x.shape[0]
    row = lax.broadcasted_iota(jnp.int32, (tm, 1), 0)
    carry = carry_ref[...]
    acc = x * w_ref[width - 1:width, :]
    for s in range(1, width):
        acc = acc + _shift_rows(x, s, carry, row) * w_ref[width - 1 - s:width - s, :]
    acc = acc + b_ref[...]
    o_ref[...] = _silu(acc)
    carry_ref[...] = x[tm - SUBLANE:tm, :]


def ssd_conv_prompt(proj, conv_w, conv_b, b, t, tm):
    cb = 1024
    nt = t // tm
    c0 = SSD_D_INNER // cb
    return pl.pallas_call(
        functools.partial(_conv_silu_seq_kernel, width=SSD_CONV_W),
        grid=(b, SSD_CONV_DIM // cb, nt),
        in_specs=[pl.BlockSpec((tm, cb), lambda bi, j, ti: (bi * nt + ti, c0 + j)),
                  pl.BlockSpec((SSD_CONV_W, cb), lambda bi, j, ti: (0, j)),
                  pl.BlockSpec((1, cb), lambda bi, j, ti: (0, j))],
        out_specs=pl.BlockSpec((tm, cb), lambda bi, j, ti: (bi * nt + ti, j)),
        out_shape=jax.ShapeDtypeStruct((b * t, SSD_CONV_DIM), F32),
        scratch_shapes=[pltpu.VMEM((SUBLANE, cb), F32)],
        compiler_params=_params("parallel", "parallel", "arbitrary"),
        name="ssd_conv_prompt",
    )(proj, conv_w, conv_b.reshape(1, SSD_CONV_DIM))


def _conv_silu_state_kernel(x_ref, p0_ref, p1_ref, p2_ref, w_ref, b_ref, o_ref):
    acc = (p0_ref[...] * w_ref[0:1, :] + p1_ref[...] * w_ref[1:2, :] + p2_ref[...] * w_ref[2:3, :]
           + x_ref[...] * w_ref[3:4, :] + b_ref[...])
    o_ref[...] = _silu(acc)


def ssd_conv_sample(proj, prev, conv_w, conv_b):
    b = proj.shape[0]
    cb = 1024
    c0 = SSD_D_INNER // cb
    col = lambda j: (0, j)
    return pl.pallas_call(
        _conv_silu_state_kernel,
        grid=(SSD_CONV_DIM // cb,),
        in_specs=[pl.BlockSpec((b, cb), lambda j: (0, c0 + j)),
                  pl.BlockSpec((b, cb), col), pl.BlockSpec((b, cb), col), pl.BlockSpec((b, cb), col),
                  pl.BlockSpec((SSD_CONV_W, cb), col), pl.BlockSpec((1, cb), col)],
        out_specs=pl.BlockSpec((b, cb), col),
        out_shape=jax.ShapeDtypeStruct((b, SSD_CONV_DIM), F32),
        compiler_params=_params("parallel"),
        name="ssd_conv_sample",
    )(proj, prev[:, 0], prev[:, 1], prev[:, 2], conv_w, conv_b.reshape(1, SSD_CONV_DIM))


def _ssd_scan_kernel(xbc_ref, dt_ref, bias_ref, alog_ref, dskip_ref, tril_ref, exp_ref,
                     y_ref, st_ref, state_ref):
    @pl.when(pl.program_id(1) == 0)
    def _():
        state_ref[...] = jnp.zeros(state_ref.shape, F32)

    l = SSD_CHUNK
    di = SSD_D_INNER
    n = SSD_D_STATE
    tril = tril_ref[...]
    expm = exp_ref[...]
    dt = _softplus(dt_ref[...] + bias_ref[...])
    a = dt * (-jnp.exp(alog_ref[...]))
    acs = _lhs_exact_dot(tril, a)
    acs_t = acs.T
    dtx = _dot_exact_rhs(dt, expm)
    eacs = jnp.exp(acs)
    eacsx = _dot_exact_rhs(eacs, expm)
    decx = _dot_exact_rhs(jnp.exp(acs[l - 1:l, :] - acs), expm)
    x = xbc_ref[:, 0:di]
    xdt = x * dtx
    xw = (xdt * decx).astype(BF16)
    xdt_b = xdt.astype(BF16)
    ri = lax.broadcasted_iota(jnp.int32, (l, l), 0)
    ci = lax.broadcasted_iota(jnp.int32, (l, l), 1)
    lower = ri >= ci
    gw = SSD_GROUP_W
    for g in range(SSD_GROUPS):
        bg = xbc_ref[:, di + g * n:di + (g + 1) * n]
        cg = xbc_ref[:, di + SSD_GROUPS * n + g * n:di + SSD_GROUPS * n + (g + 1) * n].astype(BF16)
        cb = _dot_nt(cg, bg.astype(BF16))
        st_g = state_ref[:, g * gw:(g + 1) * gw]
        y_off = _dot(cg, st_g.astype(BF16)) * eacsx[:, g * gw:(g + 1) * gw]
        state_ref[:, g * gw:(g + 1) * gw] = (
            st_g * eacsx[l - 1:l, g * gw:(g + 1) * gw] + _dot(bg.T.astype(BF16), xw[:, g * gw:(g + 1) * gw]))
        for hh in range(SSD_HEADS_PER_GROUP):
            h = g * SSD_HEADS_PER_GROUP + hh
            cols = slice(h * SSD_HEADDIM, (h + 1) * SSD_HEADDIM)
            seg = acs[:, h:h + 1] - acs_t[h:h + 1, :]
            lmat = jnp.exp(jnp.where(lower, seg, NEG_INF))
            yd = _dot((cb * lmat).astype(BF16), xdt_b[:, cols])
            y_ref[:, cols] = (yd + y_off[:, hh * SSD_HEADDIM:(hh + 1) * SSD_HEADDIM]
                              + dskip_ref[:, cols] * x[:, cols])
    st_ref[0] = state_ref[...]


def ssd_scan_prompt(xbc, proj, dt_bias_row, a_log_row, dskip_row, b, t):
    nc = t // SSD_CHUNK
    l = SSD_CHUNK
    tril = (jnp.arange(l)[:, None] >= jnp.arange(l)[None, :]).astype(BF16)
    expm = (jnp.arange(LANE)[:, None] == jnp.arange(SSD_D_INNER)[None, :] // SSD_HEADDIM).astype(BF16)
    dt_blk = (SSD_D_INNER + SSD_CONV_DIM) // LANE
    tok = lambda bi, ci: (bi * nc + ci, 0)
    const = lambda a: pl.BlockSpec(a.shape, lambda bi, ci: (0, 0))
    return pl.pallas_call(
        _ssd_scan_kernel,
        grid=(b, nc),
        in_specs=[pl.BlockSpec((l, SSD_CONV_DIM), tok),
                  pl.BlockSpec((l, LANE), lambda bi, ci: (bi * nc + ci, dt_blk)),
                  const(dt_bias_row), const(a_log_row), const(dskip_row), const(tril), const(expm)],
        out_specs=[pl.BlockSpec((l, SSD_D_INNER), tok),
                   pl.BlockSpec((1, SSD_D_STATE, SSD_D_INNER), lambda bi, ci: (bi, 0, 0))],
        out_shape=[jax.ShapeDtypeStruct((b * t, SSD_D_INNER), F32),
                   jax.ShapeDtypeStruct((b, SSD_D_STATE, SSD_D_INNER), F32)],
        scratch_shapes=[pltpu.VMEM((SSD_D_STATE, SSD_D_INNER), F32)],
        compiler_params=_params("parallel", "arbitrary"),
        name="ssd_scan_prompt",
    )(xbc, proj, dt_bias_row, a_log_row, dskip_row, tril, expm)


def _ssd_step_kernel(xbc_ref, dt_ref, bias_ref, alog_ref, dskip_ref, s_ref, y_ref, so_ref):
    di = SSD_D_INNER
    n = SSD_D_STATE
    p = SSD_HEADDIM
    dt = _softplus(dt_ref[0] + bias_ref[...])
    dec = jnp.exp(dt * (-jnp.exp(alog_ref[...])))
    ri = lax.broadcasted_iota(jnp.int32, (p, p), 0)
    ci = lax.broadcasted_iota(jnp.int32, (p, p), 1)
    eye = ri == ci
    for g in range(SSD_GROUPS):
        bmat = jnp.broadcast_to(xbc_ref[0, :, di + g * n:di + (g + 1) * n], (p, n)).astype(BF16)
        c8 = jnp.broadcast_to(xbc_ref[0, :, di + SSD_GROUPS * n + g * n:di + SSD_GROUPS * n + (g + 1) * n],
                              (SUBLANE, n)).astype(BF16)
        for hh in range(SSD_HEADS_PER_GROUP):
            h = g * SSD_HEADS_PER_GROUP + hh
            cols = slice(h * p, (h + 1) * p)
            xh = xbc_ref[0, :, cols]
            xdt = xh * dt[:, h:h + 1]
            diag = jnp.where(eye, jnp.broadcast_to(xdt, (p, p)), 0.0)
            hi_, lo_ = _split2(diag)
            s_new = s_ref[0, h] * dec[:, h:h + 1] + _dot(hi_, bmat) + _dot(lo_, bmat)
            so_ref[0, h] = s_new
            yh = _dot_nt(c8, s_new.astype(BF16))
            y_ref[0, :, cols] = yh[0:1] + dskip_ref[:, cols] * xh


def ssd_step_sample(xbc, proj, dt_bias_row, a_log_row, dskip_row, state, state_base):
    b = xbc.shape[0]
    dt_blk = (SSD_D_INNER + SSD_CONV_DIM) // LANE
    const = lambda a: pl.BlockSpec(a.shape, lambda i: (0, 0))
    st_shape = (1, SSD_HEADS, SSD_HEADDIM, SSD_D_STATE)
    return pl.pallas_call(
        _ssd_step_kernel,
        grid=(b,),
        in_specs=[pl.BlockSpec((1, 1, SSD_CONV_DIM), lambda i: (i, 0, 0)),
                  pl.BlockSpec((1, 1, LANE), lambda i: (i, 0, dt_blk)),
                  const(dt_bias_row), const(a_log_row), const(dskip_row),
                  pl.BlockSpec(st_shape, lambda i: (state_base + i, 0, 0, 0))],
        out_specs=[pl.BlockSpec((1, 1, SSD_D_INNER), lambda i: (i, 0, 0)),
                   pl.BlockSpec(st_shape, lambda i: (i, 0, 0, 0))],
        out_shape=[jax.ShapeDtypeStruct((b, 1, SSD_D_INNER), F32),
                   jax.ShapeDtypeStruct((b,) + st_shape[1:], F32)],
        compiler_params=_params("parallel"),
        name="ssd_step_sample",
    )(xbc.reshape(b, 1, SSD_CONV_DIM), proj.reshape(b, 1, proj.shape[1]),
      dt_bias_row, a_log_row, dskip_row, state)


def _ssd_out_kernel(y_ref, z_ref, ng_ref, w_ref, r_ref, o_ref):
    gated = y_ref[...] * _silu(z_ref[...])
    parts = []
    for g in range(SSD_GROUPS):
        cols = slice(g * SSD_GROUP_W, (g + 1) * SSD_GROUP_W)
        parts.append(_rms_rows(gated[:, cols], ng_ref[:, cols]).astype(BF16))
    o_ref[...] = r_ref[...] + _dot(jnp.concatenate(parts, axis=1), w_ref[...])


def ssd_out(y, proj, norm_g, w, res, tm):
    m = y.shape[0]
    di = SSD_D_INNER
    return pl.pallas_call(
        _ssd_out_kernel,
        grid=(m // tm,),
        in_specs=[pl.BlockSpec((tm, di), lambda i: (i, 0)),
                  pl.BlockSpec((tm, di), lambda i: (i, 0)),
                  pl.BlockSpec((1, di), lambda i: (0, 0)),
                  pl.BlockSpec((di, D_MODEL), lambda i: (0, 0)),
                  pl.BlockSpec((tm, D_MODEL), lambda i: (i, 0))],
        out_specs=pl.BlockSpec((tm, D_MODEL), lambda i: (i, 0)),
        out_shape=jax.ShapeDtypeStruct((m, D_MODEL), F32),
        compiler_params=_params("parallel"),
        name="ssd_out",
    )(y, proj, norm_g.reshape(1, di), w, res)


def _ffn_seq_kernel(x_ref, g_ref, wup_ref, cw_ref, wdn_ref, o_ref, st_ref, carry_ref):
    @pl.when(pl.program_id(1) == 0)
    def _():
        carry_ref[...] = jnp.zeros(carry_ref.shape, F32)

    x = x_ref[...]
    tm = x.shape[0]
    h = _rms_rows(x, g_ref[...]).astype(BF16)
    row = lax.broadcasted_iota(jnp.int32, (tm, 1), 0)
    fc = FFN_CHUNK

    def conv(u, c0):
        w = cw_ref[:, c0:c0 + fc]
        carry = carry_ref[:, c0:c0 + fc]
        y = u * w[2:3] + _shift_rows(u, 1, carry, row) * w[1:2] + _shift_rows(u, 2, carry, row) * w[0:1]
        carry_ref[:, c0:c0 + fc] = u[tm - SUBLANE:tm, :]
        return y

    acc = jnp.zeros((tm, D_MODEL), F32)
    for c0 in range(0, D_FF, fc):
        u = conv(_dot(h, wup_ref[:, c0:c0 + fc]), c0)
        gate = conv(_dot(h, wup_ref[:, D_FF + c0:D_FF + c0 + fc]), D_FF + c0)
        acc = acc + _dot((_silu(gate) * u).astype(BF16), wdn_ref[c0:c0 + fc, :])
    o_ref[...] = x + acc
    st_ref[0] = carry_ref[...]


def ffn_prompt(x, g, w_up, conv_w, w_down, b, t, tm):
    nt = t // tm
    tok = lambda bi, ti: (bi * nt + ti, 0)
    const = lambda a: pl.BlockSpec(a.shape, lambda bi, ti: (0, 0))
    g = g.reshape(1, D_MODEL)
    return pl.pallas_call(
        _ffn_seq_kernel,
        grid=(b, nt),
        in_specs=[pl.BlockSpec((tm, D_MODEL), tok), const(g), const(w_up), const(conv_w), const(w_down)],
        out_specs=[pl.BlockSpec((tm, D_MODEL), tok),
                   pl.BlockSpec((1, SUBLANE, 2 * D_FF), lambda bi, ti: (bi, 0, 0))],
        out_shape=[jax.ShapeDtypeStruct((b * t, D_MODEL), F32),
                   jax.ShapeDtypeStruct((b, SUBLANE, 2 * D_FF), F32)],
        scratch_shapes=[pltpu.VMEM((SUBLANE, 2 * D_FF), F32)],
        compiler_params=_params("parallel", "arbitrary"),
        name="ffn_prompt",
    )(x, g, w_up, conv_w, w_down)


def _ffn_state_kernel(x_ref, g_ref, wup_ref, cw_ref, wdn_ref, p0_ref, p1_ref, o_ref, up_ref):
    x = x_ref[...]
    h = _rms_rows(x, g_ref[...]).astype(BF16)
    fc = FFN_CHUNK

    def conv(u, c0):
        up_ref[:, c0:c0 + fc] = u
        w = cw_ref[:, c0:c0 + fc]
        return u * w[2:3] + p1_ref[:, c0:c0 + fc] * w[1:2] + p0_ref[:, c0:c0 + fc] * w[0:1]

    acc = jnp.zeros(x.shape, F32)
    for c0 in range(0, D_FF, fc):
        u = conv(_dot(h, wup_ref[:, c0:c0 + fc]), c0)
        gate = conv(_dot(h, wup_ref[:, D_FF + c0:D_FF + c0 + fc]), D_FF + c0)
        acc = acc + _dot((_silu(gate) * u).astype(BF16), wdn_ref[c0:c0 + fc, :])
    o_ref[...] = x + acc


def ffn_sample(x, g, w_up, conv_w, w_down, prev):
    b = x.shape[0]
    full = lambda a: pl.BlockSpec(a.shape, lambda i: (0,) * a.ndim)
    g = g.reshape(1, D_MODEL)
    p0, p1 = prev[:, 0], prev[:, 1]
    return pl.pallas_call(
        _ffn_state_kernel,
        grid=(1,),
        in_specs=[full(x), full(g), full(w_up), full(conv_w), full(w_down), full(p0), full(p1)],
        out_specs=[pl.BlockSpec((b, D_MODEL), lambda i: (0, 0)),
                   pl.BlockSpec((b, 2 * D_FF), lambda i: (0, 0))],
        out_shape=[jax.ShapeDtypeStruct((b, D_MODEL), F32),
                   jax.ShapeDtypeStruct((b, 2 * D_FF), F32)],
        compiler_params=_params("arbitrary"),
        name="ffn_sample",
    )(x, g, w_up, conv_w, w_down, p0, p1)


def _rope_tables(pos, rows):
    half = HEAD_DIM // 2
    inv_freq = ROPE_THETA ** (-jnp.arange(half, dtype=F32) / half)
    ang = pos.astype(F32)[:, None] * inv_freq[None, :]
    cos = jnp.cos(ang)
    sin = jnp.sin(ang)
    cos = jnp.tile(jnp.concatenate([cos, cos], axis=-1), (1, 4))
    sin = jnp.tile(jnp.concatenate([-sin, sin], axis=-1), (1, 4))
    if cos.shape[0] != rows:
        cos = jnp.broadcast_to(cos, (rows, 256))
        sin = jnp.broadcast_to(sin, (rows, 256))
    return cos, sin


def _block_diag_ones():
    i = jnp.arange(256)
    return (i[:, None] // HEAD_DIM == i[None, :] // HEAD_DIM).astype(BF16)


def _pad_cols(w, n):
    return jnp.pad(w, ((0, 0), (0, n - w.shape[1])))


def _nsa_weights(w_in, q_norm, k_norm, cmp_pos, cmp_w1, cmp_w2, w_out):
    gain = jnp.concatenate(
        [jnp.tile(q_norm, NSA_HEADS)]
        + [jnp.tile(k_norm[br], NSA_KV_HEADS) if kv == 0 else jnp.ones((NSA_KV,), F32)
           for br in range(3) for kv in range(2)]).reshape(1, NSA_QKV)
    pos_rows = [jnp.broadcast_to(cmp_pos[i][:, None, :], (CMP_BLOCK, NSA_KV_HEADS, HEAD_DIM))
                .reshape(1, CMP_BLOCK * NSA_KV) for i in range(2)]
    return dict(w_in=_pad_cols(w_in, NSA_IN_PAD).astype(BF16), gain=gain, pos=pos_rows,
                w1=[cmp_w1[i].astype(BF16) for i in range(2)],
                w2=[cmp_w2[i].astype(BF16) for i in range(2)],
                w_out=w_out.astype(BF16))


def _nsa_prompt_layer(x, gmix, w, tabs, bd, b, t, tm):
    proj = norm_matmul(x, gmix, w["w_in"], tm)
    q, kc_rows, vc_rows, ks, vs, kw, vw, gates = post_proj(proj, w["gain"], tabs[0], tabs[1], bd,
                                                            NSA_PLAN, NSA_OUT_W, tm)
    n_cmp = t // CMP_BLOCK
    wide = CMP_BLOCK * NSA_KV
    kc = compress_rows(kc_rows[:b * n_cmp * CMP_BLOCK].reshape(b, n_cmp, wide), w["pos"][0], w["w1"][0], w["w2"][0], n_cmp)
    vc = compress_rows(vc_rows[:b * n_cmp * CMP_BLOCK].reshape(b, n_cmp, wide), w["pos"][1], w["w1"][1], w["w2"][1], n_cmp)
    o = nsa_prompt_attn(q, gates, kc, vc, ks, vs, kw, vw, b, t)
    x = matmul_res(o, w["w_out"], x, tm)
    kvshape = (b, t, NSA_KV_HEADS, HEAD_DIM)
    keep = min(WINDOW, t)
    rows = tuple(a.reshape(kvshape) for a in (kc_rows, vc_rows, ks, vs))
    wins = tuple(a.reshape(kvshape)[:, t - keep:] for a in (kw, vw))
    return x, rows + wins


def _nsa_sample_layer(x, gmix, w, tabs, bd, j, page_table, caches, win_k, win_v):
    b = x.shape[0]
    n_phys = caches[0].shape[1]
    proj = norm_matmul(x, gmix, w["w_in"], b)
    q, kc_new, vc_new, ks_new, vs_new, kw_new, vw_new, gates = post_proj(
        proj, w["gain"], tabs[0], tabs[1], bd, NSA_PLAN, NSA_OUT_W, b)
    wide = CMP_BLOCK * NSA_KV
    per_page = PAGE_SIZE // CMP_BLOCK
    pool_ck, pool_cv = (c.reshape(-1, per_page, wide) for c in caches[:2])
    pool_sk, pool_sv = (c.reshape(-1, PAGE_SIZE, NSA_KV) for c in caches[2:])
    kc = compress_pages(pool_ck, page_table, j * n_phys, w["pos"][0], w["w1"][0], w["w2"][0])
    vc = compress_pages(pool_cv, page_table, j * n_phys, w["pos"][1], w["w1"][1], w["w2"][1])
    buf_len = win_k.shape[2]
    row3 = lambda a: a.reshape(b, 1, NSA_KV)
    o16, wk_out, wv_out = nsa_sample_attn(
        page_table, pool_sk, pool_sv, j * n_phys,
        q.reshape(b, NSA_HEADS, HEAD_DIM), gates[:, :NSA_HEADS * 3].reshape(b, NSA_HEADS, 3), kc, vc,
        row3(ks_new), row3(vs_new), row3(kw_new), row3(vw_new),
        win_k.reshape(-1, buf_len, NSA_KV), win_v.reshape(-1, buf_len, NSA_KV), j * b)
    x = matmul_res(o16.reshape(b, NSA_HEADS * HEAD_DIM), w["w_out"], x, b)
    kvshape = (b, 1, NSA_KV_HEADS, HEAD_DIM)
    rows = tuple(a.reshape(kvshape) for a in (kc_new, vc_new, ks_new, vs_new))
    wins = tuple(a.reshape(b, buf_len, NSA_KV_HEADS, HEAD_DIM) for a in (wk_out, wv_out))
    return x, rows + wins


def _diff_weights(w_in, q_norm, k_norm, lam_vec, sub_norm, w_out):
    nmap = 2 * DIFF_HEADS
    gain = jnp.concatenate([jnp.tile(q_norm, nmap), jnp.tile(k_norm, nmap),
                            jnp.ones((D_MODEL,), F32)]).reshape(1, 3 * D_MODEL)
    return dict(w_in=w_in.astype(BF16), gain=gain, lam=lam_vec, sub=sub_norm, w_out=w_out.astype(BF16))


def kernel(x_prompt, x_sample, cache_nsa_cmp_k, cache_nsa_cmp_v, cache_nsa_slc_k, cache_nsa_slc_v, state_nsa_win_k, state_nsa_win_v, cache_diff_k, cache_diff_v, state_ssd_conv, state_ssd_ssm, state_ffn_conv, page_table, norm_mix, norm_ffn, nsa_w_in, nsa_q_norm, nsa_k_norm, nsa_cmp_pos, nsa_cmp_w1, nsa_cmp_w2, nsa_w_out, diff_w_in, diff_q_norm, diff_k_norm, diff_lambda, diff_sub_norm, diff_w_out, ssd_w_in, ssd_conv_w, ssd_conv_b, ssd_dt_bias, ssd_a_log, ssd_d, ssd_norm, ssd_w_out, ffn_w_up, ffn_conv_w, ffn_w_down):
    bp, tp, _ = x_prompt.shape
    bs, ts, _ = x_sample.shape
    assert ts == 1 and tp % Q_BLOCK == 0
    n_pages = page_table.shape[1]
    past = n_pages * PAGE_SIZE
    depth = norm_mix.shape[0]
    tm_p = min(256, tp)
    xp = x_prompt.reshape(bp * tp, D_MODEL)
    xs = x_sample.reshape(bs, D_MODEL)
    bd = _block_diag_ones()
    tabs_p = _rope_tables(jnp.arange(tp, dtype=jnp.int32), tp)
    tabs_s = _rope_tables(jnp.full((1,), past, jnp.int32), bs)
    nsa_p, nsa_s, diff_p, diff_s, ssd_p, ssd_s, ffn_p, ffn_s = [], [], [], [], [], [], [], []
    for i in range(depth):
        kind = i % N_MIXERS
        j = i // N_MIXERS
        if kind == 0:
            w = _nsa_weights(nsa_w_in[j], nsa_q_norm[j], nsa_k_norm[j], nsa_cmp_pos[j], nsa_cmp_w1[j],
                             nsa_cmp_w2[j], nsa_w_out[j])
            xp, outs = _nsa_prompt_layer(xp, norm_mix[i], w, tabs_p, bd, bp, tp, tm_p)
            nsa_p.append(outs)
            xs, outs = _nsa_sample_layer(xs, norm_mix[i], w, tabs_s, bd, j, page_table,
                                         (cache_nsa_cmp_k, cache_nsa_cmp_v, cache_nsa_slc_k, cache_nsa_slc_v),
                                         state_nsa_win_k, state_nsa_win_v)
            nsa_s.append(outs)
        elif kind == 1:
            lam_init = 0.8 - 0.6 * math.exp(-0.3 * i)
            w = _diff_weights(diff_w_in[j], diff_q_norm[j], diff_k_norm[j], diff_lambda[j], diff_sub_norm[j],
                              diff_w_out[j])
            proj = norm_matmul(xp, norm_mix[i], w["w_in"], tm_p)
            q, k, v = post_proj(proj, w["gain"], tabs_p[0], tabs_p[1], bd, DIFF_PLAN, DIFF_OUT_W, tm_p)
            o = diff_prompt_attn(q, k, v, w["lam"], w["sub"], lam_init, bp, tp)
            xp = matmul_res(o, w["w_out"], xp, tm_p)
            diff_p.append((k.reshape(bp, tp, DIFF_HEADS, 2, HEAD_DIM), v.reshape(bp, tp, DIFF_HEADS, 2 * HEAD_DIM)))
            n_phys = cache_diff_k.shape[1]
            proj = norm_matmul(xs, norm_mix[i], w["w_in"], bs)
            q, k, v = post_proj(proj, w["gain"], tabs_s[0], tabs_s[1], bd, DIFF_PLAN, DIFF_OUT_W, bs)
            r3 = lambda a: a.reshape(bs, 1, D_MODEL)
            o = diff_sample_attn(page_table, cache_diff_k.reshape(-1, PAGE_SIZE, D_MODEL),
                                 cache_diff_v.reshape(-1, PAGE_SIZE, D_MODEL), j * n_phys,
                                 r3(q), r3(k), r3(v), w["lam"],
                                 jnp.tile(w["sub"], DIFF_HEADS).reshape(1, D_MODEL), lam_init)
            xs = matmul_res(o.reshape(bs, D_MODEL), w["w_out"], xs, bs)
            diff_s.append((k.reshape(bs, 1, DIFF_HEADS, 2, HEAD_DIM), v.reshape(bs, 1, DIFF_HEADS, 2 * HEAD_DIM)))
        else:
            w_in = _pad_cols(ssd_w_in[j], SSD_IN_PAD).astype(BF16)
            w_out = ssd_w_out[j].astype(BF16)
            pad_h = lambda a: jnp.pad(a, (0, LANE - SSD_HEADS)).reshape(1, LANE)
            bias_row, alog_row = pad_h(ssd_dt_bias[j]), pad_h(ssd_a_log[j])
            dskip_row = jnp.repeat(ssd_d[j], SSD_HEADDIM).reshape(1, SSD_D_INNER)
            xbc0 = SSD_D_INNER
            proj = norm_matmul(xp, norm_mix[i], w_in, tm_p)
            xbc = ssd_conv_prompt(proj, ssd_conv_w[j], ssd_conv_b[j], bp, tp, tm_p)
            y, st = ssd_scan_prompt(xbc, proj, bias_row, alog_row, dskip_row, bp, tp)
            xp = ssd_out(y, proj, ssd_norm[j], w_out, xp, tm_p)
            conv_new = proj.reshape(bp, tp, SSD_IN_PAD)[:, tp - (SSD_CONV_W - 1):, xbc0:xbc0 + SSD_CONV_DIM]
            ssm_new = st.reshape(bp, SSD_D_STATE, SSD_HEADS, SSD_HEADDIM).transpose(0, 2, 3, 1)
            ssd_p.append((conv_new, ssm_new))
            proj = norm_matmul(xs, norm_mix[i], w_in, bs)
            prev = state_ssd_conv[j]
            xbc = ssd_conv_sample(proj, prev, ssd_conv_w[j], ssd_conv_b[j])
            y, st = ssd_step_sample(xbc, proj, bias_row, alog_row, dskip_row,
                                    state_ssd_ssm.reshape((-1,) + state_ssd_ssm.shape[2:]), j * bs)
            xs = ssd_out(y.reshape(bs, SSD_D_INNER), proj, ssd_norm[j], w_out, xs, bs)
            conv_new = jnp.concatenate([prev[:, 1:], proj[:, None, xbc0:xbc0 + SSD_CONV_DIM]], axis=1)
            ssd_s.append((conv_new, st))
        w_up = ffn_w_up[i].astype(BF16)
        w_dn = ffn_w_down[i].astype(BF16)
        xp, st = ffn_prompt(xp, norm_ffn[i], w_up, ffn_conv_w[i], w_dn, bp, tp, tm_p)
        ffn_p.append(st[:, SUBLANE - (FFN_CONV_W - 1):])
        prev = state_ffn_conv[i]
        xs, up = ffn_sample(xs, norm_ffn[i], w_up, ffn_conv_w[i], w_dn, prev)
        ffn_s.append(jnp.concatenate([prev[:, 1:], up[:, None]], axis=1))
    outs = [xp.reshape(bp, tp, D_MODEL), xs.reshape(bs, ts, D_MODEL)]
    for r in range(6):
        outs.append(jnp.stack([o[r] for o in nsa_p]))
        outs.append(jnp.stack([o[r] for o in nsa_s]))
    for r in range(2):
        outs.append(jnp.stack([o[r] for o in diff_p]))
        outs.append(jnp.stack([o[r] for o in diff_s]))
    for r in range(2):
        outs.append(jnp.stack([o[r] for o in ssd_p]))
        outs.append(jnp.stack([o[r] for o in ssd_s]))
    outs.append(jnp.stack(ffn_p))
    outs.append(jnp.stack(ffn_s))
    return tuple(outs)
```

```python
import functools
import math

import jax
import jax.numpy as jnp
from jax import lax
from jax.experimental import pallas as pl
from jax.experimental.pallas import tpu as pltpu

F32 = jnp.float32
BF16 = jnp.bfloat16

D_MODEL = 1024
DEPTH = 4
PAGE_SIZE = 128
N_MIXERS = 3
RMS_EPS = 1e-6
ROPE_THETA = 10000.0
Q_BLOCK = 128
NEG_INF = -1e30
HEAD_DIM = 64
SCALE = HEAD_DIM ** -0.5

NSA_HEADS = D_MODEL // HEAD_DIM
NSA_KV_HEADS = 4
NSA_GROUP = NSA_HEADS // NSA_KV_HEADS
CMP_BLOCK = 32
SEL_BLOCK = 64
N_SEL = 8
WINDOW = 512
FORCED_SCORE = 1e4
NSA_KEY_TILE = 256
NSA_KV = NSA_KV_HEADS * HEAD_DIM
NSA_QKV = NSA_HEADS * HEAD_DIM + 6 * NSA_KV
NSA_IN_PAD = NSA_QKV + 128

DIFF_HEADS = D_MODEL // (2 * HEAD_DIM)
DIFF_Q_BLOCK = 512

SSD_D_INNER = 2 * D_MODEL
SSD_HEADDIM = 64
SSD_HEADS = SSD_D_INNER // SSD_HEADDIM
SSD_GROUPS = 4
SSD_D_STATE = 128
SSD_CONV_W = 4
SSD_CHUNK = 128
SSD_CONV_DIM = SSD_D_INNER + 2 * SSD_GROUPS * SSD_D_STATE
SSD_IN_PAD = SSD_D_INNER + SSD_CONV_DIM + 128
SSD_GROUP_W = SSD_D_INNER // SSD_GROUPS
SSD_HEADS_PER_GROUP = SSD_HEADS // SSD_GROUPS

D_FF = 2816
FFN_CONV_W = 3
FFN_CHUNK = 256

LANE = 128
SUBLANE = 8
VMEM_LIMIT = 56 * 1024 * 1024
REMOVED = -3.4e38
INVALID = -3.2e38

NT_DIMS = (((1,), (1,)), ((), ()))


def _params(*sem):
    return pltpu.CompilerParams(dimension_semantics=sem, vmem_limit_bytes=VMEM_LIMIT)


def _dot(a, b):
    return jnp.dot(a, b, preferred_element_type=F32)


def _dot_nt(a, b):
    return lax.dot_general(a, b, NT_DIMS, preferred_element_type=F32)


def _split2(x):
    hi = x.astype(BF16)
    lo = (x - hi.astype(F32)).astype(BF16)
    return hi, lo


def _split3(x):
    hi = x.astype(BF16)
    r = x - hi.astype(F32)
    mid = r.astype(BF16)
    lo = (r - mid.astype(F32)).astype(BF16)
    return hi, mid, lo


def _dot_exact_rhs(x, m):
    hi, mid, lo = _split3(x)
    return _dot(hi, m) + _dot(mid, m) + _dot(lo, m)


def _lhs_exact_dot(m, x):
    hi, mid, lo = _split3(x)
    return _dot(m, hi) + _dot(m, mid) + _dot(m, lo)


def _sigmoid(x):
    return 1.0 / (1.0 + jnp.exp(-x))


def _silu(x):
    return x * _sigmoid(x)


def _softplus(x):
    return jnp.maximum(x, 0.0) + jnp.log1p(jnp.exp(-jnp.abs(x)))


def _gelu_tanh(x):
    return 0.5 * x * (1.0 + jnp.tanh(math.sqrt(2.0 / math.pi) * (x + 0.044715 * (x * x * x))))


def _rms_rows(x, g):
    return x * lax.rsqrt(jnp.mean(x * x, axis=-1, keepdims=True) + RMS_EPS) * g


def _norm_matmul_kernel(x_ref, g_ref, w_ref, o_ref, *, chunk):
    h = _rms_rows(x_ref[...], g_ref[...]).astype(BF16)
    n = o_ref.shape[1]
    for c in range(0, n, chunk):
        w = min(chunk, n - c)
        o_ref[:, c:c + w] = _dot(h, w_ref[:, c:c + w])


def norm_matmul(x, g, w, tm):
    m, k = x.shape
    n = w.shape[1]
    return pl.pallas_call(
        functools.partial(_norm_matmul_kernel, chunk=512),
        grid=(m // tm,),
        in_specs=[pl.BlockSpec((tm, k), lambda i: (i, 0)),
                  pl.BlockSpec((1, k), lambda i: (0, 0)),
                  pl.BlockSpec((k, n), lambda i: (0, 0))],
        out_specs=pl.BlockSpec((tm, n), lambda i: (i, 0)),
        out_shape=jax.ShapeDtypeStruct((m, n), F32),
        compiler_params=_params("parallel"),
        name="norm_matmul",
    )(x, g.reshape(1, k), w)


def _matmul_res_kernel(a_ref, w_ref, r_ref, o_ref):
    o_ref[...] = r_ref[...] + _dot(a_ref[...].astype(BF16), w_ref[...])


def matmul_res(a, w, res, tm):
    m, k = a.shape
    n = w.shape[1]
    return pl.pallas_call(
        _matmul_res_kernel,
        grid=(m // tm,),
        in_specs=[pl.BlockSpec((tm, k), lambda i: (i, 0)),
                  pl.BlockSpec((k, n), lambda i: (0, 0)),
                  pl.BlockSpec((tm, n), lambda i: (i, 0))],
        out_specs=pl.BlockSpec((tm, n), lambda i: (i, 0)),
        out_shape=jax.ShapeDtypeStruct((m, n), F32),
        compiler_params=_params("parallel"),
        name="matmul_res",
    )(a, w, res)


def _head_norm_rope(x, gain, cos, sin_signed, bd):
    hi, lo = _split2(x * x)
    ss = _dot(hi, bd) + _dot(lo, bd)
    y = x * lax.rsqrt(ss * (1.0 / HEAD_DIM) + RMS_EPS) * gain
    lane = lax.broadcasted_iota(jnp.int32, y.shape, 1)
    half = HEAD_DIM // 2
    width = y.shape[1]
    partner = jnp.where((lane & half) != 0, pltpu.roll(y, half, 1), pltpu.roll(y, width - half, 1))
    return y * cos + partner * sin_signed


def _post_kernel(p_ref, gain_ref, cos_ref, sin_ref, bd_ref, *out_refs, plan):
    cos = cos_ref[...]
    sin = sin_ref[...]
    bd = bd_ref[...]
    for mode, src, width, dests in plan:
        x = p_ref[:, src:src + width]
        if mode == "rope":
            y = _head_norm_rope(x, gain_ref[:, src:src + width], cos, sin, bd)
        elif mode == "sigmoid":
            y = _sigmoid(x)
        else:
            y = x
        for oi, oc, transposed in dests:
            if transposed:
                out_refs[oi][0, oc:oc + width, :] = y.T
            else:
                out_refs[oi][:, oc:oc + width] = y


def post_proj(proj, gain_row, cos, sin, bd, plan, outs, tm, b):
    m, n = proj.shape
    n_tab = cos.shape[0] // tm
    nt = m // b // tm
    out_specs, out_shape = [], []
    for w, transposed in outs:
        if transposed:
            out_specs.append(pl.BlockSpec((1, w, tm), lambda i: (i // nt, 0, i % nt)))
            out_shape.append(jax.ShapeDtypeStruct((b, w, m // b), F32))
        else:
            out_specs.append(pl.BlockSpec((tm, w), lambda i: (i, 0)))
            out_shape.append(jax.ShapeDtypeStruct((m, w), F32))
    return pl.pallas_call(
        functools.partial(_post_kernel, plan=plan),
        grid=(m // tm,),
        in_specs=[pl.BlockSpec((tm, n), lambda i: (i, 0)),
                  pl.BlockSpec(gain_row.shape, lambda i: (0, 0)),
                  pl.BlockSpec((tm, 256), lambda i: (i % n_tab, 0)),
                  pl.BlockSpec((tm, 256), lambda i: (i % n_tab, 0)),
                  pl.BlockSpec((256, 256), lambda i: (0, 0))],
        out_specs=out_specs,
        out_shape=out_shape,
        compiler_params=_params("parallel"),
        name="post_proj",
    )(proj, gain_row, cos, sin, bd)


NSA_PLAN_P = tuple(
    [("rope", c * 256, 256, ((0, c * 256, False),)) for c in range(4)]
    + [("rope", 1024, 256, ((1, 0, False), (6, 0, True))),
       ("copy", 1280, 256, ((2, 0, False), (7, 0, True))),
       ("rope", 1536, 256, ((3, 0, False), (8, 0, True))),
       ("copy", 1792, 256, ((9, 0, True),)),
       ("rope", 2048, 256, ((4, 0, False), (10, 0, True))),
       ("copy", 2304, 256, ((11, 0, True),)),
       ("sigmoid", 2560, 128, ((5, 0, False),))])
NSA_OUTS_P = ((1024, False), (256, False), (256, False), (256, False), (256, False), (128, False)) + ((256, True),) * 6
NSA_PLAN_S = tuple(
    [("rope", c * 256, 256, ((0, c * 256, False),)) for c in range(4)]
    + [("rope", 1024, 256, ((1, 0, False),)), ("copy", 1280, 256, ((2, 0, False),)),
       ("rope", 1536, 256, ((3, 0, False),)), ("copy", 1792, 256, ((4, 0, False),)),
       ("rope", 2048, 256, ((5, 0, False), (8, 0, True))),
       ("copy", 2304, 256, ((6, 0, False), (9, 0, True))),
       ("sigmoid", 2560, 128, ((7, 0, False),))])
NSA_OUTS_S = ((1024, False),) + ((256, False),) * 6 + ((128, False), (256, True), (256, True))

DIFF_PLAN_P = tuple(
    [("rope", c * 256, 256, ((0, c * 256, False),)) for c in range(4)]
    + [("rope", 1024 + c * 256, 256, ((1, c * 256, False), (3, c * 256, True))) for c in range(4)]
    + [("copy", 2048 + c * 256, 256, ((2, c * 256, False), (4, c * 256, True))) for c in range(4)])
DIFF_OUTS_P = ((1024, False),) * 3 + ((1024, True),) * 2
DIFF_PLAN_S = tuple(
    [("rope", c * 256, 256, ((0, c * 256, False),)) for c in range(4)]
    + [("rope", 1024 + c * 256, 256, ((1, c * 256, False),)) for c in range(4)]
    + [("copy", 2048 + c * 256, 256, ((2, c * 256, False),)) for c in range(4)])
DIFF_OUTS_S = ((1024, False),) * 3


def _block_rows(piece):
    mats = [jnp.concatenate([piece(t, g) for t in range(CMP_BLOCK)], axis=1) for g in range(NSA_KV_HEADS)]
    return jnp.concatenate(mats, axis=0)


def _compress_rows_kernel(x_ref, pos_ref, w1_ref, w2_ref, w1t_ref, w2t_ref, o_ref, ot_ref):
    xr = x_ref[0] + pos_ref[...]
    xg = _block_rows(lambda t, g: xr[:, t * NSA_KV + g * HEAD_DIM:t * NSA_KV + (g + 1) * HEAD_DIM]).astype(BF16)
    h = _gelu_tanh(_dot(xg, w1_ref[...]))
    o_ref[0] = _dot(h.astype(BF16), w2_ref[...])
    ht = _gelu_tanh(_dot_nt(w1t_ref[...], xg))
    ot_ref[0] = _dot(w2t_ref[...], ht.astype(BF16))


def compress_rows(rows, pos_row, w1, w2, n_blocks):
    b = rows.shape[0]
    wide = CMP_BLOCK * NSA_KV
    nr = NSA_KV_HEADS * n_blocks
    w1t, w2t = w1.T, w2.T
    const = lambda a: pl.BlockSpec(a.shape, lambda i: (0, 0))
    return pl.pallas_call(
        _compress_rows_kernel,
        grid=(b,),
        in_specs=[pl.BlockSpec((1, n_blocks, wide), lambda i: (i, 0, 0)),
                  const(pos_row), const(w1), const(w2), const(w1t), const(w2t)],
        out_specs=[pl.BlockSpec((1, nr, HEAD_DIM), lambda i: (i, 0, 0)),
                   pl.BlockSpec((1, HEAD_DIM, nr), lambda i: (i, 0, 0))],
        out_shape=[jax.ShapeDtypeStruct((b, nr, HEAD_DIM), F32),
                   jax.ShapeDtypeStruct((b, HEAD_DIM, nr), F32)],
        compiler_params=_params("parallel"),
        name="compress_rows",
    )(rows, pos_row, w1, w2, w1t, w2t)


def _page_map(bi, pt, *, j, base):
    return (base + pt[bi, j], 0, 0)


def _page_map4(bi, pt, *, j, base):
    return (base + pt[bi, j], 0, 0, 0)


def _compress_pages_kernel(pt_ref, *refs, n_pages):
    del pt_ref
    x_refs = refs[:n_pages]
    pos_ref, w1_ref, w2_ref, o_ref, tok_ref = refs[n_pages:]
    halves = NSA_KV // LANE
    per_half = LANE // HEAD_DIM
    for j in range(n_pages):
        xt = x_refs[j][0].T
        for c in range(halves):
            tok_ref[c, j * PAGE_SIZE:(j + 1) * PAGE_SIZE, :] = xt[:, c * LANE:(c + 1) * LANE]
    n_blocks = n_pages * (PAGE_SIZE // CMP_BLOCK)
    ys = [[tok_ref[c, pl.ds(t, n_blocks, stride=CMP_BLOCK), :] for c in range(halves)] for t in range(CMP_BLOCK)]
    xg = _block_rows(lambda t, g: ys[t][g // per_half][:, (g % per_half) * HEAD_DIM:(g % per_half + 1) * HEAD_DIM])
    xg = xg + pos_ref[...]
    h = _gelu_tanh(_dot(xg.astype(BF16), w1_ref[...]))
    o_ref[0] = _dot(h.astype(BF16), w2_ref[...])


def compress_pages(pool, page_table, base, pos_row, w1, w2):
    b, n_pages = page_table.shape
    n_blocks = n_pages * (PAGE_SIZE // CMP_BLOCK)
    page_specs = [pl.BlockSpec((1, NSA_KV, PAGE_SIZE), functools.partial(_page_map, j=j, base=base))
                  for j in range(n_pages)]
    const = lambda a: pl.BlockSpec(a.shape, lambda i, pt: (0, 0))
    return pl.pallas_call(
        functools.partial(_compress_pages_kernel, n_pages=n_pages),
        grid_spec=pltpu.PrefetchScalarGridSpec(
            num_scalar_prefetch=1,
            grid=(b,),
            in_specs=page_specs + [const(pos_row), const(w1), const(w2)],
            out_specs=pl.BlockSpec((1, NSA_KV_HEADS * n_blocks, HEAD_DIM), lambda i, pt: (i, 0, 0)),
            scratch_shapes=[pltpu.VMEM((NSA_KV // LANE, n_pages * PAGE_SIZE, LANE), F32)]),
        out_shape=jax.ShapeDtypeStruct((b, NSA_KV_HEADS * n_blocks, HEAD_DIM), F32),
        compiler_params=_params("parallel"),
        name="compress_pages",
    )(page_table, *([pool] * n_pages), pos_row, w1, w2)


def _select_blocks(imp_b, cur, n_sb, k_sel):
    lane = lax.broadcasted_iota(jnp.int32, imp_b.shape, 1)
    forced = (lane == cur) | (lane == 0)
    score = jnp.where(forced, FORCED_SCORE, imp_b)
    score = jnp.where(lane > cur, NEG_INF, score)
    score = jnp.where(lane >= n_sb, INVALID, score)
    sel = jnp.zeros(imp_b.shape, F32)
    for _ in range(k_sel):
        mx = jnp.max(score, axis=-1, keepdims=True)
        idx = jnp.min(jnp.where(score == mx, lane, 1 << 20), axis=-1, keepdims=True)
        hit = lane == idx
        sel = jnp.where(hit, 1.0, sel)
        score = jnp.where(hit, REMOVED, score)
    return sel


def _select_blocks_t(imp_b, cur, n_sb, k_sel):
    blk = lax.broadcasted_iota(jnp.int32, imp_b.shape, 0)
    forced = (blk == cur) | (blk == 0)
    score = jnp.where(forced, FORCED_SCORE, imp_b)
    score = jnp.where(blk > cur, NEG_INF, score)
    score = jnp.where(blk >= n_sb, INVALID, score)
    sel = jnp.zeros(imp_b.shape, F32)
    for _ in range(k_sel):
        mx = jnp.max(score, axis=0, keepdims=True)
        idx = jnp.min(jnp.where(score == mx, blk, 1 << 20), axis=0, keepdims=True)
        hit = blk == idx
        sel = jnp.where(hit, 1.0, sel)
        score = jnp.where(hit, REMOVED, score)
    return sel


def _masked_softmax(s, vis, axis):
    sm = jnp.where(vis, s, NEG_INF)
    m = jnp.max(sm, axis=axis, keepdims=True)
    e = jnp.where(vis, jnp.exp(sm - m), 0.0)
    den = jnp.sum(e, axis=axis, keepdims=True)
    return e / jnp.where(den > 0.0, den, 1.0)


def _masked_softmax_rows(s, vis):
    return _masked_softmax(s, vis, -1)


def _flash_t(k_ref, vt_ref, krows, vrows, qt, lo, hi, valid_fn, kt_size):
    nq = qt.shape[1]
    dv = vrows.stop - vrows.start

    def body(kt, carry):
        m, l, acc = carry
        off = pl.multiple_of(kt * kt_size, kt_size)
        k = k_ref[pl.ds(off, kt_size), krows].astype(BF16)
        vt = vt_ref[0, vrows, pl.ds(off, kt_size)].astype(BF16)
        valid = valid_fn(off)
        s = jnp.where(valid, _dot(k, qt), NEG_INF)
        m_new = jnp.maximum(m, jnp.max(s, axis=0, keepdims=True))
        alpha = jnp.exp(m - m_new)
        p = jnp.where(valid, jnp.exp(s - m_new), 0.0)
        l = alpha * l + jnp.sum(p, axis=0, keepdims=True)
        acc = alpha * acc + _dot(vt, p.astype(BF16))
        return m_new, l, acc

    init = (jnp.full((1, nq), NEG_INF, F32), jnp.zeros((1, nq), F32), jnp.zeros((dv, nq), F32))
    _, l, acc = lax.fori_loop(lo, hi, body, init)
    return acc / l


def _nsa_prompt_kernel(q_ref, gt_ref, kc_ref, vct_ref, ks_ref, vst_ref, kw_ref, vwt_ref,
                       poolt_ref, expandt_ref, o_ref, mask_ref, *, n_cmp, n_sb, k_sel):
    qi = pl.program_id(1)
    qb = Q_BLOCK
    rep = NSA_GROUP
    qt = (q_ref[...] * SCALE).T.astype(BF16)
    gt = gt_ref[...].T
    pq = qi * qb + lax.broadcasted_iota(jnp.int32, (1, qb), 1)
    pq_rep = jnp.concatenate([pq] * rep, axis=1)
    kts = min(NSA_KEY_TILE, mask_ref.shape[0])
    krow = lax.broadcasted_iota(jnp.int32, (kts, rep * qb), 0)
    hi_kt = (qi * qb + qb + kts - 1) // kts
    lo_win = jnp.maximum(qi * qb - WINDOW, 0) // kts
    poolt = poolt_ref[...]
    pieces = []
    for g in range(NSA_KV_HEADS):
        qg = jnp.concatenate(
            [qt[(rep * g + r) * HEAD_DIM:(rep * g + r + 1) * HEAD_DIM, :] for r in range(rep)], axis=1)

        kc = kc_ref[0, g * n_cmp:(g + 1) * n_cmp, :].astype(BF16)
        vct = vct_ref[0, :, g * n_cmp:(g + 1) * n_cmp].astype(BF16)
        s_c = _dot(kc, qg)
        nrow = lax.broadcasted_iota(jnp.int32, s_c.shape, 0)
        vis = ((nrow + 1) * CMP_BLOCK - 1) <= pq_rep
        p_c = _masked_softmax(s_c, vis, 0)
        o_c = _dot(vct, p_c.astype(BF16))
        imp = p_c[:, 0:qb]
        for r in range(1, rep):
            imp = imp + p_c[:, r * qb:(r + 1) * qb]
        hi_, lo_ = _split2(imp)
        imp_b = _dot(poolt, hi_) + _dot(poolt, lo_)
        sel = _select_blocks_t(imp_b, pq // SEL_BLOCK, n_sb, k_sel)
        mask_ref[...] = _dot(expandt_ref[...], sel.astype(BF16))

        def sel_valid(off):
            msel = mask_ref[pl.ds(off, kts), :]
            msel = jnp.concatenate([msel] * rep, axis=1)
            return (msel > 0.5) & ((off + krow) <= pq_rep)

        def win_valid(off):
            dist = pq_rep - (off + krow)
            return (dist >= 0) & (dist < WINDOW)

        cols = slice(g * HEAD_DIM, (g + 1) * HEAD_DIM)
        o_s = _flash_t(ks_ref, vst_ref, cols, cols, qg, 0, hi_kt, sel_valid, kts)
        o_w = _flash_t(kw_ref, vwt_ref, cols, cols, qg, lo_win, hi_kt, win_valid, kts)

        for r in range(rep):
            h = rep * g + r
            lanes = slice(r * qb, (r + 1) * qb)
            pieces.append(gt[3 * h:3 * h + 1, :] * o_c[:, lanes]
                          + gt[3 * h + 1:3 * h + 2, :] * o_s[:, lanes]
                          + gt[3 * h + 2:3 * h + 3, :] * o_w[:, lanes])
    o_ref[...] = jnp.concatenate(pieces, axis=0).T


def nsa_prompt_attn(q, gates, kc, vct, ks, vst, kw, vwt, b, t):
    nq = t // Q_BLOCK
    n_cmp = t // CMP_BLOCK
    n_sb = -(-t // SEL_BLOCK)
    nsb_pad = -(-n_sb // SUBLANE) * SUBLANE
    k_sel = min(N_SEL, n_sb)
    ratio = SEL_BLOCK // CMP_BLOCK
    poolt = (jnp.arange(nsb_pad)[:, None] == jnp.arange(n_cmp)[None, :] // ratio).astype(BF16)
    expandt = (jnp.arange(t)[:, None] // SEL_BLOCK == jnp.arange(nsb_pad)[None, :]).astype(BF16)
    tok = lambda bi, qi: (bi * nq + qi, 0)
    seq = lambda bi, qi: (bi, 0)
    seq3 = lambda bi, qi: (bi, 0, 0)
    return pl.pallas_call(
        functools.partial(_nsa_prompt_kernel, n_cmp=n_cmp, n_sb=n_sb, k_sel=k_sel),
        grid=(b, nq),
        in_specs=[pl.BlockSpec((Q_BLOCK, NSA_HEADS * HEAD_DIM), tok),
                  pl.BlockSpec((Q_BLOCK, LANE), tok),
                  pl.BlockSpec((1, NSA_KV_HEADS * n_cmp, HEAD_DIM), seq3),
                  pl.BlockSpec((1, HEAD_DIM, NSA_KV_HEADS * n_cmp), seq3),
                  pl.BlockSpec((t, NSA_KV), seq), pl.BlockSpec((1, NSA_KV, t), seq3),
                  pl.BlockSpec((t, NSA_KV), seq), pl.BlockSpec((1, NSA_KV, t), seq3),
                  pl.BlockSpec(poolt.shape, lambda bi, qi: (0, 0)),
                  pl.BlockSpec(expandt.shape, lambda bi, qi: (0, 0))],
        out_specs=pl.BlockSpec((Q_BLOCK, NSA_HEADS * HEAD_DIM), tok),
        out_shape=jax.ShapeDtypeStruct((b * t, NSA_HEADS * HEAD_DIM), F32),
        scratch_shapes=[pltpu.VMEM((t, Q_BLOCK), F32)],
        compiler_params=_params("parallel", "arbitrary"),
        name="nsa_prompt_attn",
    )(q, gates, kc, vct, ks, vst, kw, vwt, poolt, expandt)


def _fold_groups(x, rowg, width):
    out = jnp.where(rowg == 0, x[:, 0:width], 0.0)
    for g in range(1, NSA_KV_HEADS):
        out = out + jnp.where(rowg == g, x[:, g * width:(g + 1) * width], 0.0)
    return out


def _place_groups(x, rowg):
    return jnp.concatenate([jnp.where(rowg == g, x, 0.0) for g in range(NSA_KV_HEADS)], axis=1)


def _nsa_sample_kernel(pt_ref, *refs, n_pages, past, buf_len, n_cmp, n_sb, k_sel):
    del pt_ref
    ksp = refs[:n_pages]
    vsp = refs[n_pages:2 * n_pages]
    (q_ref, gt_ref, kc_ref, vc_ref, ksn_ref, vsn_ref, kwn_ref, vwn_ref, kwnt_ref, vwnt_ref, wk_ref, wv_ref,
     pool_ref, expand_ref, gsel_ref, gselt_ref, o_ref, wko_ref, wvo_ref) = refs[2 * n_pages:]
    nh = NSA_HEADS
    rowg = lax.broadcasted_iota(jnp.int32, (nh, 1), 0) // NSA_GROUP
    q16 = q_ref[0] * SCALE
    qmat_f = _place_groups(q16, rowg)
    qmat = qmat_f.astype(BF16)

    s_all = _dot_nt(q16.astype(BF16), kc_ref[0].astype(BF16))
    s_c = _fold_groups(s_all, rowg, n_cmp)
    ncol = lax.broadcasted_iota(jnp.int32, s_c.shape, 1)
    vis = ((ncol + 1) * CMP_BLOCK - 1) <= past
    p_c = _masked_softmax_rows(s_c, vis)
    o_c = _dot(_place_groups(p_c, rowg).astype(BF16), vc_ref[0].astype(BF16))

    gsel = gsel_ref[...]
    hi_, lo_ = _split2(p_c)
    imp = _dot(gsel, hi_) + _dot(gsel, lo_)
    hi_, lo_ = _split2(imp)
    imp_b = _dot(hi_, pool_ref[...]) + _dot(lo_, pool_ref[...])
    sel = _select_blocks(imp_b, past // SEL_BLOCK, n_sb, k_sel)
    sel16 = _dot(gselt_ref[...], sel.astype(BF16))
    maskfull = _dot(sel16.astype(BF16), expand_ref[...])

    def attend(scores, valids, s_new, valid_new, values, v_new):
        m = s_new if valid_new is None else jnp.where(valid_new > 0.5, s_new, NEG_INF)
        for s, vd in zip(scores, valids):
            m = jnp.maximum(m, jnp.max(jnp.where(vd > 0.5, s, NEG_INF), axis=-1, keepdims=True))
        p_new = jnp.exp(s_new - m)
        if valid_new is not None:
            p_new = p_new * valid_new
        l = p_new
        acc = p_new * v_new
        for s, vd, v in zip(scores, valids, values):
            p = jnp.exp(jnp.where(vd > 0.5, s, NEG_INF) - m) * vd
            l = l + jnp.sum(p, axis=-1, keepdims=True)
            acc = acc + _dot_nt(p.astype(BF16), v)
        return acc / l

    scores, valids, values = [], [], []
    for j in range(n_pages):
        scores.append(_dot(qmat, ksp[j][0].astype(BF16)))
        valids.append(maskfull[:, j * PAGE_SIZE:(j + 1) * PAGE_SIZE])
        values.append(vsp[j][0].astype(BF16))
    s_new = jnp.sum(qmat_f * ksn_ref[0], axis=-1, keepdims=True)
    o_s = attend(scores, valids, s_new, maskfull[:, past:past + 1], values, vsn_ref[0])
    o_s = _fold_groups(o_s, rowg, HEAD_DIM)

    wk = wk_ref[0]
    wv = wv_ref[0]
    s_w = _dot(qmat, wk.astype(BF16))
    wcol = lax.broadcasted_iota(jnp.int32, s_w.shape, 1)
    pos_w = past - buf_len + wcol
    valid_w = jnp.where((past - pos_w < WINDOW) & (pos_w >= 0), 1.0, 0.0)
    s_new = jnp.sum(qmat_f * kwn_ref[0], axis=-1, keepdims=True)
    o_w = attend([s_w], [valid_w], s_new, None, [wv.astype(BF16)], vwn_ref[0])
    o_w = _fold_groups(o_w, rowg, HEAD_DIM)

    gt = gt_ref[0]
    o_ref[0] = gt[:, 0:1] * o_c + gt[:, 1:2] * o_s + gt[:, 2:3] * o_w

    bsel = lax.broadcasted_iota(jnp.int32, kwnt_ref.shape[1:], 1) == pl.program_id(0)
    k_col = jnp.sum(jnp.where(bsel, kwnt_ref[0], 0.0), axis=-1, keepdims=True)
    v_col = jnp.sum(jnp.where(bsel, vwnt_ref[0], 0.0), axis=-1, keepdims=True)
    wlane = lax.broadcasted_iota(jnp.int32, (1, buf_len), 1)
    wko_ref[0] = jnp.where(wlane == buf_len - 1, k_col, pltpu.roll(wk, buf_len - 1, 1))
    wvo_ref[0] = jnp.where(wlane == buf_len - 1, v_col, pltpu.roll(wv, buf_len - 1, 1))


def nsa_sample_attn(page_table, pool_k, pool_v, base, q16, gates, kc, vc, ks_new, vs_new,
                    kw_new, vw_new, kwt_new, vwt_new, win_k, win_v, win_base):
    b, n_pages = page_table.shape
    past = n_pages * PAGE_SIZE
    buf_len = win_k.shape[2]
    tk = past + 1
    n_cmp = tk // CMP_BLOCK
    n_sb = -(-tk // SEL_BLOCK)
    k_sel = min(N_SEL, n_sb)
    ratio = SEL_BLOCK // CMP_BLOCK
    pool = (jnp.arange(n_cmp)[:, None] // ratio == jnp.arange(LANE)[None, :]).astype(BF16)
    expand = (jnp.arange(LANE)[:, None] == jnp.arange(past + LANE)[None, :] // SEL_BLOCK).astype(BF16)
    gsel = (jnp.arange(SUBLANE)[:, None] == jnp.arange(NSA_HEADS)[None, :] // NSA_GROUP).astype(BF16)
    gselt = gsel.T
    page = lambda j: pl.BlockSpec((1, NSA_KV, PAGE_SIZE), functools.partial(_page_map, j=j, base=base))
    per_b = lambda shape: pl.BlockSpec((1,) + shape, lambda i, pt: (i, 0, 0))
    const = lambda a: pl.BlockSpec(a.shape, lambda i, pt: (0,) * a.ndim)
    win = pl.BlockSpec((1, NSA_KV, buf_len), lambda i, pt: (win_base + i, 0, 0))
    return pl.pallas_call(
        functools.partial(_nsa_sample_kernel, n_pages=n_pages, past=past, buf_len=buf_len,
                          n_cmp=n_cmp, n_sb=n_sb, k_sel=k_sel),
        grid_spec=pltpu.PrefetchScalarGridSpec(
            num_scalar_prefetch=1,
            grid=(b,),
            in_specs=([page(j) for j in range(n_pages)] + [page(j) for j in range(n_pages)]
                      + [per_b((NSA_HEADS, HEAD_DIM)), per_b((NSA_HEADS, 3)),
                         per_b((NSA_KV_HEADS * n_cmp, HEAD_DIM)), per_b((NSA_KV_HEADS * n_cmp, HEAD_DIM)),
                         per_b((1, NSA_KV)), per_b((1, NSA_KV)), per_b((1, NSA_KV)), per_b((1, NSA_KV)),
                         const(kwt_new), const(vwt_new),
                         win, win, const(pool), const(expand), const(gsel), const(gselt)]),
            out_specs=[per_b((NSA_HEADS, HEAD_DIM)), per_b((NSA_KV, buf_len)), per_b((NSA_KV, buf_len))]),
        out_shape=[jax.ShapeDtypeStruct((b, NSA_HEADS, HEAD_DIM), F32),
                   jax.ShapeDtypeStruct((b, NSA_KV, buf_len), F32),
                   jax.ShapeDtypeStruct((b, NSA_KV, buf_len), F32)],
        compiler_params=_params("parallel"),
        name="nsa_sample_attn",
    )(page_table, *([pool_k] * n_pages), *([pool_v] * n_pages), q16, gates, kc, vc,
      ks_new, vs_new, kw_new, vw_new, kwt_new, vwt_new, win_k, win_v, pool, expand, gsel, gselt)


def _diff_lambda(lam_ref, lam_init):
    lv = lam_ref[...]
    a = jnp.sum(lv[0:1] * lv[1:2], axis=-1, keepdims=True)
    c = jnp.sum(lv[2:3] * lv[3:4], axis=-1, keepdims=True)
    return jnp.exp(a) - jnp.exp(c) + lam_init


def _diff_prompt_kernel(q_ref, k_ref, vt_ref, lam_ref, sub_ref, o_ref, *, lam_init):
    si = pl.program_id(1)
    qs = q_ref.shape[0]
    qt = (q_ref[...] * SCALE).T.astype(BF16)
    pq = si * qs + lax.broadcasted_iota(jnp.int32, (1, qs), 1)
    kts = qs
    krow = lax.broadcasted_iota(jnp.int32, (kts, qs), 0)
    lam = _diff_lambda(lam_ref, lam_init)
    vw = 2 * HEAD_DIM
    n_kt = si + 1

    def valid_fn(off):
        return (off + krow) <= pq

    pieces = []
    for h in range(DIFF_HEADS):
        vrows = slice(h * vw, (h + 1) * vw)
        outs = []
        for c in range(2):
            m_ = 2 * h + c
            cols = slice(m_ * HEAD_DIM, (m_ + 1) * HEAD_DIM)
            outs.append(_flash_t(k_ref, vt_ref, cols, vrows, qt[cols, :], 0, n_kt, valid_fn, kts))
        o = outs[0] - lam * outs[1]
        o = o * lax.rsqrt(jnp.mean(o * o, axis=0, keepdims=True) + RMS_EPS) * sub_ref[...]
        pieces.append(o * (1.0 - lam_init))
    o_ref[...] = jnp.concatenate(pieces, axis=0).T


def diff_prompt_attn(q, k, vt, lam_vec, sub_norm, lam_init, b, t):
    qs = min(DIFF_Q_BLOCK, t)
    nq = t // qs
    tok = lambda bi, qi: (bi * nq + qi, 0)
    return pl.pallas_call(
        functools.partial(_diff_prompt_kernel, lam_init=lam_init),
        grid=(b, nq),
        in_specs=[pl.BlockSpec((qs, D_MODEL), tok),
                  pl.BlockSpec((t, D_MODEL), lambda bi, qi: (bi, 0)),
                  pl.BlockSpec((1, D_MODEL, t), lambda bi, qi: (bi, 0, 0)),
                  pl.BlockSpec(lam_vec.shape, lambda bi, qi: (0, 0)),
                  pl.BlockSpec((2 * HEAD_DIM, 1), lambda bi, qi: (0, 0))],
        out_specs=pl.BlockSpec((qs, D_MODEL), tok),
        out_shape=jax.ShapeDtypeStruct((b * t, D_MODEL), F32),
        compiler_params=_params("parallel", "arbitrary"),
        name="diff_prompt_attn",
    )(q, k, vt, lam_vec, sub_norm.reshape(2 * HEAD_DIM, 1))


def _diff_sample_kernel(pt_ref, *refs, n_pages, lam_init):
    del pt_ref
    kp = refs[:n_pages]
    vp = refs[n_pages:2 * n_pages]
    q_ref, kn_ref, vn_ref, lam_ref, sub_ref, o_ref = refs[2 * n_pages:]
    nmap = 2 * DIFF_HEADS
    vw = 2 * HEAD_DIM
    lam = _diff_lambda(lam_ref, lam_init)
    r = lax.broadcasted_iota(jnp.int32, (nmap, D_MODEL), 0)
    col = lax.broadcasted_iota(jnp.int32, (nmap, D_MODEL), 1)
    own = jnp.where(r < DIFF_HEADS, 2 * r, 2 * (r - DIFF_HEADS) + 1)
    qmat_f = jnp.where(col // HEAD_DIM == own, q_ref[0] * SCALE, 0.0)
    qmat = qmat_f.astype(BF16)
    scores = [_dot(qmat, kp[j][0].astype(BF16)) for j in range(n_pages)]
    s_new = jnp.sum(qmat_f * kn_ref[0], axis=-1, keepdims=True)
    m = s_new
    for s in scores:
        m = jnp.maximum(m, jnp.max(s, axis=-1, keepdims=True))
    p_new = jnp.exp(s_new - m)
    ps = [jnp.exp(s - m) for s in scores]
    l = p_new
    for p in ps:
        l = l + jnp.sum(p, axis=-1, keepdims=True)
    inv = 1.0 / l
    pd_new = (p_new * inv)[0:DIFF_HEADS] - lam * (p_new * inv)[DIFF_HEADS:nmap]
    pds = []
    for j in range(n_pages):
        pn = ps[j] * inv
        pds.append((pn[0:DIFF_HEADS] - lam * pn[DIFF_HEADS:nmap]).astype(BF16))
    cols = []
    for h in range(DIFF_HEADS):
        a = _dot(pds[0], vp[0][0, :, h, :].astype(BF16))
        for j in range(1, n_pages):
            a = a + _dot(pds[j], vp[j][0, :, h, :].astype(BF16))
        cols.append(a)
    acc = pd_new * vn_ref[0] + jnp.concatenate(cols, axis=1)
    rr = lax.broadcasted_iota(jnp.int32, (DIFF_HEADS, D_MODEL), 0)
    cc = lax.broadcasted_iota(jnp.int32, (DIFF_HEADS, D_MODEL), 1)
    om = jnp.where(cc // vw == rr, acc, 0.0)
    ss = jnp.sum(om * om, axis=-1, keepdims=True) * (1.0 / vw)
    on = om * lax.rsqrt(ss + RMS_EPS)
    o_ref[0] = jnp.sum(on, axis=0, keepdims=True) * sub_ref[...] * (1.0 - lam_init)


def diff_sample_attn(page_table, pool_k, pool_v, base, q, k_new, v_new, lam_vec, sub_row, lam_init):
    b, n_pages = page_table.shape
    kpage = lambda j: pl.BlockSpec((1, D_MODEL, PAGE_SIZE), functools.partial(_page_map, j=j, base=base))
    vpage = lambda j: pl.BlockSpec((1, PAGE_SIZE, DIFF_HEADS, 2 * HEAD_DIM),
                                   functools.partial(_page_map4, j=j, base=base))
    per_b = pl.BlockSpec((1, 1, D_MODEL), lambda i, pt: (i, 0, 0))
    return pl.pallas_call(
        functools.partial(_diff_sample_kernel, n_pages=n_pages, lam_init=lam_init),
        grid_spec=pltpu.PrefetchScalarGridSpec(
            num_scalar_prefetch=1,
            grid=(b,),
            in_specs=([kpage(j) for j in range(n_pages)] + [vpage(j) for j in range(n_pages)]
                      + [per_b, per_b, per_b,
                         pl.BlockSpec(lam_vec.shape, lambda i, pt: (0, 0)),
                         pl.BlockSpec((1, D_MODEL), lambda i, pt: (0, 0))]),
            out_specs=per_b),
        out_shape=jax.ShapeDtypeStruct((b, 1, D_MODEL), F32),
        compiler_params=_params("parallel"),
        name="diff_sample_attn",
    )(page_table, *([pool_k] * n_pages), *([pool_v] * n_pages), q, k_new, v_new, lam_vec, sub_row)


def _shift_rows(x, s, carry, row):
    r = pltpu.roll(x, s, 0)
    for i in range(s):
        r = jnp.where(row == i, carry[SUBLANE - s + i:SUBLANE - s + i + 1, :], r)
    return r


def _conv_silu_seq_kernel(x_ref, w_ref, b_ref, o_ref, carry_ref, *, width):
    @pl.when(pl.program_id(2) == 0)
    def _():
        carry_ref[...] = jnp.zeros(carry_ref.shape, F32)

    x = x_ref[...]
    tm = x.shape[0]
    row = lax.broadcasted_iota(jnp.int32, (tm, 1), 0)
    carry = carry_ref[...]
    acc = x * w_ref[width - 1:width, :]
    for s in range(1, width):
        acc = acc + _shift_rows(x, s, carry, row) * w_ref[width - 1 - s:width - s, :]
    acc = acc + b_ref[...]
    o_ref[...] = _silu(acc)
    carry_ref[...] = x[tm - SUBLANE:tm, :]


def ssd_conv_prompt(proj, conv_w, conv_b, b, t, tm):
    cb = 1024
    nt = t // tm
    c0 = SSD_D_INNER // cb
    return pl.pallas_call(
        functools.partial(_conv_silu_seq_kernel, width=SSD_CONV_W),
        grid=(b, SSD_CONV_DIM // cb, nt),
        in_specs=[pl.BlockSpec((tm, cb), lambda bi, j, ti: (bi * nt + ti, c0 + j)),
                  pl.BlockSpec((SSD_CONV_W, cb), lambda bi, j, ti: (0, j)),
                  pl.BlockSpec((1, cb), lambda bi, j, ti: (0, j))],
        out_specs=pl.BlockSpec((tm, cb), lambda bi, j, ti: (bi * nt + ti, j)),
        out_shape=jax.ShapeDtypeStruct((b * t, SSD_CONV_DIM), F32),
        scratch_shapes=[pltpu.VMEM((SUBLANE, cb), F32)],
        compiler_params=_params("parallel", "parallel", "arbitrary"),
        name="ssd_conv_prompt",
    )(proj, conv_w, conv_b.reshape(1, SSD_CONV_DIM))


def _conv_silu_state_kernel(x_ref, p0_ref, p1_ref, p2_ref, w_ref, b_ref, o_ref):
    acc = (p0_ref[...] * w_ref[0:1, :] + p1_ref[...] * w_ref[1:2, :] + p2_ref[...] * w_ref[2:3, :]
           + x_ref[...] * w_ref[3:4, :] + b_ref[...])
    o_ref[...] = _silu(acc)


def ssd_conv_sample(proj, prev, conv_w, conv_b):
    b = proj.shape[0]
    cb = 1024
    c0 = SSD_D_INNER // cb
    col = lambda j: (0, j)
    return pl.pallas_call(
        _conv_silu_state_kernel,
        grid=(SSD_CONV_DIM // cb,),
        in_specs=[pl.BlockSpec((b, cb), lambda j: (0, c0 + j)),
                  pl.BlockSpec((b, cb), col), pl.BlockSpec((b, cb), col), pl.BlockSpec((b, cb), col),
                  pl.BlockSpec((SSD_CONV_W, cb), col), pl.BlockSpec((1, cb), col)],
        out_specs=pl.BlockSpec((b, cb), col),
        out_shape=jax.ShapeDtypeStruct((b, SSD_CONV_DIM), F32),
        compiler_params=_params("parallel"),
        name="ssd_conv_sample",
    )(proj, prev[0], prev[1], prev[2], conv_w, conv_b.reshape(1, SSD_CONV_DIM))


def _ssd_scan_kernel(xbc_ref, dt_ref, bias_ref, alog_ref, dskip_ref, tril_ref, exp_ref,
                     y_ref, st_ref, state_ref):
    @pl.when(pl.program_id(1) == 0)
    def _():
        state_ref[...] = jnp.zeros(state_ref.shape, F32)

    l = SSD_CHUNK
    di = SSD_D_INNER
    n = SSD_D_STATE
    tril = tril_ref[...]
    expm = exp_ref[...]
    dt = _softplus(dt_ref[...] + bias_ref[...])
    a = dt * (-jnp.exp(alog_ref[...]))
    acs = _lhs_exact_dot(tril, a)
    acs_t = acs.T
    dtx = _dot_exact_rhs(dt, expm)
    eacs = jnp.exp(acs)
    eacsx = _dot_exact_rhs(eacs, expm)
    decx = _dot_exact_rhs(jnp.exp(acs[l - 1:l, :] - acs), expm)
    x = xbc_ref[:, 0:di]
    xdt = x * dtx
    xw = (xdt * decx).astype(BF16)
    xdt_b = xdt.astype(BF16)
    ri = lax.broadcasted_iota(jnp.int32, (l, l), 0)
    ci = lax.broadcasted_iota(jnp.int32, (l, l), 1)
    lower = ri >= ci
    gw = SSD_GROUP_W
    for g in range(SSD_GROUPS):
        bg = xbc_ref[:, di + g * n:di + (g + 1) * n]
        cg = xbc_ref[:, di + SSD_GROUPS * n + g * n:di + SSD_GROUPS * n + (g + 1) * n].astype(BF16)
        cb = _dot_nt(cg, bg.astype(BF16))
        st_g = state_ref[:, g * gw:(g + 1) * gw]
        y_off = _dot(cg, st_g.astype(BF16)) * eacsx[:, g * gw:(g + 1) * gw]
        state_ref[:, g * gw:(g + 1) * gw] = (
            st_g * eacsx[l - 1:l, g * gw:(g + 1) * gw] + _dot(bg.T.astype(BF16), xw[:, g * gw:(g + 1) * gw]))
        for hh in range(SSD_HEADS_PER_GROUP):
            h = g * SSD_HEADS_PER_GROUP + hh
            cols = slice(h * SSD_HEADDIM, (h + 1) * SSD_HEADDIM)
            seg = acs[:, h:h + 1] - acs_t[h:h + 1, :]
            lmat = jnp.exp(jnp.where(lower, seg, NEG_INF))
            yd = _dot((cb * lmat).astype(BF16), xdt_b[:, cols])
            y_ref[:, cols] = (yd + y_off[:, hh * SSD_HEADDIM:(hh + 1) * SSD_HEADDIM]
                              + dskip_ref[:, cols] * x[:, cols])
    st_ref[0] = state_ref[...]


def ssd_scan_prompt(xbc, proj, dt_bias_row, a_log_row, dskip_row, b, t):
    nc = t // SSD_CHUNK
    l = SSD_CHUNK
    tril = (jnp.arange(l)[:, None] >= jnp.arange(l)[None, :]).astype(BF16)
    expm = (jnp.arange(LANE)[:, None] == jnp.arange(SSD_D_INNER)[None, :] // SSD_HEADDIM).astype(BF16)
    dt_blk = (SSD_D_INNER + SSD_CONV_DIM) // LANE
    tok = lambda bi, ci: (bi * nc + ci, 0)
    const = lambda a: pl.BlockSpec(a.shape, lambda bi, ci: (0, 0))
    return pl.pallas_call(
        _ssd_scan_kernel,
        grid=(b, nc),
        in_specs=[pl.BlockSpec((l, SSD_CONV_DIM), tok),
                  pl.BlockSpec((l, LANE), lambda bi, ci: (bi * nc + ci, dt_blk)),
                  const(dt_bias_row), const(a_log_row), const(dskip_row), const(tril), const(expm)],
        out_specs=[pl.BlockSpec((l, SSD_D_INNER), tok),
                   pl.BlockSpec((1, SSD_D_STATE, SSD_D_INNER), lambda bi, ci: (bi, 0, 0))],
        out_shape=[jax.ShapeDtypeStruct((b * t, SSD_D_INNER), F32),
                   jax.ShapeDtypeStruct((b, SSD_D_STATE, SSD_D_INNER), F32)],
        scratch_shapes=[pltpu.VMEM((SSD_D_STATE, SSD_D_INNER), F32)],
        compiler_params=_params("parallel", "arbitrary"),
        name="ssd_scan_prompt",
    )(xbc, proj, dt_bias_row, a_log_row, dskip_row, tril, expm)


def _ssd_step_kernel(xbc_ref, dt_ref, bias_ref, alog_ref, dskip_ref, s_ref, y_ref, so_ref):
    di = SSD_D_INNER
    n = SSD_D_STATE
    p = SSD_HEADDIM
    dt = _softplus(dt_ref[0] + bias_ref[...])
    dec = jnp.exp(dt * (-jnp.exp(alog_ref[...])))
    ri = lax.broadcasted_iota(jnp.int32, (p, p), 0)
    ci = lax.broadcasted_iota(jnp.int32, (p, p), 1)
    eye = ri == ci
    for g in range(SSD_GROUPS):
        bmat = jnp.broadcast_to(xbc_ref[0, :, di + g * n:di + (g + 1) * n], (p, n)).astype(BF16)
        c8 = jnp.broadcast_to(xbc_ref[0, :, di + SSD_GROUPS * n + g * n:di + SSD_GROUPS * n + (g + 1) * n],
                              (SUBLANE, n)).astype(BF16)
        for hh in range(SSD_HEADS_PER_GROUP):
            h = g * SSD_HEADS_PER_GROUP + hh
            cols = slice(h * p, (h + 1) * p)
            xh = xbc_ref[0, :, cols]
            xdt = xh * dt[:, h:h + 1]
            diag = jnp.where(eye, jnp.broadcast_to(xdt, (p, p)), 0.0)
            hi_, lo_ = _split2(diag)
            s_new = s_ref[0, h] * dec[:, h:h + 1] + _dot(hi_, bmat) + _dot(lo_, bmat)
            so_ref[0, h] = s_new
            yh = _dot_nt(c8, s_new.astype(BF16))
            y_ref[0, :, cols] = yh[0:1] + dskip_ref[:, cols] * xh


def ssd_step_sample(xbc, proj, dt_bias_row, a_log_row, dskip_row, state, state_base):
    b = xbc.shape[0]
    dt_blk = (SSD_D_INNER + SSD_CONV_DIM) // LANE
    const = lambda a: pl.BlockSpec(a.shape, lambda i: (0, 0))
    st_shape = (1, SSD_HEADS, SSD_HEADDIM, SSD_D_STATE)
    return pl.pallas_call(
        _ssd_step_kernel,
        grid=(b,),
        in_specs=[pl.BlockSpec((1, 1, SSD_CONV_DIM), lambda i: (i, 0, 0)),
                  pl.BlockSpec((1, 1, LANE), lambda i: (i, 0, dt_blk)),
                  const(dt_bias_row), const(a_log_row), const(dskip_row),
                  pl.BlockSpec(st_shape, lambda i: (state_base + i, 0, 0, 0))],
        out_specs=[pl.BlockSpec((1, 1, SSD_D_INNER), lambda i: (i, 0, 0)),
                   pl.BlockSpec(st_shape, lambda i: (i, 0, 0, 0))],
        out_shape=[jax.ShapeDtypeStruct((b, 1, SSD_D_INNER), F32),
                   jax.ShapeDtypeStruct((b,) + st_shape[1:], F32)],
        compiler_params=_params("parallel"),
        name="ssd_step_sample",
    )(xbc.reshape(b, 1, SSD_CONV_DIM), proj.reshape(b, 1, proj.shape[1]),
      dt_bias_row, a_log_row, dskip_row, state)


def _ssd_out_kernel(y_ref, z_ref, ng_ref, w_ref, r_ref, o_ref):
    gated = y_ref[...] * _silu(z_ref[...])
    parts = []
    for g in range(SSD_GROUPS):
        cols = slice(g * SSD_GROUP_W, (g + 1) * SSD_GROUP_W)
        parts.append(_rms_rows(gated[:, cols], ng_ref[:, cols]).astype(BF16))
    o_ref[...] = r_ref[...] + _dot(jnp.concatenate(parts, axis=1), w_ref[...])


def ssd_out(y, proj, norm_g, w, res, tm):
    m = y.shape[0]
    di = SSD_D_INNER
    return pl.pallas_call(
        _ssd_out_kernel,
        grid=(m // tm,),
        in_specs=[pl.BlockSpec((tm, di), lambda i: (i, 0)),
                  pl.BlockSpec((tm, di), lambda i: (i, 0)),
                  pl.BlockSpec((1, di), lambda i: (0, 0)),
                  pl.BlockSpec((di, D_MODEL), lambda i: (0, 0)),
                  pl.BlockSpec((tm, D_MODEL), lambda i: (i, 0))],
        out_specs=pl.BlockSpec((tm, D_MODEL), lambda i: (i, 0)),
        out_shape=jax.ShapeDtypeStruct((m, D_MODEL), F32),
        compiler_params=_params("parallel"),
        name="ssd_out",
    )(y, proj, norm_g.reshape(1, di), w, res)


def _ffn_seq_kernel(x_ref, g_ref, wup_ref, cw_ref, wdn_ref, o_ref, st_ref, carry_ref):
    @pl.when(pl.program_id(1) == 0)
    def _():
        carry_ref[...] = jnp.zeros(carry_ref.shape, F32)

    x = x_ref[...]
    tm = x.shape[0]
    h = _rms_rows(x, g_ref[...]).astype(BF16)
    row = lax.broadcasted_iota(jnp.int32, (tm, 1), 0)
    fc = FFN_CHUNK

    def conv(u, c0):
        w = cw_ref[:, c0:c0 + fc]
        carry = carry_ref[:, c0:c0 + fc]
        y = u * w[2:3] + _shift_rows(u, 1, carry, row) * w[1:2] + _shift_rows(u, 2, carry, row) * w[0:1]
        carry_ref[:, c0:c0 + fc] = u[tm - SUBLANE:tm, :]
        return y

    acc = jnp.zeros((tm, D_MODEL), F32)
    for c0 in range(0, D_FF, fc):
        u = conv(_dot(h, wup_ref[:, c0:c0 + fc]), c0)
        gate = conv(_dot(h, wup_ref[:, D_FF + c0:D_FF + c0 + fc]), D_FF + c0)
        acc = acc + _dot((_silu(gate) * u).astype(BF16), wdn_ref[c0:c0 + fc, :])
    o_ref[...] = x + acc
    st_ref[0] = carry_ref[...]


def ffn_prompt(x, g, w_up, conv_w, w_down, b, t, tm):
    nt = t // tm
    tok = lambda bi, ti: (bi * nt + ti, 0)
    const = lambda a: pl.BlockSpec(a.shape, lambda bi, ti: (0, 0))
    g = g.reshape(1, D_MODEL)
    return pl.pallas_call(
        _ffn_seq_kernel,
        grid=(b, nt),
        in_specs=[pl.BlockSpec((tm, D_MODEL), tok), const(g), const(w_up), const(conv_w), const(w_down)],
        out_specs=[pl.BlockSpec((tm, D_MODEL), tok),
                   pl.BlockSpec((1, SUBLANE, 2 * D_FF), lambda bi, ti: (bi, 0, 0))],
        out_shape=[jax.ShapeDtypeStruct((b * t, D_MODEL), F32),
                   jax.ShapeDtypeStruct((b, SUBLANE, 2 * D_FF), F32)],
        scratch_shapes=[pltpu.VMEM((SUBLANE, 2 * D_FF), F32)],
        compiler_params=_params("parallel", "arbitrary"),
        name="ffn_prompt",
    )(x, g, w_up, conv_w, w_down)


def _ffn_state_kernel(x_ref, g_ref, wup_ref, cw_ref, wdn_ref, p0_ref, p1_ref, o_ref, up_ref):
    x = x_ref[...]
    h = _rms_rows(x, g_ref[...]).astype(BF16)
    fc = FFN_CHUNK

    def conv(u, c0):
        up_ref[:, c0:c0 + fc] = u
        w = cw_ref[:, c0:c0 + fc]
        return u * w[2:3] + p1_ref[:, c0:c0 + fc] * w[1:2] + p0_ref[:, c0:c0 + fc] * w[0:1]

    acc = jnp.zeros(x.shape, F32)
    for c0 in range(0, D_FF, fc):
        u = conv(_dot(h, wup_ref[:, c0:c0 + fc]), c0)
        gate = conv(_dot(h, wup_ref[:, D_FF + c0:D_FF + c0 + fc]), D_FF + c0)
        acc = acc + _dot((_silu(gate) * u).astype(BF16), wdn_ref[c0:c0 + fc, :])
    o_ref[...] = x + acc


def ffn_sample(x, g, w_up, conv_w, w_down, prev):
    b = x.shape[0]
    full = lambda a: pl.BlockSpec(a.shape, lambda i: (0,) * a.ndim)
    g = g.reshape(1, D_MODEL)
    p0, p1 = prev[:, 0], prev[:, 1]
    return pl.pallas_call(
        _ffn_state_kernel,
        grid=(1,),
        in_specs=[full(x), full(g), full(w_up), full(conv_w), full(w_down), full(p0), full(p1)],
        out_specs=[pl.BlockSpec((b, D_MODEL), lambda i: (0, 0)),
                   pl.BlockSpec((b, 2 * D_FF), lambda i: (0, 0))],
        out_shape=[jax.ShapeDtypeStruct((b, D_MODEL), F32),
                   jax.ShapeDtypeStruct((b, 2 * D_FF), F32)],
        compiler_params=_params("arbitrary"),
        name="ffn_sample",
    )(x, g, w_up, conv_w, w_down, p0, p1)


def _rope_tables(pos, rows):
    half = HEAD_DIM // 2
    inv_freq = ROPE_THETA ** (-jnp.arange(half, dtype=F32) / half)
    ang = pos.astype(F32)[:, None] * inv_freq[None, :]
    cos = jnp.cos(ang)
    sin = jnp.sin(ang)
    cos = jnp.tile(jnp.concatenate([cos, cos], axis=-1), (1, 4))
    sin = jnp.tile(jnp.concatenate([-sin, sin], axis=-1), (1, 4))
    if cos.shape[0] != rows:
        cos = jnp.broadcast_to(cos, (rows, 256))
        sin = jnp.broadcast_to(sin, (rows, 256))
    return cos, sin


def _block_diag_ones():
    i = jnp.arange(256)
    return (i[:, None] // HEAD_DIM == i[None, :] // HEAD_DIM).astype(BF16)


def _pad_cols(w, n):
    return jnp.pad(w, ((0, 0), (0, n - w.shape[1])))


def _nsa_weights(w_in, q_norm, k_norm, cmp_pos, cmp_w1, cmp_w2, w_out):
    gain = jnp.concatenate(
        [jnp.tile(q_norm, NSA_HEADS)]
        + [jnp.tile(k_norm[br], NSA_KV_HEADS) if kv == 0 else jnp.ones((NSA_KV,), F32)
           for br in range(3) for kv in range(2)]).reshape(1, NSA_QKV)
    pos_rows = [jnp.broadcast_to(cmp_pos[i][:, None, :], (CMP_BLOCK, NSA_KV_HEADS, HEAD_DIM))
                .reshape(1, CMP_BLOCK * NSA_KV) for i in range(2)]
    pos_blk = [cmp_pos[i].reshape(1, CMP_BLOCK * HEAD_DIM) for i in range(2)]
    return dict(w_in=_pad_cols(w_in, NSA_IN_PAD).astype(BF16), gain=gain, pos=pos_rows, pos_blk=pos_blk,
                w1=[cmp_w1[i].astype(BF16) for i in range(2)],
                w2=[cmp_w2[i].astype(BF16) for i in range(2)],
                w_out=w_out.astype(BF16))


def _kv_rows_view(a):
    b, _, t = a.shape
    return a.reshape(b, NSA_KV_HEADS, HEAD_DIM, t).transpose(0, 3, 1, 2)


def _nsa_prompt_layer(x, gmix, w, tabs, bd, b, t, tm):
    assert t % CMP_BLOCK == 0
    proj = norm_matmul(x, gmix, w["w_in"], tm)
    q, kc_rows, vc_rows, ks, kw, gates, kct, vct, kst, vst, kwt, vwt = post_proj(
        proj, w["gain"], tabs[0], tabs[1], bd, NSA_PLAN_P, NSA_OUTS_P, tm, b)
    n_cmp = t // CMP_BLOCK
    wide = CMP_BLOCK * NSA_KV
    kc, _ = compress_rows(kc_rows.reshape(b, n_cmp, wide), w["pos"][0], w["w1"][0], w["w2"][0], n_cmp)
    _, vc_t = compress_rows(vc_rows.reshape(b, n_cmp, wide), w["pos"][1], w["w1"][1], w["w2"][1], n_cmp)
    o = nsa_prompt_attn(q, gates, kc, vc_t, ks, vst, kw, vwt, b, t)
    x = matmul_res(o, w["w_out"], x, tm)
    keep = min(WINDOW, t)
    rows = tuple(_kv_rows_view(a) for a in (kct, vct, kst, vst))
    wins = tuple(_kv_rows_view(a[:, :, t - keep:]) for a in (kwt, vwt))
    return x, rows + wins


def _nsa_sample_layer(x, gmix, w, tabs, bd, j, page_table, caches, win_k, win_v):
    b = x.shape[0]
    n_phys = caches[0].shape[1]
    proj = norm_matmul(x, gmix, w["w_in"], b)
    q, kc_new, vc_new, ks_new, vs_new, kw_new, vw_new, gates, kwt_new, vwt_new = post_proj(
        proj, w["gain"], tabs[0], tabs[1], bd, NSA_PLAN_S, NSA_OUTS_S, b, 1)
    fm = lambda c: c.transpose(0, 1, 3, 4, 2).reshape(-1, NSA_KV, c.shape[2])
    pool_ck, pool_cv, pool_sk, pool_sv = (fm(c) for c in caches)
    kc = compress_pages(pool_ck, page_table, j * n_phys, w["pos_blk"][0], w["w1"][0], w["w2"][0])
    vc = compress_pages(pool_cv, page_table, j * n_phys, w["pos_blk"][1], w["w1"][1], w["w2"][1])
    row3 = lambda a: a.reshape(b, 1, NSA_KV)
    o16, wk_out, wv_out = nsa_sample_attn(
        page_table, pool_sk, pool_sv, j * n_phys,
        q.reshape(b, NSA_HEADS, HEAD_DIM), gates[:, :NSA_HEADS * 3].reshape(b, NSA_HEADS, 3), kc, vc,
        row3(ks_new), row3(vs_new), row3(kw_new), row3(vw_new), kwt_new, vwt_new,
        fm(win_k), fm(win_v), j * b)
    x = matmul_res(o16.reshape(b, NSA_HEADS * HEAD_DIM), w["w_out"], x, b)
    kvshape = (b, 1, NSA_KV_HEADS, HEAD_DIM)
    rows = tuple(a.reshape(kvshape) for a in (kc_new, vc_new, ks_new, vs_new))
    wins = tuple(_kv_rows_view(a) for a in (wk_out, wv_out))
    return x, rows + wins


def _diff_weights(w_in, q_norm, k_norm, lam_vec, sub_norm, w_out):
    nmap = 2 * DIFF_HEADS
    gain = jnp.concatenate([jnp.tile(q_norm, nmap), jnp.tile(k_norm, nmap),
                            jnp.ones((D_MODEL,), F32)]).reshape(1, 3 * D_MODEL)
    return dict(w_in=w_in.astype(BF16), gain=gain, lam=lam_vec, sub=sub_norm, w_out=w_out.astype(BF16))


def kernel(x_prompt, x_sample, cache_nsa_cmp_k, cache_nsa_cmp_v, cache_nsa_slc_k, cache_nsa_slc_v, state_nsa_win_k, state_nsa_win_v, cache_diff_k, cache_diff_v, state_ssd_conv, state_ssd_ssm, state_ffn_conv, page_table, norm_mix, norm_ffn, nsa_w_in, nsa_q_norm, nsa_k_norm, nsa_cmp_pos, nsa_cmp_w1, nsa_cmp_w2, nsa_w_out, diff_w_in, diff_q_norm, diff_k_norm, diff_lambda, diff_sub_norm, diff_w_out, ssd_w_in, ssd_conv_w, ssd_conv_b, ssd_dt_bias, ssd_a_log, ssd_d, ssd_norm, ssd_w_out, ffn_w_up, ffn_conv_w, ffn_w_down):
    bp, tp, _ = x_prompt.shape
    bs, ts, _ = x_sample.shape
    assert ts == 1 and tp % Q_BLOCK == 0
    n_pages = page_table.shape[1]
    past = n_pages * PAGE_SIZE
    depth = norm_mix.shape[0]
    tm_p = min(256, tp)
    xp = x_prompt.reshape(bp * tp, D_MODEL)
    xs = x_sample.reshape(bs, D_MODEL)
    bd = _block_diag_ones()
    tabs_p = _rope_tables(jnp.arange(tp, dtype=jnp.int32), tp)
    tabs_s = _rope_tables(jnp.full((1,), past, jnp.int32), bs)
    nsa_p, nsa_s, diff_p, diff_s, ssd_p, ssd_s, ffn_p, ffn_s = [], [], [], [], [], [], [], []
    for i in range(depth):
        kind = i % N_MIXERS
        j = i // N_MIXERS
        if kind == 0:
            w = _nsa_weights(nsa_w_in[j], nsa_q_norm[j], nsa_k_norm[j], nsa_cmp_pos[j], nsa_cmp_w1[j],
                             nsa_cmp_w2[j], nsa_w_out[j])
            xp, outs = _nsa_prompt_layer(xp, norm_mix[i], w, tabs_p, bd, bp, tp, tm_p)
            nsa_p.append(outs)
            xs, outs = _nsa_sample_layer(xs, norm_mix[i], w, tabs_s, bd, j, page_table,
                                         (cache_nsa_cmp_k, cache_nsa_cmp_v, cache_nsa_slc_k, cache_nsa_slc_v),
                                         state_nsa_win_k, state_nsa_win_v)
            nsa_s.append(outs)
        elif kind == 1:
            lam_init = 0.8 - 0.6 * math.exp(-0.3 * i)
            w = _diff_weights(diff_w_in[j], diff_q_norm[j], diff_k_norm[j], diff_lambda[j], diff_sub_norm[j],
                              diff_w_out[j])
            proj = norm_matmul(xp, norm_mix[i], w["w_in"], tm_p)
            q, k, v, kt, vt = post_proj(proj, w["gain"], tabs_p[0], tabs_p[1], bd, DIFF_PLAN_P, DIFF_OUTS_P,
                                        tm_p, bp)
            o = diff_prompt_attn(q, k, vt, w["lam"], w["sub"], lam_init, bp, tp)
            xp = matmul_res(o, w["w_out"], xp, tm_p)
            diff_p.append((kt.reshape(bp, DIFF_HEADS, 2, HEAD_DIM, tp).transpose(0, 4, 1, 2, 3),
                           v.reshape(bp, tp, DIFF_HEADS, 2 * HEAD_DIM)))
            n_phys = cache_diff_k.shape[1]
            proj = norm_matmul(xs, norm_mix[i], w["w_in"], bs)
            q, k, v = post_proj(proj, w["gain"], tabs_s[0], tabs_s[1], bd, DIFF_PLAN_S, DIFF_OUTS_S, bs, 1)
            r3 = lambda a: a.reshape(bs, 1, D_MODEL)
            pool_k = cache_diff_k.transpose(0, 1, 3, 4, 5, 2).reshape(-1, D_MODEL, PAGE_SIZE)
            pool_v = cache_diff_v.reshape((-1,) + cache_diff_v.shape[2:])
            o = diff_sample_attn(page_table, pool_k, pool_v, j * n_phys,
                                 r3(q), r3(k), r3(v), w["lam"],
                                 jnp.tile(w["sub"], DIFF_HEADS).reshape(1, D_MODEL), lam_init)
            xs = matmul_res(o.reshape(bs, D_MODEL), w["w_out"], xs, bs)
            diff_s.append((k.reshape(bs, 1, DIFF_HEADS, 2, HEAD_DIM), v.reshape(bs, 1, DIFF_HEADS, 2 * HEAD_DIM)))
        else:
            w_in = _pad_cols(ssd_w_in[j], SSD_IN_PAD).astype(BF16)
            w_out = ssd_w_out[j].astype(BF16)
            pad_h = lambda a: jnp.pad(a, (0, LANE - SSD_HEADS)).reshape(1, LANE)
            bias_row, alog_row = pad_h(ssd_dt_bias[j]), pad_h(ssd_a_log[j])
            dskip_row = jnp.repeat(ssd_d[j], SSD_HEADDIM).reshape(1, SSD_D_INNER)
            xbc0 = SSD_D_INNER
            proj = norm_matmul(xp, norm_mix[i], w_in, tm_p)
            xbc = ssd_conv_prompt(proj, ssd_conv_w[j], ssd_conv_b[j], bp, tp, tm_p)
            y, st = ssd_scan_prompt(xbc, proj, bias_row, alog_row, dskip_row, bp, tp)
            xp = ssd_out(y, proj, ssd_norm[j], w_out, xp, tm_p)
            conv_new = proj.reshape(bp, tp, SSD_IN_PAD)[:, tp - (SSD_CONV_W - 1):, xbc0:xbc0 + SSD_CONV_DIM]
            ssm_new = st.reshape(bp, SSD_D_STATE, SSD_HEADS, SSD_HEADDIM).transpose(0, 2, 3, 1)
            ssd_p.append((conv_new, ssm_new))
            proj = norm_matmul(xs, norm_mix[i], w_in, bs)
            prev = state_ssd_conv[j]
            xbc = ssd_conv_sample(proj, prev.transpose(1, 0, 2), ssd_conv_w[j], ssd_conv_b[j])
            y, st = ssd_step_sample(xbc, proj, bias_row, alog_row, dskip_row,
                                    state_ssd_ssm.reshape((-1,) + state_ssd_ssm.shape[2:]), j * bs)
            xs = ssd_out(y.reshape(bs, SSD_D_INNER), proj, ssd_norm[j], w_out, xs, bs)
            conv_new = jnp.concatenate([prev[:, 1:], proj[:, None, xbc0:xbc0 + SSD_CONV_DIM]], axis=1)
            ssd_s.append((conv_new, st))
        w_up = ffn_w_up[i].astype(BF16)
        w_dn = ffn_w_down[i].astype(BF16)
        xp, st = ffn_prompt(xp, norm_ffn[i], w_up, ffn_conv_w[i], w_dn, bp, tp, tm_p)
        ffn_p.append(st[:, SUBLANE - (FFN_CONV_W - 1):])
        prev = state_ffn_conv[i]
        xs, up = ffn_sample(xs, norm_ffn[i], w_up, ffn_conv_w[i], w_dn, prev)
        ffn_s.append(jnp.concatenate([prev[:, 1:], up[:, None]], axis=1))
    outs = [xp.reshape(bp, tp, D_MODEL), xs.reshape(bs, ts, D_MODEL)]
    for r in range(6):
        outs.append(jnp.stack([o[r] for o in nsa_p]))
        outs.append(jnp.stack([o[r] for o in nsa_s]))
    for r in range(2):
        outs.append(jnp.stack([o[r] for o in diff_p]))
        outs.append(jnp.stack([o[r] for o in diff_s]))
    for r in range(2):
        outs.append(jnp.stack([o[r] for o in ssd_p]))
        outs.append(jnp.stack([o[r] for o in ssd_s]))
    outs.append(jnp.stack(ffn_p))
    outs.append(jnp.stack(ffn_s))
    return tuple(outs)
```

```python
import functools
import math

import jax
import jax.numpy as jnp
from jax import lax
from jax.experimental import pallas as pl
from jax.experimental.pallas import tpu as pltpu

F32 = jnp.float32
BF16 = jnp.bfloat16

D_MODEL = 1024
DEPTH = 4
PAGE_SIZE = 128
N_MIXERS = 3
RMS_EPS = 1e-6
ROPE_THETA = 10000.0
Q_BLOCK = 128
NEG_INF = -1e30
HEAD_DIM = 64
SCALE = HEAD_DIM ** -0.5
LOG2E = 1.4426950408889634

NSA_HEADS = D_MODEL // HEAD_DIM
NSA_KV_HEADS = 4
NSA_GROUP = NSA_HEADS // NSA_KV_HEADS
CMP_BLOCK = 32
SEL_BLOCK = 64
N_SEL = 8
WINDOW = 512
FORCED_SCORE = 1e4
NSA_KEY_TILE = 256
NSA_KV = NSA_KV_HEADS * HEAD_DIM
NSA_QKV = NSA_HEADS * HEAD_DIM + 6 * NSA_KV
NSA_IN_PAD = NSA_QKV + 128

DIFF_HEADS = D_MODEL // (2 * HEAD_DIM)
DIFF_Q_BLOCK = 512

SSD_D_INNER = 2 * D_MODEL
SSD_HEADDIM = 64
SSD_HEADS = SSD_D_INNER // SSD_HEADDIM
SSD_GROUPS = 4
SSD_D_STATE = 128
SSD_CONV_W = 4
SSD_CHUNK = 128
SSD_CONV_DIM = SSD_D_INNER + 2 * SSD_GROUPS * SSD_D_STATE
SSD_IN_PAD = SSD_D_INNER + SSD_CONV_DIM + 128
SSD_GROUP_W = SSD_D_INNER // SSD_GROUPS
SSD_HEADS_PER_GROUP = SSD_HEADS // SSD_GROUPS

D_FF = 2816
FFN_CONV_W = 3
FFN_CHUNK = 256

PROMPT_ROW_TILE = 512
LANE = 128
SUBLANE = 8
VMEM_LIMIT = 56 * 1024 * 1024
REMOVED = -3.4e38
INVALID = -3.2e38

NT_DIMS = (((1,), (1,)), ((), ()))


def _params(*sem):
    return pltpu.CompilerParams(dimension_semantics=sem, vmem_limit_bytes=VMEM_LIMIT)


def _dot(a, b):
    return jnp.dot(a, b, preferred_element_type=F32)


def _dot_nt(a, b):
    return lax.dot_general(a, b, NT_DIMS, preferred_element_type=F32)


def _split2(x):
    hi = x.astype(BF16)
    lo = (x - hi.astype(F32)).astype(BF16)
    return hi, lo


def _split3(x):
    hi = x.astype(BF16)
    r = x - hi.astype(F32)
    mid = r.astype(BF16)
    lo = (r - mid.astype(F32)).astype(BF16)
    return hi, mid, lo


def _dot_exact_rhs(x, m):
    hi, mid, lo = _split3(x)
    return _dot(hi, m) + _dot(mid, m) + _dot(lo, m)


def _lhs_exact_dot(m, x):
    hi, mid, lo = _split3(x)
    return _dot(m, hi) + _dot(m, mid) + _dot(m, lo)


def _sigmoid(x):
    return 1.0 / (1.0 + jnp.exp(-x))


def _silu(x):
    return x * _sigmoid(x)


def _softplus(x):
    return jnp.maximum(x, 0.0) + jnp.log1p(jnp.exp(-jnp.abs(x)))


def _gelu_tanh(x):
    return 0.5 * x * (1.0 + jnp.tanh(math.sqrt(2.0 / math.pi) * (x + 0.044715 * (x * x * x))))


def _rms_rows(x, g):
    return x * lax.rsqrt(jnp.mean(x * x, axis=-1, keepdims=True) + RMS_EPS) * g


def _norm_matmul_kernel(x_ref, g_ref, w_ref, o_ref, *, chunk):
    h = _rms_rows(x_ref[...], g_ref[...]).astype(BF16)
    n = o_ref.shape[1]
    for c in range(0, n, chunk):
        w = min(chunk, n - c)
        o_ref[:, c:c + w] = _dot(h, w_ref[:, c:c + w])


def norm_matmul(x, g, w, tm):
    m, k = x.shape
    n = w.shape[1]
    return pl.pallas_call(
        functools.partial(_norm_matmul_kernel, chunk=512),
        grid=(m // tm,),
        in_specs=[pl.BlockSpec((tm, k), lambda i: (i, 0)),
                  pl.BlockSpec((1, k), lambda i: (0, 0)),
                  pl.BlockSpec((k, n), lambda i: (0, 0))],
        out_specs=pl.BlockSpec((tm, n), lambda i: (i, 0)),
        out_shape=jax.ShapeDtypeStruct((m, n), F32),
        compiler_params=_params("parallel"),
        name="norm_matmul",
    )(x, g.reshape(1, k), w)


def _matmul_res_kernel(a_ref, w_ref, r_ref, o_ref):
    o_ref[...] = r_ref[...] + _dot(a_ref[...].astype(BF16), w_ref[...])


def matmul_res(a, w, res, tm):
    m, k = a.shape
    n = w.shape[1]
    return pl.pallas_call(
        _matmul_res_kernel,
        grid=(m // tm,),
        in_specs=[pl.BlockSpec((tm, k), lambda i: (i, 0)),
                  pl.BlockSpec((k, n), lambda i: (0, 0)),
                  pl.BlockSpec((tm, n), lambda i: (i, 0))],
        out_specs=pl.BlockSpec((tm, n), lambda i: (i, 0)),
        out_shape=jax.ShapeDtypeStruct((m, n), F32),
        compiler_params=_params("parallel"),
        name="matmul_res",
    )(a, w, res)


def _head_norm_rope(x, gain, cos, sin_signed, bd):
    hi, lo = _split2(x * x)
    ss = _dot(hi, bd) + _dot(lo, bd)
    y = x * lax.rsqrt(ss * (1.0 / HEAD_DIM) + RMS_EPS) * gain
    lane = lax.broadcasted_iota(jnp.int32, y.shape, 1)
    half = HEAD_DIM // 2
    width = y.shape[1]
    partner = jnp.where((lane & half) != 0, pltpu.roll(y, half, 1), pltpu.roll(y, width - half, 1))
    return y * cos + partner * sin_signed


def _post_kernel(p_ref, gain_ref, cos_ref, sin_ref, bd_ref, *out_refs, plan):
    cos = cos_ref[...]
    sin = sin_ref[...]
    bd = bd_ref[...]
    for mode, src, width, dests in plan:
        x = p_ref[:, src:src + width]
        if mode == "rope":
            y = _head_norm_rope(x, gain_ref[:, src:src + width], cos, sin, bd)
        elif mode == "sigmoid":
            y = _sigmoid(x)
        else:
            y = x
        for oi, oc, transposed in dests:
            if transposed:
                out_refs[oi][0, oc:oc + width, :] = y.T
            else:
                out_refs[oi][:, oc:oc + width] = y


def post_proj(proj, gain_row, cos, sin, bd, plan, outs, tm, b):
    m, n = proj.shape
    n_tab = cos.shape[0] // tm
    nt = m // b // tm
    out_specs, out_shape = [], []
    for w, transposed in outs:
        if transposed:
            out_specs.append(pl.BlockSpec((1, w, tm), lambda i: (i // nt, 0, i % nt)))
            out_shape.append(jax.ShapeDtypeStruct((b, w, m // b), F32))
        else:
            out_specs.append(pl.BlockSpec((tm, w), lambda i: (i, 0)))
            out_shape.append(jax.ShapeDtypeStruct((m, w), F32))
    return pl.pallas_call(
        functools.partial(_post_kernel, plan=plan),
        grid=(m // tm,),
        in_specs=[pl.BlockSpec((tm, n), lambda i: (i, 0)),
                  pl.BlockSpec(gain_row.shape, lambda i: (0, 0)),
                  pl.BlockSpec((tm, 256), lambda i: (i % n_tab, 0)),
                  pl.BlockSpec((tm, 256), lambda i: (i % n_tab, 0)),
                  pl.BlockSpec((256, 256), lambda i: (0, 0))],
        out_specs=out_specs,
        out_shape=out_shape,
        compiler_params=_params("parallel"),
        name="post_proj",
    )(proj, gain_row, cos, sin, bd)


NSA_PLAN_P = tuple(
    [("rope", c * 256, 256, ((0, c * 256, False),)) for c in range(4)]
    + [("rope", 1024, 256, ((1, 0, False), (6, 0, True))),
       ("copy", 1280, 256, ((2, 0, False), (7, 0, True))),
       ("rope", 1536, 256, ((3, 0, False), (8, 0, True))),
       ("copy", 1792, 256, ((9, 0, True),)),
       ("rope", 2048, 256, ((4, 0, False), (10, 0, True))),
       ("copy", 2304, 256, ((11, 0, True),)),
       ("sigmoid", 2560, 128, ((5, 0, False),))])
NSA_OUTS_P = ((1024, False), (256, False), (256, False), (256, False), (256, False), (128, False)) + ((256, True),) * 6
NSA_PLAN_S = tuple(
    [("rope", c * 256, 256, ((0, c * 256, False),)) for c in range(4)]
    + [("rope", 1024, 256, ((1, 0, False),)), ("copy", 1280, 256, ((2, 0, False),)),
       ("rope", 1536, 256, ((3, 0, False),)), ("copy", 1792, 256, ((4, 0, False),)),
       ("rope", 2048, 256, ((5, 0, False), (8, 0, True))),
       ("copy", 2304, 256, ((6, 0, False), (9, 0, True))),
       ("sigmoid", 2560, 128, ((7, 0, False),))])
NSA_OUTS_S = ((1024, False),) + ((256, False),) * 6 + ((128, False), (256, True), (256, True))

DIFF_PLAN_P = tuple(
    [("rope", c * 256, 256, ((0, c * 256, False),)) for c in range(4)]
    + [("rope", 1024 + c * 256, 256, ((1, c * 256, False), (3, c * 256, True))) for c in range(4)]
    + [("copy", 2048 + c * 256, 256, ((2, c * 256, False), (4, c * 256, True))) for c in range(4)])
DIFF_OUTS_P = ((1024, False),) * 3 + ((1024, True),) * 2
DIFF_PLAN_S = tuple(
    [("rope", c * 256, 256, ((0, c * 256, False),)) for c in range(4)]
    + [("rope", 1024 + c * 256, 256, ((1, c * 256, False),)) for c in range(4)]
    + [("copy", 2048 + c * 256, 256, ((2, c * 256, False),)) for c in range(4)])
DIFF_OUTS_S = ((1024, False),) * 3


def _block_rows(piece):
    mats = [jnp.concatenate([piece(t, g) for t in range(CMP_BLOCK)], axis=1) for g in range(NSA_KV_HEADS)]
    return jnp.concatenate(mats, axis=0)


def _compress_rows_kernel(x_ref, pos_ref, w1_ref, w2_ref, w1t_ref, w2t_ref, o_ref, ot_ref):
    xr = x_ref[0] + pos_ref[...]
    xg = _block_rows(lambda t, g: xr[:, t * NSA_KV + g * HEAD_DIM:t * NSA_KV + (g + 1) * HEAD_DIM]).astype(BF16)
    h = _gelu_tanh(_dot(xg, w1_ref[...]))
    o_ref[0] = _dot(h.astype(BF16), w2_ref[...])
    ht = _gelu_tanh(_dot_nt(w1t_ref[...], xg))
    ot_ref[0] = _dot(w2t_ref[...], ht.astype(BF16))


def compress_rows(rows, pos_row, w1, w2, n_blocks):
    b = rows.shape[0]
    wide = CMP_BLOCK * NSA_KV
    nr = NSA_KV_HEADS * n_blocks
    w1t, w2t = w1.T, w2.T
    const = lambda a: pl.BlockSpec(a.shape, lambda i: (0, 0))
    return pl.pallas_call(
        _compress_rows_kernel,
        grid=(b,),
        in_specs=[pl.BlockSpec((1, n_blocks, wide), lambda i: (i, 0, 0)),
                  const(pos_row), const(w1), const(w2), const(w1t), const(w2t)],
        out_specs=[pl.BlockSpec((1, nr, HEAD_DIM), lambda i: (i, 0, 0)),
                   pl.BlockSpec((1, HEAD_DIM, nr), lambda i: (i, 0, 0))],
        out_shape=[jax.ShapeDtypeStruct((b, nr, HEAD_DIM), F32),
                   jax.ShapeDtypeStruct((b, HEAD_DIM, nr), F32)],
        compiler_params=_params("parallel"),
        name="compress_rows",
    )(rows, pos_row, w1, w2, w1t, w2t)


def _page_map(bi, pt, *, j, base):
    return (base + pt[bi, j], 0, 0)


def _compress_pages_kernel(pt_ref, *refs, n_pages):
    del pt_ref
    x_refs = refs[:n_pages]
    pos_ref, w1_ref, w2_ref, o_ref, tok_ref = refs[n_pages:]
    halves = NSA_KV // LANE
    per_half = LANE // HEAD_DIM
    for j in range(n_pages):
        xt = x_refs[j][0].T
        for c in range(halves):
            tok_ref[c, j * PAGE_SIZE:(j + 1) * PAGE_SIZE, :] = xt[:, c * LANE:(c + 1) * LANE]
    n_blocks = n_pages * (PAGE_SIZE // CMP_BLOCK)
    ys = [[tok_ref[c, pl.ds(t, n_blocks, stride=CMP_BLOCK), :] for c in range(halves)] for t in range(CMP_BLOCK)]
    xg = _block_rows(lambda t, g: ys[t][g // per_half][:, (g % per_half) * HEAD_DIM:(g % per_half + 1) * HEAD_DIM])
    xg = xg + pos_ref[...]
    h = _gelu_tanh(_dot(xg.astype(BF16), w1_ref[...]))
    o_ref[0] = _dot(h.astype(BF16), w2_ref[...])


def compress_pages(pool, page_table, base, pos_row, w1, w2):
    b, n_pages = page_table.shape
    n_blocks = n_pages * (PAGE_SIZE // CMP_BLOCK)
    page_specs = [pl.BlockSpec((1, NSA_KV, PAGE_SIZE), functools.partial(_page_map, j=j, base=base))
                  for j in range(n_pages)]
    const = lambda a: pl.BlockSpec(a.shape, lambda i, pt: (0, 0))
    return pl.pallas_call(
        functools.partial(_compress_pages_kernel, n_pages=n_pages),
        grid_spec=pltpu.PrefetchScalarGridSpec(
            num_scalar_prefetch=1,
            grid=(b,),
            in_specs=page_specs + [const(pos_row), const(w1), const(w2)],
            out_specs=pl.BlockSpec((1, NSA_KV_HEADS * n_blocks, HEAD_DIM), lambda i, pt: (i, 0, 0)),
            scratch_shapes=[pltpu.VMEM((NSA_KV // LANE, n_pages * PAGE_SIZE, LANE), F32)]),
        out_shape=jax.ShapeDtypeStruct((b, NSA_KV_HEADS * n_blocks, HEAD_DIM), F32),
        compiler_params=_params("parallel"),
        name="compress_pages",
    )(page_table, *([pool] * n_pages), pos_row, w1, w2)


def _select_blocks(imp_b, cur, n_sb, k_sel):
    lane = lax.broadcasted_iota(jnp.int32, imp_b.shape, 1)
    forced = (lane == cur) | (lane == 0)
    score = jnp.where(forced, FORCED_SCORE, imp_b)
    score = jnp.where(lane > cur, NEG_INF, score)
    score = jnp.where(lane >= n_sb, INVALID, score)
    sel = jnp.zeros(imp_b.shape, F32)
    for _ in range(k_sel):
        mx = jnp.max(score, axis=-1, keepdims=True)
        idx = jnp.min(jnp.where(score == mx, lane, 1 << 20), axis=-1, keepdims=True)
        hit = lane == idx
        sel = jnp.where(hit, 1.0, sel)
        score = jnp.where(hit, REMOVED, score)
    return sel


def _select_blocks_t(imp_b, cur, n_sb, k_sel):
    blk = lax.broadcasted_iota(jnp.int32, imp_b.shape, 0)
    forced = (blk == cur) | (blk == 0)
    score = jnp.where(forced, FORCED_SCORE, imp_b)
    score = jnp.where(blk > cur, NEG_INF, score)
    score = jnp.where(blk >= n_sb, INVALID, score)
    sel = jnp.zeros(imp_b.shape, F32)
    for _ in range(k_sel):
        mx = jnp.max(score, axis=0, keepdims=True)
        idx = jnp.min(jnp.where(score == mx, blk, 1 << 20), axis=0, keepdims=True)
        hit = blk == idx
        sel = jnp.where(hit, 1.0, sel)
        score = jnp.where(hit, REMOVED, score)
    return sel


def _masked_softmax(s, vis, axis, exp_fn=jnp.exp):
    sm = jnp.where(vis, s, NEG_INF)
    m = jnp.max(sm, axis=axis, keepdims=True)
    e = jnp.where(vis, exp_fn(sm - m), 0.0)
    den = jnp.sum(e, axis=axis, keepdims=True)
    return e / jnp.where(den > 0.0, den, 1.0)


def _masked_softmax_rows(s, vis):
    return _masked_softmax(s, vis, -1)


def _flash_init(chains):
    out = []
    for _, _, _, vrows, qt in chains:
        nq = qt.shape[1]
        out.append((jnp.full((1, nq), NEG_INF, F32), jnp.zeros((1, nq), F32),
                    jnp.zeros((vrows.stop - vrows.start, nq), F32)))
    return tuple(out)


def _flash_tile(chains, carry, kt, kts, bias_fns):
    off = pl.multiple_of(kt * kts, kts)
    out = []
    for (k_ref, kcols, vt_ref, vrows, qt), (m, l, acc), bias_fn in zip(chains, carry, bias_fns):
        k = k_ref[pl.ds(off, kts), kcols].astype(BF16)
        vt = vt_ref[0, vrows, pl.ds(off, kts)].astype(BF16)
        s = _dot(k, qt)
        if bias_fn is not None:
            s = s + bias_fn(off)
        m_new = jnp.maximum(m, jnp.max(s, axis=0, keepdims=True))
        alpha = jnp.exp2(m - m_new)
        p = jnp.exp2(s - m_new)
        l = alpha * l + jnp.sum(p, axis=0, keepdims=True)
        acc = alpha * acc + _dot(vt, p.astype(BF16))
        out.append((m_new, l, acc))
    return tuple(out)


def _flash_loop(chains, lo, hi, kts, bias_fns):
    carry = lax.fori_loop(lo, hi, lambda kt, c: _flash_tile(chains, c, kt, kts, bias_fns), _flash_init(chains))
    return [acc / l for _, l, acc in carry]


def _nsa_prompt_kernel(q_ref, gt_ref, kc_ref, vct_ref, ks_ref, vst_ref, kw_ref, vwt_ref,
                       poolt_ref, expandt_ref, o_ref, selb_ref, winb_ref, *, n_cmp, n_sb, k_sel):
    qi = pl.program_id(1)
    qb = Q_BLOCK
    rep = NSA_GROUP
    t = winb_ref.shape[0]
    qt = (q_ref[...] * (SCALE * LOG2E)).T.astype(BF16)
    gt = gt_ref[...].T
    pq = qi * qb + lax.broadcasted_iota(jnp.int32, (1, qb), 1)
    pq_rep = jnp.concatenate([pq] * rep, axis=1)
    kts = min(NSA_KEY_TILE, t)
    hi_kt = (qi * qb + qb + kts - 1) // kts
    lo_win = jnp.maximum(qi * qb - WINDOW, 0) // kts
    dist = pq - lax.broadcasted_iota(jnp.int32, (t, qb), 0)
    causal = dist >= 0
    winb_ref[...] = jnp.where(causal & (dist < WINDOW), 0.0, NEG_INF)
    poolt = poolt_ref[...]
    expandt = expandt_ref[...]
    sel_chains, win_chains, o_cs = [], [], []
    for g in range(NSA_KV_HEADS):
        qg = jnp.concatenate(
            [qt[(rep * g + r) * HEAD_DIM:(rep * g + r + 1) * HEAD_DIM, :] for r in range(rep)], axis=1)

        kc = kc_ref[0, g * n_cmp:(g + 1) * n_cmp, :].astype(BF16)
        vct = vct_ref[0, :, g * n_cmp:(g + 1) * n_cmp].astype(BF16)
        s_c = _dot(kc, qg)
        nrow = lax.broadcasted_iota(jnp.int32, s_c.shape, 0)
        vis = ((nrow + 1) * CMP_BLOCK - 1) <= pq_rep
        p_c = _masked_softmax(s_c, vis, 0, jnp.exp2)
        o_cs.append(_dot(vct, p_c.astype(BF16)))
        imp = p_c[:, 0:qb]
        for r in range(1, rep):
            imp = imp + p_c[:, r * qb:(r + 1) * qb]
        hi_, lo_ = _split2(imp)
        imp_b = _dot(poolt, hi_) + _dot(poolt, lo_)
        sel = _select_blocks_t(imp_b, pq // SEL_BLOCK, n_sb, k_sel)
        picked = _dot(expandt, sel.astype(BF16)) > 0.5
        selb_ref[g] = jnp.where(picked & causal, 0.0, NEG_INF)
        cols = slice(g * HEAD_DIM, (g + 1) * HEAD_DIM)
        sel_chains.append((ks_ref, cols, vst_ref, cols, qg))
        win_chains.append((kw_ref, cols, vwt_ref, cols, qg))

    def tiled(ref_tile):
        return jnp.concatenate([ref_tile] * rep, axis=1)

    sel_bias = [lambda off, g=g: tiled(selb_ref[g, pl.ds(off, kts), :]) for g in range(NSA_KV_HEADS)]
    win_bias = [lambda off: tiled(winb_ref[pl.ds(off, kts), :])] * NSA_KV_HEADS
    o_ss = _flash_loop(sel_chains, 0, hi_kt, kts, sel_bias)
    o_ws = _flash_loop(win_chains, lo_win, hi_kt, kts, win_bias)

    pieces = []
    for g in range(NSA_KV_HEADS):
        for r in range(rep):
            h = rep * g + r
            lanes = slice(r * qb, (r + 1) * qb)
            pieces.append(gt[3 * h:3 * h + 1, :] * o_cs[g][:, lanes]
                          + gt[3 * h + 1:3 * h + 2, :] * o_ss[g][:, lanes]
                          + gt[3 * h + 2:3 * h + 3, :] * o_ws[g][:, lanes])
    o_ref[...] = jnp.concatenate(pieces, axis=0).T


def nsa_prompt_attn(q, gates, kc, vct, ks, vst, kw, vwt, b, t):
    nq = t // Q_BLOCK
    n_cmp = t // CMP_BLOCK
    n_sb = -(-t // SEL_BLOCK)
    nsb_pad = -(-n_sb // SUBLANE) * SUBLANE
    k_sel = min(N_SEL, n_sb)
    ratio = SEL_BLOCK // CMP_BLOCK
    poolt = (jnp.arange(nsb_pad)[:, None] == jnp.arange(n_cmp)[None, :] // ratio).astype(BF16)
    expandt = (jnp.arange(t)[:, None] // SEL_BLOCK == jnp.arange(nsb_pad)[None, :]).astype(BF16)
    tok = lambda bi, qi: (bi * nq + qi, 0)
    seq = lambda bi, qi: (bi, 0)
    seq3 = lambda bi, qi: (bi, 0, 0)
    return pl.pallas_call(
        functools.partial(_nsa_prompt_kernel, n_cmp=n_cmp, n_sb=n_sb, k_sel=k_sel),
        grid=(b, nq),
        in_specs=[pl.BlockSpec((Q_BLOCK, NSA_HEADS * HEAD_DIM), tok),
                  pl.BlockSpec((Q_BLOCK, LANE), tok),
                  pl.BlockSpec((1, NSA_KV_HEADS * n_cmp, HEAD_DIM), seq3),
                  pl.BlockSpec((1, HEAD_DIM, NSA_KV_HEADS * n_cmp), seq3),
                  pl.BlockSpec((t, NSA_KV), seq), pl.BlockSpec((1, NSA_KV, t), seq3),
                  pl.BlockSpec((t, NSA_KV), seq), pl.BlockSpec((1, NSA_KV, t), seq3),
                  pl.BlockSpec(poolt.shape, lambda bi, qi: (0, 0)),
                  pl.BlockSpec(expandt.shape, lambda bi, qi: (0, 0))],
        out_specs=pl.BlockSpec((Q_BLOCK, NSA_HEADS * HEAD_DIM), tok),
        out_shape=jax.ShapeDtypeStruct((b * t, NSA_HEADS * HEAD_DIM), F32),
        scratch_shapes=[pltpu.VMEM((NSA_KV_HEADS, t, Q_BLOCK), F32), pltpu.VMEM((t, Q_BLOCK), F32)],
        compiler_params=_params("parallel", "arbitrary"),
        name="nsa_prompt_attn",
    )(q, gates, kc, vct, ks, vst, kw, vwt, poolt, expandt)


def _fold_groups(x, rowg, width):
    out = jnp.where(rowg == 0, x[:, 0:width], 0.0)
    for g in range(1, NSA_KV_HEADS):
        out = out + jnp.where(rowg == g, x[:, g * width:(g + 1) * width], 0.0)
    return out


def _place_groups(x, rowg):
    return jnp.concatenate([jnp.where(rowg == g, x, 0.0) for g in range(NSA_KV_HEADS)], axis=1)


def _nsa_sample_kernel(pt_ref, *refs, n_pages, past, buf_len, n_cmp, n_sb, k_sel):
    del pt_ref
    ksp = refs[:n_pages]
    vsp = refs[n_pages:2 * n_pages]
    (q_ref, gt_ref, kc_ref, vc_ref, ksn_ref, vsn_ref, kwn_ref, vwn_ref, kwnt_ref, vwnt_ref, wk_ref, wv_ref,
     pool_ref, expand_ref, gsel_ref, gselt_ref, o_ref, wko_ref, wvo_ref) = refs[2 * n_pages:]
    nh = NSA_HEADS
    rowg = lax.broadcasted_iota(jnp.int32, (nh, 1), 0) // NSA_GROUP
    q16 = q_ref[0] * SCALE
    qmat_f = _place_groups(q16, rowg)
    qmat = qmat_f.astype(BF16)

    s_all = _dot_nt(q16.astype(BF16), kc_ref[0].astype(BF16))
    s_c = _fold_groups(s_all, rowg, n_cmp)
    ncol = lax.broadcasted_iota(jnp.int32, s_c.shape, 1)
    vis = ((ncol + 1) * CMP_BLOCK - 1) <= past
    p_c = _masked_softmax_rows(s_c, vis)
    o_c = _dot(_place_groups(p_c, rowg).astype(BF16), vc_ref[0].astype(BF16))

    gsel = gsel_ref[...]
    hi_, lo_ = _split2(p_c)
    imp = _dot(gsel, hi_) + _dot(gsel, lo_)
    hi_, lo_ = _split2(imp)
    imp_b = _dot(hi_, pool_ref[...]) + _dot(lo_, pool_ref[...])
    sel = _select_blocks(imp_b, past // SEL_BLOCK, n_sb, k_sel)
    sel16 = _dot(gselt_ref[...], sel.astype(BF16))
    maskfull = _dot(sel16.astype(BF16), expand_ref[...])

    def attend(scores, valids, s_new, valid_new, values, v_new):
        m = s_new if valid_new is None else jnp.where(valid_new > 0.5, s_new, NEG_INF)
        for s, vd in zip(scores, valids):
            m = jnp.maximum(m, jnp.max(jnp.where(vd > 0.5, s, NEG_INF), axis=-1, keepdims=True))
        p_new = jnp.exp(s_new - m)
        if valid_new is not None:
            p_new = p_new * valid_new
        l = p_new
        acc = p_new * v_new
        for s, vd, v in zip(scores, valids, values):
            p = jnp.exp(jnp.where(vd > 0.5, s, NEG_INF) - m) * vd
            l = l + jnp.sum(p, axis=-1, keepdims=True)
            acc = acc + _dot_nt(p.astype(BF16), v)
        return acc / l

    scores, valids, values = [], [], []
    for j in range(n_pages):
        scores.append(_dot(qmat, ksp[j][0].astype(BF16)))
        valids.append(maskfull[:, j * PAGE_SIZE:(j + 1) * PAGE_SIZE])
        values.append(vsp[j][0].astype(BF16))
    s_new = jnp.sum(qmat_f * ksn_ref[0], axis=-1, keepdims=True)
    o_s = attend(scores, valids, s_new, maskfull[:, past:past + 1], values, vsn_ref[0])
    o_s = _fold_groups(o_s, rowg, HEAD_DIM)

    wk = wk_ref[0]
    wv = wv_ref[0]
    s_w = _dot(qmat, wk.astype(BF16))
    wcol = lax.broadcasted_iota(jnp.int32, s_w.shape, 1)
    pos_w = past - buf_len + wcol
    valid_w = jnp.where((past - pos_w < WINDOW) & (pos_w >= 0), 1.0, 0.0)
    s_new = jnp.sum(qmat_f * kwn_ref[0], axis=-1, keepdims=True)
    o_w = attend([s_w], [valid_w], s_new, None, [wv.astype(BF16)], vwn_ref[0])
    o_w = _fold_groups(o_w, rowg, HEAD_DIM)

    gt = gt_ref[0]
    o_ref[0] = gt[:, 0:1] * o_c + gt[:, 1:2] * o_s + gt[:, 2:3] * o_w

    bsel = lax.broadcasted_iota(jnp.int32, kwnt_ref.shape[1:], 1) == pl.program_id(0)
    k_col = jnp.sum(jnp.where(bsel, kwnt_ref[0], 0.0), axis=-1, keepdims=True)
    v_col = jnp.sum(jnp.where(bsel, vwnt_ref[0], 0.0), axis=-1, keepdims=True)
    wlane = lax.broadcasted_iota(jnp.int32, (1, buf_len), 1)
    wko_ref[0] = jnp.where(wlane == buf_len - 1, k_col, pltpu.roll(wk, buf_len - 1, 1))
    wvo_ref[0] = jnp.where(wlane == buf_len - 1, v_col, pltpu.roll(wv, buf_len - 1, 1))


def nsa_sample_attn(page_table, pool_k, pool_v, base, q16, gates, kc, vc, ks_new, vs_new,
                    kw_new, vw_new, kwt_new, vwt_new, win_k, win_v, win_base):
    b, n_pages = page_table.shape
    past = n_pages * PAGE_SIZE
    buf_len = win_k.shape[2]
    tk = past + 1
    n_cmp = tk // CMP_BLOCK
    n_sb = -(-tk // SEL_BLOCK)
    k_sel = min(N_SEL, n_sb)
    ratio = SEL_BLOCK // CMP_BLOCK
    pool = (jnp.arange(n_cmp)[:, None] // ratio == jnp.arange(LANE)[None, :]).astype(BF16)
    expand = (jnp.arange(LANE)[:, None] == jnp.arange(past + LANE)[None, :] // SEL_BLOCK).astype(BF16)
    gsel = (jnp.arange(SUBLANE)[:, None] == jnp.arange(NSA_HEADS)[None, :] // NSA_GROUP).astype(BF16)
    gselt = gsel.T
    page = lambda j: pl.BlockSpec((1, NSA_KV, PAGE_SIZE), functools.partial(_page_map, j=j, base=base))
    per_b = lambda shape: pl.BlockSpec((1,) + shape, lambda i, pt: (i, 0, 0))
    const = lambda a: pl.BlockSpec(a.shape, lambda i, pt: (0,) * a.ndim)
    win = pl.BlockSpec((1, NSA_KV, buf_len), lambda i, pt: (win_base + i, 0, 0))
    return pl.pallas_call(
        functools.partial(_nsa_sample_kernel, n_pages=n_pages, past=past, buf_len=buf_len,
                          n_cmp=n_cmp, n_sb=n_sb, k_sel=k_sel),
        grid_spec=pltpu.PrefetchScalarGridSpec(
            num_scalar_prefetch=1,
            grid=(b,),
            in_specs=([page(j) for j in range(n_pages)] + [page(j) for j in range(n_pages)]
                      + [per_b((NSA_HEADS, HEAD_DIM)), per_b((NSA_HEADS, 3)),
                         per_b((NSA_KV_HEADS * n_cmp, HEAD_DIM)), per_b((NSA_KV_HEADS * n_cmp, HEAD_DIM)),
                         per_b((1, NSA_KV)), per_b((1, NSA_KV)), per_b((1, NSA_KV)), per_b((1, NSA_KV)),
                         const(kwt_new), const(vwt_new),
                         win, win, const(pool), const(expand), const(gsel), const(gselt)]),
            out_specs=[per_b((NSA_HEADS, HEAD_DIM)), per_b((NSA_KV, buf_len)), per_b((NSA_KV, buf_len))]),
        out_shape=[jax.ShapeDtypeStruct((b, NSA_HEADS, HEAD_DIM), F32),
                   jax.ShapeDtypeStruct((b, NSA_KV, buf_len), F32),
                   jax.ShapeDtypeStruct((b, NSA_KV, buf_len), F32)],
        compiler_params=_params("parallel"),
        name="nsa_sample_attn",
    )(page_table, *([pool_k] * n_pages), *([pool_v] * n_pages), q16, gates, kc, vc,
      ks_new, vs_new, kw_new, vw_new, kwt_new, vwt_new, win_k, win_v, pool, expand, gsel, gselt)


def _diff_lambda(lam_ref, lam_init):
    lv = lam_ref[...]
    a = jnp.sum(lv[0:1] * lv[1:2], axis=-1, keepdims=True)
    c = jnp.sum(lv[2:3] * lv[3:4], axis=-1, keepdims=True)
    return jnp.exp(a) - jnp.exp(c) + lam_init


def _diff_prompt_kernel(q_ref, k_ref, vt_ref, lam_ref, sub_ref, o_ref, *, lam_init):
    si = pl.program_id(1)
    qs = q_ref.shape[0]
    qt = (q_ref[...] * (SCALE * LOG2E)).T.astype(BF16)
    kts = qs
    lane = lax.broadcasted_iota(jnp.int32, (kts, qs), 1)
    krow = lax.broadcasted_iota(jnp.int32, (kts, qs), 0)
    diag_bias = jnp.where(krow <= lane, 0.0, NEG_INF)
    lam = _diff_lambda(lam_ref, lam_init)
    vw = 2 * HEAD_DIM

    pieces = []
    for h in range(DIFF_HEADS):
        vrows = slice(h * vw, (h + 1) * vw)
        chains = []
        for c in range(2):
            cols = slice((2 * h + c) * HEAD_DIM, (2 * h + c + 1) * HEAD_DIM)
            chains.append((k_ref, cols, vt_ref, vrows, qt[cols, :]))
        carry = lax.fori_loop(0, si, lambda kt, c: _flash_tile(chains, c, kt, kts, (None, None)),
                              _flash_init(chains))
        carry = _flash_tile(chains, carry, si, kts, (lambda off: diag_bias,) * 2)
        outs = [acc / l for _, l, acc in carry]
        o = outs[0] - lam * outs[1]
        o = o * lax.rsqrt(jnp.mean(o * o, axis=0, keepdims=True) + RMS_EPS) * sub_ref[...]
        pieces.append(o * (1.0 - lam_init))
    o_ref[...] = jnp.concatenate(pieces, axis=0).T


def diff_prompt_attn(q, k, vt, lam_vec, sub_norm, lam_init, b, t):
    qs = min(DIFF_Q_BLOCK, t)
    nq = t // qs
    tok = lambda bi, qi: (bi * nq + qi, 0)
    return pl.pallas_call(
        functools.partial(_diff_prompt_kernel, lam_init=lam_init),
        grid=(b, nq),
        in_specs=[pl.BlockSpec((qs, D_MODEL), tok),
                  pl.BlockSpec((t, D_MODEL), lambda bi, qi: (bi, 0)),
                  pl.BlockSpec((1, D_MODEL, t), lambda bi, qi: (bi, 0, 0)),
                  pl.BlockSpec(lam_vec.shape, lambda bi, qi: (0, 0)),
                  pl.BlockSpec((2 * HEAD_DIM, 1), lambda bi, qi: (0, 0))],
        out_specs=pl.BlockSpec((qs, D_MODEL), tok),
        out_shape=jax.ShapeDtypeStruct((b * t, D_MODEL), F32),
        compiler_params=_params("parallel", "arbitrary"),
        name="diff_prompt_attn",
    )(q, k, vt, lam_vec, sub_norm.reshape(2 * HEAD_DIM, 1))


def _diff_sample_kernel(pt_ref, *refs, n_pages, lam_init):
    del pt_ref
    kp = refs[:n_pages]
    vp = refs[n_pages:2 * n_pages]
    q_ref, kn_ref, vn_ref, lam_ref, sub_ref, expm_ref, hmask_ref, o_ref = refs[2 * n_pages:]
    nmap = 2 * DIFF_HEADS
    lam = _diff_lambda(lam_ref, lam_init)
    r = lax.broadcasted_iota(jnp.int32, (nmap, D_MODEL), 0)
    col = lax.broadcasted_iota(jnp.int32, (nmap, D_MODEL), 1)
    own = jnp.where(r < DIFF_HEADS, 2 * r, 2 * (r - DIFF_HEADS) + 1)
    qmat_f = jnp.where(col // HEAD_DIM == own, q_ref[0] * SCALE, 0.0)
    qmat = qmat_f.astype(BF16)
    scores = [_dot(qmat, kp[j][0].astype(BF16)) for j in range(n_pages)]
    s_new = jnp.sum(qmat_f * kn_ref[0], axis=-1, keepdims=True)
    m = s_new
    for s in scores:
        m = jnp.maximum(m, jnp.max(s, axis=-1, keepdims=True))
    p_new = jnp.exp(s_new - m)
    ps = [jnp.exp(s - m) for s in scores]
    l = p_new
    for p in ps:
        l = l + jnp.sum(p, axis=-1, keepdims=True)
    inv = 1.0 / l
    pd_new = (p_new * inv)[0:DIFF_HEADS] - lam * (p_new * inv)[DIFF_HEADS:nmap]
    pds = []
    for j in range(n_pages):
        pn = ps[j] * inv
        pds.append((pn[0:DIFF_HEADS] - lam * pn[DIFF_HEADS:nmap]).astype(BF16))
    expm = expm_ref[...]
    hmask = hmask_ref[...]
    acc = pd_new * vn_ref[0]
    for j in range(n_pages):
        spread = (_dot(pds[j], expm) * hmask).astype(BF16)
        acc = acc + _dot(spread, vp[j][0].astype(BF16))
    o_ref[0] = _rms_rows(acc, sub_ref[...]) * (1.0 - lam_init)


def diff_sample_attn(page_table, pool_k, pool_v, base, q, k_new, v_new, lam_vec, sub_row, lam_init):
    b, n_pages = page_table.shape
    vw = 2 * HEAD_DIM
    rows = PAGE_SIZE * DIFF_HEADS
    expm = (jnp.arange(PAGE_SIZE)[:, None] == jnp.arange(rows)[None, :] // DIFF_HEADS).astype(BF16)
    hmask = (jnp.arange(DIFF_HEADS)[:, None] == jnp.arange(rows)[None, :] % DIFF_HEADS).astype(F32)
    kpage = lambda j: pl.BlockSpec((1, D_MODEL, PAGE_SIZE), functools.partial(_page_map, j=j, base=base))
    vpage = lambda j: pl.BlockSpec((1, rows, vw), functools.partial(_page_map, j=j, base=base))
    per_b = pl.BlockSpec((1, 1, D_MODEL), lambda i, pt: (i, 0, 0))
    per_bh = pl.BlockSpec((1, DIFF_HEADS, vw), lambda i, pt: (i, 0, 0))
    const = lambda a: pl.BlockSpec(a.shape, lambda i, pt: (0, 0))
    return pl.pallas_call(
        functools.partial(_diff_sample_kernel, n_pages=n_pages, lam_init=lam_init),
        grid_spec=pltpu.PrefetchScalarGridSpec(
            num_scalar_prefetch=1,
            grid=(b,),
            in_specs=([kpage(j) for j in range(n_pages)] + [vpage(j) for j in range(n_pages)]
                      + [per_b, per_b, per_bh, const(lam_vec), const(sub_row), const(expm), const(hmask)]),
            out_specs=per_bh),
        out_shape=jax.ShapeDtypeStruct((b, DIFF_HEADS, vw), F32),
        compiler_params=_params("parallel"),
        name="diff_sample_attn",
    )(page_table, *([pool_k] * n_pages), *([pool_v] * n_pages), q, k_new, v_new, lam_vec, sub_row, expm, hmask)


def _shift_rows(x, s, carry, row):
    r = pltpu.roll(x, s, 0)
    for i in range(s):
        r = jnp.where(row == i, carry[SUBLANE - s + i:SUBLANE - s + i + 1, :], r)
    return r


def _conv_silu_seq_kernel(x_ref, w_ref, b_ref, o_ref, carry_ref, *, width):
    @pl.when(pl.program_id(2) == 0)
    def _():
        carry_ref[...] = jnp.zeros(carry_ref.shape, F32)

    x = x_ref[...]
    tm = x.shape[0]
    row = lax.broadcasted_iota(jnp.int32, (tm, 1), 0)
    carry = carry_ref[...]
    acc = x * w_ref[width - 1:width, :]
    for s in range(1, width):
        acc = acc + _shift_rows(x, s, carry, row) * w_ref[width - 1 - s:width - s, :]
    acc = acc + b_ref[...]
    o_ref[...] = _silu(acc)
    carry_ref[...] = x[tm - SUBLANE:tm, :]


def ssd_conv_prompt(proj, conv_w, conv_b, b, t, tm):
    cb = 1024
    nt = t // tm
    c0 = SSD_D_INNER // cb
    return pl.pallas_call(
        functools.partial(_conv_silu_seq_kernel, width=SSD_CONV_W),
        grid=(b, SSD_CONV_DIM // cb, nt),
        in_specs=[pl.BlockSpec((tm, cb), lambda bi, j, ti: (bi * nt + ti, c0 + j)),
                  pl.BlockSpec((SSD_CONV_W, cb), lambda bi, j, ti: (0, j)),
                  pl.BlockSpec((1, cb), lambda bi, j, ti: (0, j))],
        out_specs=pl.BlockSpec((tm, cb), lambda bi, j, ti: (bi * nt + ti, j)),
        out_shape=jax.ShapeDtypeStruct((b * t, SSD_CONV_DIM), F32),
        scratch_shapes=[pltpu.VMEM((SUBLANE, cb), F32)],
        compiler_params=_params("parallel", "parallel", "arbitrary"),
        name="ssd_conv_prompt",
    )(proj, conv_w, conv_b.reshape(1, SSD_CONV_DIM))


def _conv_silu_state_kernel(x_ref, p0_ref, p1_ref, p2_ref, w_ref, b_ref, o_ref):
    acc = (p0_ref[...] * w_ref[0:1, :] + p1_ref[...] * w_ref[1:2, :] + p2_ref[...] * w_ref[2:3, :]
           + x_ref[...] * w_ref[3:4, :] + b_ref[...])
    o_ref[...] = _silu(acc)


def ssd_conv_sample(proj, prev, conv_w, conv_b):
    b = proj.shape[0]
    cb = 1024
    c0 = SSD_D_INNER // cb
    col = lambda j: (0, j)
    return pl.pallas_call(
        _conv_silu_state_kernel,
        grid=(SSD_CONV_DIM // cb,),
        in_specs=[pl.BlockSpec((b, cb), lambda j: (0, c0 + j)),
                  pl.BlockSpec((b, cb), col), pl.BlockSpec((b, cb), col), pl.BlockSpec((b, cb), col),
                  pl.BlockSpec((SSD_CONV_W, cb), col), pl.BlockSpec((1, cb), col)],
        out_specs=pl.BlockSpec((b, cb), col),
        out_shape=jax.ShapeDtypeStruct((b, SSD_CONV_DIM), F32),
        compiler_params=_params("parallel"),
        name="ssd_conv_sample",
    )(proj, prev[0], prev[1], prev[2], conv_w, conv_b.reshape(1, SSD_CONV_DIM))


def _ssd_scan_kernel(xbc_ref, dt_ref, bias_ref, alog_ref, dskip_ref, tril_ref, exp_ref,
                     y_ref, st_ref, state_ref):
    @pl.when(pl.program_id(1) == 0)
    def _():
        state_ref[...] = jnp.zeros(state_ref.shape, F32)

    l = SSD_CHUNK
    di = SSD_D_INNER
    n = SSD_D_STATE
    tril = tril_ref[...]
    expm = exp_ref[...]
    dt = _softplus(dt_ref[...] + bias_ref[...])
    a = dt * (-jnp.exp(alog_ref[...]))
    acs = _lhs_exact_dot(tril, a)
    acs_t = acs.T
    dtx = _dot_exact_rhs(dt, expm)
    eacs = jnp.exp(acs)
    eacsx = _dot_exact_rhs(eacs, expm)
    decx = _dot_exact_rhs(jnp.exp(acs[l - 1:l, :] - acs), expm)
    x = xbc_ref[:, 0:di]
    xdt = x * dtx
    xw = (xdt * decx).astype(BF16)
    xdt_b = xdt.astype(BF16)
    ri = lax.broadcasted_iota(jnp.int32, (l, l), 0)
    ci = lax.broadcasted_iota(jnp.int32, (l, l), 1)
    lower = ri >= ci
    gw = SSD_GROUP_W
    for g in range(SSD_GROUPS):
        bg = xbc_ref[:, di + g * n:di + (g + 1) * n]
        cg = xbc_ref[:, di + SSD_GROUPS * n + g * n:di + SSD_GROUPS * n + (g + 1) * n].astype(BF16)
        cb = _dot_nt(cg, bg.astype(BF16))
        st_g = state_ref[:, g * gw:(g + 1) * gw]
        y_off = _dot(cg, st_g.astype(BF16)) * eacsx[:, g * gw:(g + 1) * gw]
        state_ref[:, g * gw:(g + 1) * gw] = (
            st_g * eacsx[l - 1:l, g * gw:(g + 1) * gw] + _dot(bg.T.astype(BF16), xw[:, g * gw:(g + 1) * gw]))
        for hh in range(SSD_HEADS_PER_GROUP):
            h = g * SSD_HEADS_PER_GROUP + hh
            cols = slice(h * SSD_HEADDIM, (h + 1) * SSD_HEADDIM)
            seg = acs[:, h:h + 1] - acs_t[h:h + 1, :]
            lmat = jnp.exp(jnp.where(lower, seg, NEG_INF))
            yd = _dot((cb * lmat).astype(BF16), xdt_b[:, cols])
            y_ref[:, cols] = (yd + y_off[:, hh * SSD_HEADDIM:(hh + 1) * SSD_HEADDIM]
                              + dskip_ref[:, cols] * x[:, cols])
    st_ref[0] = state_ref[...]


def ssd_scan_prompt(xbc, proj, dt_bias_row, a_log_row, dskip_row, b, t):
    nc = t // SSD_CHUNK
    l = SSD_CHUNK
    tril = (jnp.arange(l)[:, None] >= jnp.arange(l)[None, :]).astype(BF16)
    expm = (jnp.arange(LANE)[:, None] == jnp.arange(SSD_D_INNER)[None, :] // SSD_HEADDIM).astype(BF16)
    dt_blk = (SSD_D_INNER + SSD_CONV_DIM) // LANE
    tok = lambda bi, ci: (bi * nc + ci, 0)
    const = lambda a: pl.BlockSpec(a.shape, lambda bi, ci: (0, 0))
    return pl.pallas_call(
        _ssd_scan_kernel,
        grid=(b, nc),
        in_specs=[pl.BlockSpec((l, SSD_CONV_DIM), tok),
                  pl.BlockSpec((l, LANE), lambda bi, ci: (bi * nc + ci, dt_blk)),
                  const(dt_bias_row), const(a_log_row), const(dskip_row), const(tril), const(expm)],
        out_specs=[pl.BlockSpec((l, SSD_D_INNER), tok),
                   pl.BlockSpec((1, SSD_D_STATE, SSD_D_INNER), lambda bi, ci: (bi, 0, 0))],
        out_shape=[jax.ShapeDtypeStruct((b * t, SSD_D_INNER), F32),
                   jax.ShapeDtypeStruct((b, SSD_D_STATE, SSD_D_INNER), F32)],
        scratch_shapes=[pltpu.VMEM((SSD_D_STATE, SSD_D_INNER), F32)],
        compiler_params=_params("parallel", "arbitrary"),
        name="ssd_scan_prompt",
    )(xbc, proj, dt_bias_row, a_log_row, dskip_row, tril, expm)


def _ssd_step_kernel(xbc_ref, dt_ref, bias_ref, alog_ref, dskip_ref, s_ref, y_ref, so_ref):
    di = SSD_D_INNER
    n = SSD_D_STATE
    p = SSD_HEADDIM
    dt = _softplus(dt_ref[0] + bias_ref[...])
    dec = jnp.exp(dt * (-jnp.exp(alog_ref[...])))
    ri = lax.broadcasted_iota(jnp.int32, (p, p), 0)
    ci = lax.broadcasted_iota(jnp.int32, (p, p), 1)
    eye = ri == ci
    for g in range(SSD_GROUPS):
        bmat = jnp.broadcast_to(xbc_ref[0, :, di + g * n:di + (g + 1) * n], (p, n)).astype(BF16)
        c8 = jnp.broadcast_to(xbc_ref[0, :, di + SSD_GROUPS * n + g * n:di + SSD_GROUPS * n + (g + 1) * n],
                              (SUBLANE, n)).astype(BF16)
        for hh in range(SSD_HEADS_PER_GROUP):
            h = g * SSD_HEADS_PER_GROUP + hh
            cols = slice(h * p, (h + 1) * p)
            xh = xbc_ref[0, :, cols]
            xdt = xh * dt[:, h:h + 1]
            diag = jnp.where(eye, jnp.broadcast_to(xdt, (p, p)), 0.0)
            hi_, lo_ = _split2(diag)
            s_new = s_ref[0, h] * dec[:, h:h + 1] + _dot(hi_, bmat) + _dot(lo_, bmat)
            so_ref[0, h] = s_new
            yh = _dot_nt(c8, s_new.astype(BF16))
            y_ref[0, :, cols] = yh[0:1] + dskip_ref[:, cols] * xh


def ssd_step_sample(xbc, proj, dt_bias_row, a_log_row, dskip_row, state, state_base):
    b = xbc.shape[0]
    dt_blk = (SSD_D_INNER + SSD_CONV_DIM) // LANE
    const = lambda a: pl.BlockSpec(a.shape, lambda i: (0, 0))
    st_shape = (1, SSD_HEADS, SSD_HEADDIM, SSD_D_STATE)
    return pl.pallas_call(
        _ssd_step_kernel,
        grid=(b,),
        in_specs=[pl.BlockSpec((1, 1, SSD_CONV_DIM), lambda i: (i, 0, 0)),
                  pl.BlockSpec((1, 1, LANE), lambda i: (i, 0, dt_blk)),
                  const(dt_bias_row), const(a_log_row), const(dskip_row),
                  pl.BlockSpec(st_shape, lambda i: (state_base + i, 0, 0, 0))],
        out_specs=[pl.BlockSpec((1, 1, SSD_D_INNER), lambda i: (i, 0, 0)),
                   pl.BlockSpec(st_shape, lambda i: (i, 0, 0, 0))],
        out_shape=[jax.ShapeDtypeStruct((b, 1, SSD_D_INNER), F32),
                   jax.ShapeDtypeStruct((b,) + st_shape[1:], F32)],
        compiler_params=_params("parallel"),
        name="ssd_step_sample",
    )(xbc.reshape(b, 1, SSD_CONV_DIM), proj.reshape(b, 1, proj.shape[1]),
      dt_bias_row, a_log_row, dskip_row, state)


def _ssd_out_kernel(y_ref, z_ref, ng_ref, w_ref, r_ref, o_ref):
    gated = y_ref[...] * _silu(z_ref[...])
    parts = []
    for g in range(SSD_GROUPS):
        cols = slice(g * SSD_GROUP_W, (g + 1) * SSD_GROUP_W)
        parts.append(_rms_rows(gated[:, cols], ng_ref[:, cols]).astype(BF16))
    o_ref[...] = r_ref[...] + _dot(jnp.concatenate(parts, axis=1), w_ref[...])


def ssd_out(y, proj, norm_g, w, res, tm):
    m = y.shape[0]
    di = SSD_D_INNER
    return pl.pallas_call(
        _ssd_out_kernel,
        grid=(m // tm,),
        in_specs=[pl.BlockSpec((tm, di), lambda i: (i, 0)),
                  pl.BlockSpec((tm, di), lambda i: (i, 0)),
                  pl.BlockSpec((1, di), lambda i: (0, 0)),
                  pl.BlockSpec((di, D_MODEL), lambda i: (0, 0)),
                  pl.BlockSpec((tm, D_MODEL), lambda i: (i, 0))],
        out_specs=pl.BlockSpec((tm, D_MODEL), lambda i: (i, 0)),
        out_shape=jax.ShapeDtypeStruct((m, D_MODEL), F32),
        compiler_params=_params("parallel"),
        name="ssd_out",
    )(y, proj, norm_g.reshape(1, di), w, res)


def _ffn_seq_kernel(x_ref, g_ref, wup_ref, cw_ref, wdn_ref, o_ref, st_ref, carry_ref):
    @pl.when(pl.program_id(1) == 0)
    def _():
        carry_ref[...] = jnp.zeros(carry_ref.shape, F32)

    x = x_ref[...]
    tm = x.shape[0]
    h = _rms_rows(x, g_ref[...]).astype(BF16)
    row = lax.broadcasted_iota(jnp.int32, (tm, 1), 0)
    fc = FFN_CHUNK

    def conv(u, c0):
        w = cw_ref[:, c0:c0 + fc]
        carry = carry_ref[:, c0:c0 + fc]
        y = u * w[2:3] + _shift_rows(u, 1, carry, row) * w[1:2] + _shift_rows(u, 2, carry, row) * w[0:1]
        carry_ref[:, c0:c0 + fc] = u[tm - SUBLANE:tm, :]
        return y

    acc = jnp.zeros((tm, D_MODEL), F32)
    for c0 in range(0, D_FF, fc):
        u = conv(_dot(h, wup_ref[:, c0:c0 + fc]), c0)
        gate = conv(_dot(h, wup_ref[:, D_FF + c0:D_FF + c0 + fc]), D_FF + c0)
        acc = acc + _dot((_silu(gate) * u).astype(BF16), wdn_ref[c0:c0 + fc, :])
    o_ref[...] = x + acc
    st_ref[0] = carry_ref[...]


def ffn_prompt(x, g, w_up, conv_w, w_down, b, t, tm):
    nt = t // tm
    tok = lambda bi, ti: (bi * nt + ti, 0)
    const = lambda a: pl.BlockSpec(a.shape, lambda bi, ti: (0, 0))
    g = g.reshape(1, D_MODEL)
    return pl.pallas_call(
        _ffn_seq_kernel,
        grid=(b, nt),
        in_specs=[pl.BlockSpec((tm, D_MODEL), tok), const(g), const(w_up), const(conv_w), const(w_down)],
        out_specs=[pl.BlockSpec((tm, D_MODEL), tok),
                   pl.BlockSpec((1, SUBLANE, 2 * D_FF), lambda bi, ti: (bi, 0, 0))],
        out_shape=[jax.ShapeDtypeStruct((b * t, D_MODEL), F32),
                   jax.ShapeDtypeStruct((b, SUBLANE, 2 * D_FF), F32)],
        scratch_shapes=[pltpu.VMEM((SUBLANE, 2 * D_FF), F32)],
        compiler_params=_params("parallel", "arbitrary"),
        name="ffn_prompt",
    )(x, g, w_up, conv_w, w_down)


def _ffn_state_kernel(x_ref, g_ref, wup_ref, cw_ref, wdn_ref, p0_ref, p1_ref, o_ref, up_ref):
    x = x_ref[...]
    h = _rms_rows(x, g_ref[...]).astype(BF16)
    fc = FFN_CHUNK

    def conv(u, c0):
        up_ref[:, c0:c0 + fc] = u
        w = cw_ref[:, c0:c0 + fc]
        return u * w[2:3] + p1_ref[:, c0:c0 + fc] * w[1:2] + p0_ref[:, c0:c0 + fc] * w[0:1]

    acc = jnp.zeros(x.shape, F32)
    for c0 in range(0, D_FF, fc):
        u = conv(_dot(h, wup_ref[:, c0:c0 + fc]), c0)
        gate = conv(_dot(h, wup_ref[:, D_FF + c0:D_FF + c0 + fc]), D_FF + c0)
        acc = acc + _dot((_silu(gate) * u).astype(BF16), wdn_ref[c0:c0 + fc, :])
    o_ref[...] = x + acc


def ffn_sample(x, g, w_up, conv_w, w_down, prev):
    b = x.shape[0]
    full = lambda a: pl.BlockSpec(a.shape, lambda i: (0,) * a.ndim)
    g = g.reshape(1, D_MODEL)
    p0, p1 = prev[:, 0], prev[:, 1]
    return pl.pallas_call(
        _ffn_state_kernel,
        grid=(1,),
        in_specs=[full(x), full(g), full(w_up), full(conv_w), full(w_down), full(p0), full(p1)],
        out_specs=[pl.BlockSpec((b, D_MODEL), lambda i: (0, 0)),
                   pl.BlockSpec((b, 2 * D_FF), lambda i: (0, 0))],
        out_shape=[jax.ShapeDtypeStruct((b, D_MODEL), F32),
                   jax.ShapeDtypeStruct((b, 2 * D_FF), F32)],
        compiler_params=_params("arbitrary"),
        name="ffn_sample",
    )(x, g, w_up, conv_w, w_down, p0, p1)


def _rope_tables(pos, rows):
    half = HEAD_DIM // 2
    inv_freq = ROPE_THETA ** (-jnp.arange(half, dtype=F32) / half)
    ang = pos.astype(F32)[:, None] * inv_freq[None, :]
    cos = jnp.cos(ang)
    sin = jnp.sin(ang)
    cos = jnp.tile(jnp.concatenate([cos, cos], axis=-1), (1, 4))
    sin = jnp.tile(jnp.concatenate([-sin, sin], axis=-1), (1, 4))
    if cos.shape[0] != rows:
        cos = jnp.broadcast_to(cos, (rows, 256))
        sin = jnp.broadcast_to(sin, (rows, 256))
    return cos, sin


def _block_diag_ones():
    i = jnp.arange(256)
    return (i[:, None] // HEAD_DIM == i[None, :] // HEAD_DIM).astype(BF16)


def _pad_cols(w, n):
    return jnp.pad(w, ((0, 0), (0, n - w.shape[1])))


def _nsa_weights(w_in, q_norm, k_norm, cmp_pos, cmp_w1, cmp_w2, w_out):
    gain = jnp.concatenate(
        [jnp.tile(q_norm, NSA_HEADS)]
        + [jnp.tile(k_norm[br], NSA_KV_HEADS) if kv == 0 else jnp.ones((NSA_KV,), F32)
           for br in range(3) for kv in range(2)]).reshape(1, NSA_QKV)
    pos_rows = [jnp.broadcast_to(cmp_pos[i][:, None, :], (CMP_BLOCK, NSA_KV_HEADS, HEAD_DIM))
                .reshape(1, CMP_BLOCK * NSA_KV) for i in range(2)]
    pos_blk = [cmp_pos[i].reshape(1, CMP_BLOCK * HEAD_DIM) for i in range(2)]
    return dict(w_in=_pad_cols(w_in, NSA_IN_PAD).astype(BF16), gain=gain, pos=pos_rows, pos_blk=pos_blk,
                w1=[cmp_w1[i].astype(BF16) for i in range(2)],
                w2=[cmp_w2[i].astype(BF16) for i in range(2)],
                w_out=w_out.astype(BF16))


def _kv_rows_view(a):
    b, _, t = a.shape
    return a.reshape(b, NSA_KV_HEADS, HEAD_DIM, t).transpose(0, 3, 1, 2)


def _nsa_prompt_layer(x, gmix, w, tabs, bd, b, t, tm):
    assert t % CMP_BLOCK == 0
    proj = norm_matmul(x, gmix, w["w_in"], tm)
    q, kc_rows, vc_rows, ks, kw, gates, kct, vct, kst, vst, kwt, vwt = post_proj(
        proj, w["gain"], tabs[0], tabs[1], bd, NSA_PLAN_P, NSA_OUTS_P, tm, b)
    n_cmp = t // CMP_BLOCK
    wide = CMP_BLOCK * NSA_KV
    kc, _ = compress_rows(kc_rows.reshape(b, n_cmp, wide), w["pos"][0], w["w1"][0], w["w2"][0], n_cmp)
    _, vc_t = compress_rows(vc_rows.reshape(b, n_cmp, wide), w["pos"][1], w["w1"][1], w["w2"][1], n_cmp)
    o = nsa_prompt_attn(q, gates, kc, vc_t, ks, vst, kw, vwt, b, t)
    x = matmul_res(o, w["w_out"], x, tm)
    keep = min(WINDOW, t)
    rows = tuple(_kv_rows_view(a) for a in (kct, vct, kst, vst))
    wins = tuple(_kv_rows_view(a[:, :, t - keep:]) for a in (kwt, vwt))
    return x, rows + wins


def _nsa_sample_layer(x, gmix, w, tabs, bd, j, page_table, caches, win_k, win_v):
    b = x.shape[0]
    n_phys = caches[0].shape[1]
    proj = norm_matmul(x, gmix, w["w_in"], b)
    q, kc_new, vc_new, ks_new, vs_new, kw_new, vw_new, gates, kwt_new, vwt_new = post_proj(
        proj, w["gain"], tabs[0], tabs[1], bd, NSA_PLAN_S, NSA_OUTS_S, b, 1)
    fm = lambda c: c.transpose(0, 1, 3, 4, 2).reshape(-1, NSA_KV, c.shape[2])
    pool_ck, pool_cv, pool_sk, pool_sv = (fm(c) for c in caches)
    kc = compress_pages(pool_ck, page_table, j * n_phys, w["pos_blk"][0], w["w1"][0], w["w2"][0])
    vc = compress_pages(pool_cv, page_table, j * n_phys, w["pos_blk"][1], w["w1"][1], w["w2"][1])
    row3 = lambda a: a.reshape(b, 1, NSA_KV)
    o16, wk_out, wv_out = nsa_sample_attn(
        page_table, pool_sk, pool_sv, j * n_phys,
        q.reshape(b, NSA_HEADS, HEAD_DIM), gates[:, :NSA_HEADS * 3].reshape(b, NSA_HEADS, 3), kc, vc,
        row3(ks_new), row3(vs_new), row3(kw_new), row3(vw_new), kwt_new, vwt_new,
        fm(win_k), fm(win_v), j * b)
    x = matmul_res(o16.reshape(b, NSA_HEADS * HEAD_DIM), w["w_out"], x, b)
    kvshape = (b, 1, NSA_KV_HEADS, HEAD_DIM)
    rows = tuple(a.reshape(kvshape) for a in (kc_new, vc_new, ks_new, vs_new))
    wins = tuple(_kv_rows_view(a) for a in (wk_out, wv_out))
    return x, rows + wins


def _diff_weights(w_in, q_norm, k_norm, lam_vec, sub_norm, w_out):
    nmap = 2 * DIFF_HEADS
    gain = jnp.concatenate([jnp.tile(q_norm, nmap), jnp.tile(k_norm, nmap),
                            jnp.ones((D_MODEL,), F32)]).reshape(1, 3 * D_MODEL)
    return dict(w_in=w_in.astype(BF16), gain=gain, lam=lam_vec, sub=sub_norm, w_out=w_out.astype(BF16))


def kernel(x_prompt, x_sample, cache_nsa_cmp_k, cache_nsa_cmp_v, cache_nsa_slc_k, cache_nsa_slc_v, state_nsa_win_k, state_nsa_win_v, cache_diff_k, cache_diff_v, state_ssd_conv, state_ssd_ssm, state_ffn_conv, page_table, norm_mix, norm_ffn, nsa_w_in, nsa_q_norm, nsa_k_norm, nsa_cmp_pos, nsa_cmp_w1, nsa_cmp_w2, nsa_w_out, diff_w_in, diff_q_norm, diff_k_norm, diff_lambda, diff_sub_norm, diff_w_out, ssd_w_in, ssd_conv_w, ssd_conv_b, ssd_dt_bias, ssd_a_log, ssd_d, ssd_norm, ssd_w_out, ffn_w_up, ffn_conv_w, ffn_w_down):
    bp, tp, _ = x_prompt.shape
    bs, ts, _ = x_sample.shape
    assert ts == 1 and tp % Q_BLOCK == 0
    n_pages = page_table.shape[1]
    past = n_pages * PAGE_SIZE
    depth = norm_mix.shape[0]
    tm_p = min(PROMPT_ROW_TILE, tp)
    xp = x_prompt.reshape(bp * tp, D_MODEL)
    xs = x_sample.reshape(bs, D_MODEL)
    bd = _block_diag_ones()
    tabs_p = _rope_tables(jnp.arange(tp, dtype=jnp.int32), tp)
    tabs_s = _rope_tables(jnp.full((1,), past, jnp.int32), bs)
    nsa_p, nsa_s, diff_p, diff_s, ssd_p, ssd_s, ffn_p, ffn_s = [], [], [], [], [], [], [], []
    for i in range(depth):
        kind = i % N_MIXERS
        j = i // N_MIXERS
        if kind == 0:
            w = _nsa_weights(nsa_w_in[j], nsa_q_norm[j], nsa_k_norm[j], nsa_cmp_pos[j], nsa_cmp_w1[j],
                             nsa_cmp_w2[j], nsa_w_out[j])
            xp, outs = _nsa_prompt_layer(xp, norm_mix[i], w, tabs_p, bd, bp, tp, tm_p)
            nsa_p.append(outs)
            xs, outs = _nsa_sample_layer(xs, norm_mix[i], w, tabs_s, bd, j, page_table,
                                         (cache_nsa_cmp_k, cache_nsa_cmp_v, cache_nsa_slc_k, cache_nsa_slc_v),
                                         state_nsa_win_k, state_nsa_win_v)
            nsa_s.append(outs)
        elif kind == 1:
            lam_init = 0.8 - 0.6 * math.exp(-0.3 * i)
            w = _diff_weights(diff_w_in[j], diff_q_norm[j], diff_k_norm[j], diff_lambda[j], diff_sub_norm[j],
                              diff_w_out[j])
            proj = norm_matmul(xp, norm_mix[i], w["w_in"], tm_p)
            q, k, v, kt, vt = post_proj(proj, w["gain"], tabs_p[0], tabs_p[1], bd, DIFF_PLAN_P, DIFF_OUTS_P,
                                        tm_p, bp)
            o = diff_prompt_attn(q, k, vt, w["lam"], w["sub"], lam_init, bp, tp)
            xp = matmul_res(o, w["w_out"], xp, tm_p)
            diff_p.append((kt.reshape(bp, DIFF_HEADS, 2, HEAD_DIM, tp).transpose(0, 4, 1, 2, 3),
                           v.reshape(bp, tp, DIFF_HEADS, 2 * HEAD_DIM)))
            n_phys = cache_diff_k.shape[1]
            proj = norm_matmul(xs, norm_mix[i], w["w_in"], bs)
            q, k, v = post_proj(proj, w["gain"], tabs_s[0], tabs_s[1], bd, DIFF_PLAN_S, DIFF_OUTS_S, bs, 1)
            r3 = lambda a: a.reshape(bs, 1, D_MODEL)
            pool_k = cache_diff_k.transpose(0, 1, 3, 4, 5, 2).reshape(-1, D_MODEL, PAGE_SIZE)
            pool_v = cache_diff_v.reshape(-1, PAGE_SIZE * DIFF_HEADS, 2 * HEAD_DIM)
            o = diff_sample_attn(page_table, pool_k, pool_v, j * n_phys,
                                 r3(q), r3(k), v.reshape(bs, DIFF_HEADS, 2 * HEAD_DIM), w["lam"],
                                 w["sub"].reshape(1, 2 * HEAD_DIM), lam_init)
            xs = matmul_res(o.reshape(bs, D_MODEL), w["w_out"], xs, bs)
            diff_s.append((k.reshape(bs, 1, DIFF_HEADS, 2, HEAD_DIM), v.reshape(bs, 1, DIFF_HEADS, 2 * HEAD_DIM)))
        else:
            w_in = _pad_cols(ssd_w_in[j], SSD_IN_PAD).astype(BF16)
            w_out = ssd_w_out[j].astype(BF16)
            pad_h = lambda a: jnp.pad(a, (0, LANE - SSD_HEADS)).reshape(1, LANE)
            bias_row, alog_row = pad_h(ssd_dt_bias[j]), pad_h(ssd_a_log[j])
            dskip_row = jnp.repeat(ssd_d[j], SSD_HEADDIM).reshape(1, SSD_D_INNER)
            xbc0 = SSD_D_INNER
            proj = norm_matmul(xp, norm_mix[i], w_in, tm_p)
            xbc = ssd_conv_prompt(proj, ssd_conv_w[j], ssd_conv_b[j], bp, tp, tm_p)
            y, st = ssd_scan_prompt(xbc, proj, bias_row, alog_row, dskip_row, bp, tp)
            xp = ssd_out(y, proj, ssd_norm[j], w_out, xp, tm_p)
            conv_new = proj.reshape(bp, tp, SSD_IN_PAD)[:, tp - (SSD_CONV_W - 1):, xbc0:xbc0 + SSD_CONV_DIM]
            ssm_new = st.reshape(bp, SSD_D_STATE, SSD_HEADS, SSD_HEADDIM).transpose(0, 2, 3, 1)
            ssd_p.append((conv_new, ssm_new))
            proj = norm_matmul(xs, norm_mix[i], w_in, bs)
            prev = state_ssd_conv[j]
            xbc = ssd_conv_sample(proj, prev.transpose(1, 0, 2), ssd_conv_w[j], ssd_conv_b[j])
            y, st = ssd_step_sample(xbc, proj, bias_row, alog_row, dskip_row,
                                    state_ssd_ssm.reshape((-1,) + state_ssd_ssm.shape[2:]), j * bs)
            xs = ssd_out(y.reshape(bs, SSD_D_INNER), proj, ssd_norm[j], w_out, xs, bs)
            conv_new = jnp.concatenate([prev[:, 1:], proj[:, None, xbc0:xbc0 + SSD_CONV_DIM]], axis=1)
            ssd_s.append((conv_new, st))
        w_up = ffn_w_up[i].astype(BF16)
        w_dn = ffn_w_down[i].astype(BF16)
        xp, st = ffn_prompt(xp, norm_ffn[i], w_up, ffn_conv_w[i], w_dn, bp, tp, tm_p)
        ffn_p.append(st[:, SUBLANE - (FFN_CONV_W - 1):])
        prev = state_ffn_conv[i]
        xs, up = ffn_sample(xs, norm_ffn[i], w_up, ffn_conv_w[i], w_dn, prev)
        ffn_s.append(jnp.concatenate([prev[:, 1:], up[:, None]], axis=1))
    outs = [xp.reshape(bp, tp, D_MODEL), xs.reshape(bs, ts, D_MODEL)]
    for r in range(6):
        outs.append(jnp.stack([o[r] for o in nsa_p]))
        outs.append(jnp.stack([o[r] for o in nsa_s]))
    for r in range(2):
        outs.append(jnp.stack([o[r] for o in diff_p]))
        outs.append(jnp.stack([o[r] for o in diff_s]))
    for r in range(2):
        outs.append(jnp.stack([o[r] for o in ssd_p]))
        outs.append(jnp.stack([o[r] for o in ssd_s]))
    outs.append(jnp.stack(ffn_p))
    outs.append(jnp.stack(ffn_s))
    return tuple(outs)
```

```python
import functools
import math

import jax
import jax.numpy as jnp
from jax import lax
from jax.experimental import pallas as pl
from jax.experimental.pallas import tpu as pltpu

F32 = jnp.float32
BF16 = jnp.bfloat16

D_MODEL = 1024
DEPTH = 4
PAGE_SIZE = 128
N_MIXERS = 3
RMS_EPS = 1e-6
ROPE_THETA = 10000.0
Q_BLOCK = 128
NEG_INF = -1e30
HEAD_DIM = 64
SCALE = HEAD_DIM ** -0.5
LOG2E = 1.4426950408889634

NSA_HEADS = D_MODEL // HEAD_DIM
NSA_KV_HEADS = 4
NSA_GROUP = NSA_HEADS // NSA_KV_HEADS
CMP_BLOCK = 32
SEL_BLOCK = 64
N_SEL = 8
WINDOW = 512
FORCED_SCORE = 1e4
NSA_KEY_TILE = 256
NSA_KV = NSA_KV_HEADS * HEAD_DIM
NSA_QKV = NSA_HEADS * HEAD_DIM + 6 * NSA_KV
NSA_IN_PAD = NSA_QKV + 128

DIFF_HEADS = D_MODEL // (2 * HEAD_DIM)
DIFF_Q_BLOCK = 512

SSD_D_INNER = 2 * D_MODEL
SSD_HEADDIM = 64
SSD_HEADS = SSD_D_INNER // SSD_HEADDIM
SSD_GROUPS = 4
SSD_D_STATE = 128
SSD_CONV_W = 4
SSD_CHUNK = 128
SSD_CONV_DIM = SSD_D_INNER + 2 * SSD_GROUPS * SSD_D_STATE
SSD_IN_PAD = SSD_D_INNER + SSD_CONV_DIM + 128
SSD_GROUP_W = SSD_D_INNER // SSD_GROUPS
SSD_HEADS_PER_GROUP = SSD_HEADS // SSD_GROUPS

D_FF = 2816
FFN_CONV_W = 3
FFN_CHUNK = 256

PROMPT_ROW_TILE = 512
LANE = 128
SUBLANE = 8
VMEM_LIMIT = 56 * 1024 * 1024
NSA_SAMPLE_PER_STEP = 2
INVALID = -3.2e38

NT_DIMS = (((1,), (1,)), ((), ()))


def _params(*sem):
    return pltpu.CompilerParams(dimension_semantics=sem, vmem_limit_bytes=VMEM_LIMIT)


def _dot(a, b):
    return jnp.dot(a, b, preferred_element_type=F32)


def _dot_nt(a, b):
    return lax.dot_general(a, b, NT_DIMS, preferred_element_type=F32)


def _split2(x):
    hi = x.astype(BF16)
    lo = (x - hi.astype(F32)).astype(BF16)
    return hi, lo


def _split3(x):
    hi = x.astype(BF16)
    r = x - hi.astype(F32)
    mid = r.astype(BF16)
    lo = (r - mid.astype(F32)).astype(BF16)
    return hi, mid, lo


def _dot_exact_rhs(x, m):
    hi, mid, lo = _split3(x)
    return _dot(hi, m) + _dot(mid, m) + _dot(lo, m)


def _lhs_exact_dot(m, x):
    hi, mid, lo = _split3(x)
    return _dot(m, hi) + _dot(m, mid) + _dot(m, lo)


def _sigmoid(x):
    return 1.0 / (1.0 + jnp.exp(-x))


def _silu(x):
    return x * _sigmoid(x)


def _softplus(x):
    return jnp.maximum(x, 0.0) + jnp.log1p(jnp.exp(-jnp.abs(x)))


def _gelu_tanh(x):
    return 0.5 * x * (1.0 + jnp.tanh(math.sqrt(2.0 / math.pi) * (x + 0.044715 * (x * x * x))))


def _rms_rows(x, g):
    return x * lax.rsqrt(jnp.mean(x * x, axis=-1, keepdims=True) + RMS_EPS) * g


def _norm_matmul_kernel(x_ref, g_ref, w_ref, o_ref, *, chunk):
    h = _rms_rows(x_ref[...], g_ref[...]).astype(BF16)
    n = o_ref.shape[1]
    for c in range(0, n, chunk):
        w = min(chunk, n - c)
        o_ref[:, c:c + w] = _dot(h, w_ref[:, c:c + w])


def norm_matmul(x, g, w, tm):
    m, k = x.shape
    n = w.shape[1]
    return pl.pallas_call(
        functools.partial(_norm_matmul_kernel, chunk=512),
        grid=(m // tm,),
        in_specs=[pl.BlockSpec((tm, k), lambda i: (i, 0)),
                  pl.BlockSpec((1, k), lambda i: (0, 0)),
                  pl.BlockSpec((k, n), lambda i: (0, 0))],
        out_specs=pl.BlockSpec((tm, n), lambda i: (i, 0)),
        out_shape=jax.ShapeDtypeStruct((m, n), F32),
        compiler_params=_params("parallel"),
        name="norm_matmul",
    )(x, g.reshape(1, k), w)


def _matmul_res_kernel(a_ref, w_ref, r_ref, o_ref):
    o_ref[...] = r_ref[...] + _dot(a_ref[...].astype(BF16), w_ref[...])


def matmul_res(a, w, res, tm):
    m, k = a.shape
    n = w.shape[1]
    return pl.pallas_call(
        _matmul_res_kernel,
        grid=(m // tm,),
        in_specs=[pl.BlockSpec((tm, k), lambda i: (i, 0)),
                  pl.BlockSpec((k, n), lambda i: (0, 0)),
                  pl.BlockSpec((tm, n), lambda i: (i, 0))],
        out_specs=pl.BlockSpec((tm, n), lambda i: (i, 0)),
        out_shape=jax.ShapeDtypeStruct((m, n), F32),
        compiler_params=_params("parallel"),
        name="matmul_res",
    )(a, w, res)


def _head_norm_rope(x, gain, cos, sin_signed, bd):
    hi, lo = _split2(x * x)
    ss = _dot(hi, bd) + _dot(lo, bd)
    y = x * lax.rsqrt(ss * (1.0 / HEAD_DIM) + RMS_EPS) * gain
    lane = lax.broadcasted_iota(jnp.int32, y.shape, 1)
    half = HEAD_DIM // 2
    width = y.shape[1]
    partner = jnp.where((lane & half) != 0, pltpu.roll(y, half, 1), pltpu.roll(y, width - half, 1))
    return y * cos + partner * sin_signed


def _post_kernel(p_ref, gain_ref, cos_ref, sin_ref, bd_ref, *out_refs, plan):
    cos = cos_ref[...]
    sin = sin_ref[...]
    bd = bd_ref[...]
    for mode, src, width, dests in plan:
        x = p_ref[:, src:src + width]
        if mode == "rope":
            y = _head_norm_rope(x, gain_ref[:, src:src + width], cos, sin, bd)
        elif mode == "sigmoid":
            y = _sigmoid(x)
        else:
            y = x
        for oi, oc, transposed in dests:
            if transposed:
                out_refs[oi][0, oc:oc + width, :] = y.T
            else:
                out_refs[oi][:, oc:oc + width] = y


def post_proj(proj, gain_row, cos, sin, bd, plan, outs, tm, b):
    m, n = proj.shape
    n_tab = cos.shape[0] // tm
    nt = m // b // tm
    out_specs, out_shape = [], []
    for w, transposed in outs:
        if transposed:
            out_specs.append(pl.BlockSpec((1, w, tm), lambda i: (i // nt, 0, i % nt)))
            out_shape.append(jax.ShapeDtypeStruct((b, w, m // b), F32))
        else:
            out_specs.append(pl.BlockSpec((tm, w), lambda i: (i, 0)))
            out_shape.append(jax.ShapeDtypeStruct((m, w), F32))
    return pl.pallas_call(
        functools.partial(_post_kernel, plan=plan),
        grid=(m // tm,),
        in_specs=[pl.BlockSpec((tm, n), lambda i: (i, 0)),
                  pl.BlockSpec(gain_row.shape, lambda i: (0, 0)),
                  pl.BlockSpec((tm, 256), lambda i: (i % n_tab, 0)),
                  pl.BlockSpec((tm, 256), lambda i: (i % n_tab, 0)),
                  pl.BlockSpec((256, 256), lambda i: (0, 0))],
        out_specs=out_specs,
        out_shape=out_shape,
        compiler_params=_params("parallel"),
        name="post_proj",
    )(proj, gain_row, cos, sin, bd)


NSA_PLAN_P = tuple(
    [("rope", c * 256, 256, ((0, c * 256, False),)) for c in range(4)]
    + [("rope", 1024, 256, ((1, 0, False), (6, 0, True))),
       ("copy", 1280, 256, ((2, 0, False), (7, 0, True))),
       ("rope", 1536, 256, ((3, 0, False), (8, 0, True))),
       ("copy", 1792, 256, ((9, 0, True),)),
       ("rope", 2048, 256, ((4, 0, False), (10, 0, True))),
       ("copy", 2304, 256, ((11, 0, True),)),
       ("sigmoid", 2560, 128, ((5, 0, False),))])
NSA_OUTS_P = ((1024, False), (256, False), (256, False), (256, False), (256, False), (128, False)) + ((256, True),) * 6
NSA_PLAN_S = tuple(
    [("rope", c * 256, 256, ((0, c * 256, False),)) for c in range(4)]
    + [("rope", 1024, 256, ((1, 0, False),)), ("copy", 1280, 256, ((2, 0, False),)),
       ("rope", 1536, 256, ((3, 0, False),)), ("copy", 1792, 256, ((4, 0, False),)),
       ("rope", 2048, 256, ((5, 0, False), (8, 0, True))),
       ("copy", 2304, 256, ((6, 0, False), (9, 0, True))),
       ("sigmoid", 2560, 128, ((7, 0, False),))])
NSA_OUTS_S = ((1024, False),) + ((256, False),) * 6 + ((128, False), (256, True), (256, True))

DIFF_PLAN_P = tuple(
    [("rope", c * 256, 256, ((0, c * 256, False),)) for c in range(4)]
    + [("rope", 1024 + c * 256, 256, ((1, c * 256, False), (3, c * 256, True))) for c in range(4)]
    + [("copy", 2048 + c * 256, 256, ((2, c * 256, False), (4, c * 256, True))) for c in range(4)])
DIFF_OUTS_P = ((1024, False),) * 3 + ((1024, True),) * 2
DIFF_PLAN_S = tuple(
    [("rope", c * 256, 256, ((0, c * 256, False),)) for c in range(4)]
    + [("rope", 1024 + c * 256, 256, ((1, c * 256, False),)) for c in range(4)]
    + [("copy", 2048 + c * 256, 256, ((2, c * 256, False),)) for c in range(4)])
DIFF_OUTS_S = ((1024, False),) * 3


def _block_rows(piece):
    mats = [jnp.concatenate([piece(t, g) for t in range(CMP_BLOCK)], axis=1) for g in range(NSA_KV_HEADS)]
    return jnp.concatenate(mats, axis=0)


def _compress_rows_kernel(x_ref, pos_ref, w1_ref, w2_ref, w1t_ref, w2t_ref, o_ref, ot_ref):
    xr = x_ref[0] + pos_ref[...]
    xg = _block_rows(lambda t, g: xr[:, t * NSA_KV + g * HEAD_DIM:t * NSA_KV + (g + 1) * HEAD_DIM]).astype(BF16)
    h = _gelu_tanh(_dot(xg, w1_ref[...]))
    o_ref[0] = _dot(h.astype(BF16), w2_ref[...])
    ht = _gelu_tanh(_dot_nt(w1t_ref[...], xg))
    ot_ref[0] = _dot(w2t_ref[...], ht.astype(BF16))


def compress_rows(rows, pos_row, w1, w2, n_blocks):
    b = rows.shape[0]
    wide = CMP_BLOCK * NSA_KV
    nr = NSA_KV_HEADS * n_blocks
    w1t, w2t = w1.T, w2.T
    const = lambda a: pl.BlockSpec(a.shape, lambda i: (0, 0))
    return pl.pallas_call(
        _compress_rows_kernel,
        grid=(b,),
        in_specs=[pl.BlockSpec((1, n_blocks, wide), lambda i: (i, 0, 0)),
                  const(pos_row), const(w1), const(w2), const(w1t), const(w2t)],
        out_specs=[pl.BlockSpec((1, nr, HEAD_DIM), lambda i: (i, 0, 0)),
                   pl.BlockSpec((1, HEAD_DIM, nr), lambda i: (i, 0, 0))],
        out_shape=[jax.ShapeDtypeStruct((b, nr, HEAD_DIM), F32),
                   jax.ShapeDtypeStruct((b, HEAD_DIM, nr), F32)],
        compiler_params=_params("parallel"),
        name="compress_rows",
    )(rows, pos_row, w1, w2, w1t, w2t)


def _page_map(bi, pt, *, j, base):
    return (base + pt[bi, j], 0, 0)


def _page_map_u(i, pt, *, j, u, per_step, base):
    return (base + pt[i * per_step + u, j], 0, 0)


def _compress_pages_kernel(pt_ref, *refs, n_pages):
    del pt_ref
    x_refs = refs[:n_pages]
    pos_ref, w1_ref, w2_ref, o_ref, tok_ref = refs[n_pages:]
    halves = NSA_KV // LANE
    per_half = LANE // HEAD_DIM
    for j in range(n_pages):
        xt = x_refs[j][0].T
        for c in range(halves):
            tok_ref[c, j * PAGE_SIZE:(j + 1) * PAGE_SIZE, :] = xt[:, c * LANE:(c + 1) * LANE]
    n_blocks = n_pages * (PAGE_SIZE // CMP_BLOCK)
    ys = [[tok_ref[c, pl.ds(t, n_blocks, stride=CMP_BLOCK), :] for c in range(halves)] for t in range(CMP_BLOCK)]
    xg = _block_rows(lambda t, g: ys[t][g // per_half][:, (g % per_half) * HEAD_DIM:(g % per_half + 1) * HEAD_DIM])
    xg = xg + pos_ref[...]
    h = _gelu_tanh(_dot(xg.astype(BF16), w1_ref[...]))
    o_ref[0] = _dot(h.astype(BF16), w2_ref[...])


def compress_pages(pool, page_table, base, pos_row, w1, w2):
    b, n_pages = page_table.shape
    n_blocks = n_pages * (PAGE_SIZE // CMP_BLOCK)
    page_specs = [pl.BlockSpec((1, NSA_KV, PAGE_SIZE), functools.partial(_page_map, j=j, base=base))
                  for j in range(n_pages)]
    const = lambda a: pl.BlockSpec(a.shape, lambda i, pt: (0, 0))
    return pl.pallas_call(
        functools.partial(_compress_pages_kernel, n_pages=n_pages),
        grid_spec=pltpu.PrefetchScalarGridSpec(
            num_scalar_prefetch=1,
            grid=(b,),
            in_specs=page_specs + [const(pos_row), const(w1), const(w2)],
            out_specs=pl.BlockSpec((1, NSA_KV_HEADS * n_blocks, HEAD_DIM), lambda i, pt: (i, 0, 0)),
            scratch_shapes=[pltpu.VMEM((NSA_KV // LANE, n_pages * PAGE_SIZE, LANE), F32)]),
        out_shape=jax.ShapeDtypeStruct((b, NSA_KV_HEADS * n_blocks, HEAD_DIM), F32),
        compiler_params=_params("parallel"),
        name="compress_pages",
    )(page_table, *([pool] * n_pages), pos_row, w1, w2)


def _select_blocks(imp_b, cur, n_sb, k_sel):
    lane = lax.broadcasted_iota(jnp.int32, imp_b.shape, 1)
    forced = (lane == cur) | (lane == 0)
    score = jnp.where(forced, FORCED_SCORE, imp_b)
    score = jnp.where(lane > cur, NEG_INF, score)
    score = jnp.where(lane >= n_sb, INVALID, score)
    rank = jnp.zeros(imp_b.shape, F32)
    for j in range(n_sb):
        rival = score[:, j:j + 1]
        rank = rank + jnp.where((rival > score) | ((rival == score) & (lane > j)), 1.0, 0.0)
    return jnp.where((rank < k_sel) & (lane < n_sb), 1.0, 0.0)


def _select_blocks_t(imp_b, cur, n_sb, k_sel):
    blk = lax.broadcasted_iota(jnp.int32, imp_b.shape, 0)
    forced = (blk == cur) | (blk == 0)
    score = jnp.where(forced, FORCED_SCORE, imp_b)
    score = jnp.where(blk > cur, NEG_INF, score)
    score = jnp.where(blk >= n_sb, INVALID, score)
    rank = jnp.zeros(imp_b.shape, F32)
    for j in range(n_sb):
        rival = score[j:j + 1, :]
        rank = rank + jnp.where((rival > score) | ((rival == score) & (blk > j)), 1.0, 0.0)
    return jnp.where((rank < k_sel) & (blk < n_sb), 1.0, 0.0)


def _masked_softmax(s, vis, axis, exp_fn=jnp.exp):
    sm = jnp.where(vis, s, NEG_INF)
    m = jnp.max(sm, axis=axis, keepdims=True)
    e = jnp.where(vis, exp_fn(sm - m), 0.0)
    den = jnp.sum(e, axis=axis, keepdims=True)
    return e / jnp.where(den > 0.0, den, 1.0)


def _masked_softmax_rows(s, vis):
    return _masked_softmax(s, vis, -1)


def _block_diag(blocks):
    n = len(blocks)
    r, c = blocks[0].shape
    rows = []
    for i, blk in enumerate(blocks):
        parts = ([jnp.zeros((r, i * c), blk.dtype)] if i else []) + [blk]
        parts += [jnp.zeros((r, (n - 1 - i) * c), blk.dtype)] if i < n - 1 else []
        rows.append(jnp.concatenate(parts, axis=1))
    return jnp.concatenate(rows, axis=0)


def _flash_init(dv, nq):
    return (jnp.full((1, nq), NEG_INF, F32), jnp.zeros((1, nq), F32), jnp.zeros((dv, nq), F32))


def _flash_step(k, vt, qt_bd, carry, bias, n_blocks):
    m, l, acc = carry
    s = _dot(k, qt_bd)
    if bias is not None:
        s = s + bias
    m_new = jnp.maximum(m, jnp.max(s, axis=0, keepdims=True))
    alpha = jnp.exp2(m - m_new)
    p = jnp.exp2(s - m_new)
    l = alpha * l + jnp.sum(p, axis=0, keepdims=True)
    pv = _dot(vt, p.astype(BF16))
    if n_blocks > 1:
        dv = pv.shape[0] // n_blocks
        w = pv.shape[1] // n_blocks
        pv = jnp.concatenate([pv[g * dv:(g + 1) * dv, g * w:(g + 1) * w] for g in range(n_blocks)], axis=1)
    return m_new, l, alpha * acc + pv


def _nsa_prompt_kernel(q_ref, gt_ref, kc_ref, vct_ref, ks_ref, vst_ref, kw_ref, vwt_ref,
                       poolt_ref, expandt_ref, o_ref, selb_ref, winb_ref, *, n_cmp, n_sb, k_sel):
    qi = pl.program_id(1)
    qb = Q_BLOCK
    rep = NSA_GROUP
    t = winb_ref.shape[0]
    qt = (q_ref[...] * (SCALE * LOG2E)).T.astype(BF16)
    gt = gt_ref[...].T
    pq = qi * qb + lax.broadcasted_iota(jnp.int32, (1, qb), 1)
    pq_rep = jnp.concatenate([pq] * rep, axis=1)
    kts = min(NSA_KEY_TILE, t)
    hi_kt = (qi * qb + qb + kts - 1) // kts
    lo_win = jnp.maximum(qi * qb - WINDOW, 0) // kts
    dist = pq - lax.broadcasted_iota(jnp.int32, (t, qb), 0)
    causal = dist >= 0
    winb_ref[...] = jnp.where(causal & (dist < WINDOW), 0.0, NEG_INF)
    poolt = poolt_ref[...]
    expandt = expandt_ref[...]
    qgs, o_cs = [], []
    for g in range(NSA_KV_HEADS):
        qg = jnp.concatenate(
            [qt[(rep * g + r) * HEAD_DIM:(rep * g + r + 1) * HEAD_DIM, :] for r in range(rep)], axis=1)

        kc = kc_ref[0, g * n_cmp:(g + 1) * n_cmp, :].astype(BF16)
        vct = vct_ref[0, :, g * n_cmp:(g + 1) * n_cmp].astype(BF16)
        s_c = _dot(kc, qg)
        nrow = lax.broadcasted_iota(jnp.int32, s_c.shape, 0)
        vis = ((nrow + 1) * CMP_BLOCK - 1) <= pq_rep
        p_c = _masked_softmax(s_c, vis, 0, jnp.exp2)
        o_cs.append(_dot(vct, p_c.astype(BF16)))
        imp = p_c[:, 0:qb]
        for r in range(1, rep):
            imp = imp + p_c[:, r * qb:(r + 1) * qb]
        hi_, lo_ = _split2(imp)
        imp_b = _dot(poolt, hi_) + _dot(poolt, lo_)
        sel = _select_blocks_t(imp_b, pq // SEL_BLOCK, n_sb, k_sel)
        picked = _dot(expandt, sel.astype(BF16)) > 0.5
        selb_ref[g] = jnp.where(picked & causal, 0.0, NEG_INF)
        qgs.append(qg)

    qt_bd = _block_diag(qgs)
    ng = NSA_KV_HEADS
    nq = ng * rep * qb

    def step(k_ref, vt_ref, bias_fn, kt, carry):
        off = pl.multiple_of(kt * kts, kts)
        k = k_ref[pl.ds(off, kts), :].astype(BF16)
        vt = vt_ref[0, :, pl.ds(off, kts)].astype(BF16)
        return _flash_step(k, vt, qt_bd, carry, bias_fn(off), ng)

    def sel_bias(off):
        return jnp.concatenate([selb_ref[g, pl.ds(off, kts), :] for g in range(ng) for _ in range(rep)], axis=1)

    def win_bias(off):
        return jnp.concatenate([winb_ref[pl.ds(off, kts), :]] * (ng * rep), axis=1)

    sel_step = functools.partial(step, ks_ref, vst_ref, sel_bias)
    win_step = functools.partial(step, kw_ref, vwt_ref, win_bias)
    carry_s = lax.fori_loop(0, lo_win, sel_step, _flash_init(HEAD_DIM, nq))
    carry_s, carry_w = lax.fori_loop(
        lo_win, hi_kt, lambda kt, c: (sel_step(kt, c[0]), win_step(kt, c[1])),
        (carry_s, _flash_init(HEAD_DIM, nq)))
    o_s = carry_s[2] / carry_s[1]
    o_w = carry_w[2] / carry_w[1]

    pieces = []
    for g in range(ng):
        for r in range(rep):
            h = rep * g + r
            lanes = slice(r * qb, (r + 1) * qb)
            wide = slice(h * qb, (h + 1) * qb)
            pieces.append(gt[3 * h:3 * h + 1, :] * o_cs[g][:, lanes]
                          + gt[3 * h + 1:3 * h + 2, :] * o_s[:, wide]
                          + gt[3 * h + 2:3 * h + 3, :] * o_w[:, wide])
    o_ref[...] = jnp.concatenate(pieces, axis=0).T


def nsa_prompt_attn(q, gates, kc, vct, ks, vst, kw, vwt, b, t):
    nq = t // Q_BLOCK
    n_cmp = t // CMP_BLOCK
    n_sb = -(-t // SEL_BLOCK)
    nsb_pad = -(-n_sb // SUBLANE) * SUBLANE
    k_sel = min(N_SEL, n_sb)
    ratio = SEL_BLOCK // CMP_BLOCK
    poolt = (jnp.arange(nsb_pad)[:, None] == jnp.arange(n_cmp)[None, :] // ratio).astype(BF16)
    expandt = (jnp.arange(t)[:, None] // SEL_BLOCK == jnp.arange(nsb_pad)[None, :]).astype(BF16)
    tok = lambda bi, qi: (bi * nq + qi, 0)
    seq = lambda bi, qi: (bi, 0)
    seq3 = lambda bi, qi: (bi, 0, 0)
    return pl.pallas_call(
        functools.partial(_nsa_prompt_kernel, n_cmp=n_cmp, n_sb=n_sb, k_sel=k_sel),
        grid=(b, nq),
        in_specs=[pl.BlockSpec((Q_BLOCK, NSA_HEADS * HEAD_DIM), tok),
                  pl.BlockSpec((Q_BLOCK, LANE), tok),
                  pl.BlockSpec((1, NSA_KV_HEADS * n_cmp, HEAD_DIM), seq3),
                  pl.BlockSpec((1, HEAD_DIM, NSA_KV_HEADS * n_cmp), seq3),
                  pl.BlockSpec((t, NSA_KV), seq), pl.BlockSpec((1, NSA_KV, t), seq3),
                  pl.BlockSpec((t, NSA_KV), seq), pl.BlockSpec((1, NSA_KV, t), seq3),
                  pl.BlockSpec(poolt.shape, lambda bi, qi: (0, 0)),
                  pl.BlockSpec(expandt.shape, lambda bi, qi: (0, 0))],
        out_specs=pl.BlockSpec((Q_BLOCK, NSA_HEADS * HEAD_DIM), tok),
        out_shape=jax.ShapeDtypeStruct((b * t, NSA_HEADS * HEAD_DIM), F32),
        scratch_shapes=[pltpu.VMEM((NSA_KV_HEADS, t, Q_BLOCK), F32), pltpu.VMEM((t, Q_BLOCK), F32)],
        compiler_params=_params("parallel", "arbitrary"),
        name="nsa_prompt_attn",
    )(q, gates, kc, vct, ks, vst, kw, vwt, poolt, expandt)


def _fold_groups(x, rowg, width):
    out = jnp.where(rowg == 0, x[:, 0:width], 0.0)
    for g in range(1, NSA_KV_HEADS):
        out = out + jnp.where(rowg == g, x[:, g * width:(g + 1) * width], 0.0)
    return out


def _place_groups(x, rowg):
    return jnp.concatenate([jnp.where(rowg == g, x, 0.0) for g in range(NSA_KV_HEADS)], axis=1)


def _nsa_sample_kernel(pt_ref, *refs, n_pages, per_step, past, buf_len, n_cmp, n_sb, k_sel):
    del pt_ref
    for u in range(per_step):
        _nsa_sample_one(u, refs[u * n_pages:(u + 1) * n_pages],
                        refs[(per_step + u) * n_pages:(per_step + u + 1) * n_pages], refs[2 * per_step * n_pages:],
                        pl.program_id(0) * per_step + u, n_pages, past, buf_len, n_cmp, n_sb, k_sel)


def _nsa_sample_one(u, ksp, vsp, refs, bi, n_pages, past, buf_len, n_cmp, n_sb, k_sel):
    (q_ref, gt_ref, kc_ref, vc_ref, ksn_ref, vsn_ref, kwn_ref, vwn_ref, kwnt_ref, vwnt_ref, wk_ref, wv_ref,
     pool_ref, expand_ref, gsel_ref, gselt_ref, o_ref, wko_ref, wvo_ref) = refs
    nh = NSA_HEADS
    rowg = lax.broadcasted_iota(jnp.int32, (nh, 1), 0) // NSA_GROUP
    q16 = q_ref[u] * SCALE
    qmat_f = _place_groups(q16, rowg)
    qmat = qmat_f.astype(BF16)

    s_all = _dot_nt(q16.astype(BF16), kc_ref[u].astype(BF16))
    s_c = _fold_groups(s_all, rowg, n_cmp)
    ncol = lax.broadcasted_iota(jnp.int32, s_c.shape, 1)
    vis = ((ncol + 1) * CMP_BLOCK - 1) <= past
    p_c = _masked_softmax_rows(s_c, vis)
    o_c = _dot(_place_groups(p_c, rowg).astype(BF16), vc_ref[u].astype(BF16))

    gsel = gsel_ref[...]
    hi_, lo_ = _split2(p_c)
    imp = _dot(gsel, hi_) + _dot(gsel, lo_)
    hi_, lo_ = _split2(imp)
    imp_b = _dot(hi_, pool_ref[...]) + _dot(lo_, pool_ref[...])
    sel = _select_blocks(imp_b, past // SEL_BLOCK, n_sb, k_sel)
    sel16 = _dot(gselt_ref[...], sel.astype(BF16))
    maskfull = _dot(sel16.astype(BF16), expand_ref[...])

    def attend(scores, valids, s_new, valid_new, values, v_new):
        m = s_new if valid_new is None else jnp.where(valid_new > 0.5, s_new, NEG_INF)
        for s, vd in zip(scores, valids):
            m = jnp.maximum(m, jnp.max(jnp.where(vd > 0.5, s, NEG_INF), axis=-1, keepdims=True))
        p_new = jnp.exp(s_new - m)
        if valid_new is not None:
            p_new = p_new * valid_new
        l = p_new
        acc = p_new * v_new
        for s, vd, v in zip(scores, valids, values):
            p = jnp.exp(jnp.where(vd > 0.5, s, NEG_INF) - m) * vd
            l = l + jnp.sum(p, axis=-1, keepdims=True)
            acc = acc + _dot_nt(p.astype(BF16), v)
        return acc / l

    kt_all = jnp.concatenate([ksp[j][0].astype(BF16) for j in range(n_pages)], axis=1)
    vt_all = jnp.concatenate([vsp[j][0].astype(BF16) for j in range(n_pages)], axis=1)
    s_new = jnp.sum(qmat_f * ksn_ref[u], axis=-1, keepdims=True)
    o_s = attend([_dot(qmat, kt_all)], [maskfull[:, 0:past]], s_new, maskfull[:, past:past + 1],
                 [vt_all], vsn_ref[u])
    o_s = _fold_groups(o_s, rowg, HEAD_DIM)

    wk = wk_ref[u]
    wv = wv_ref[u]
    s_w = _dot(qmat, wk.astype(BF16))
    wcol = lax.broadcasted_iota(jnp.int32, s_w.shape, 1)
    pos_w = past - buf_len + wcol
    valid_w = jnp.where((past - pos_w < WINDOW) & (pos_w >= 0), 1.0, 0.0)
    s_new = jnp.sum(qmat_f * kwn_ref[u], axis=-1, keepdims=True)
    o_w = attend([s_w], [valid_w], s_new, None, [wv.astype(BF16)], vwn_ref[u])
    o_w = _fold_groups(o_w, rowg, HEAD_DIM)

    gt = gt_ref[u]
    o_ref[u] = gt[:, 0:1] * o_c + gt[:, 1:2] * o_s + gt[:, 2:3] * o_w

    bsel = lax.broadcasted_iota(jnp.int32, kwnt_ref.shape[1:], 1) == bi
    k_col = jnp.sum(jnp.where(bsel, kwnt_ref[0], 0.0), axis=-1, keepdims=True)
    v_col = jnp.sum(jnp.where(bsel, vwnt_ref[0], 0.0), axis=-1, keepdims=True)
    wlane = lax.broadcasted_iota(jnp.int32, (1, buf_len), 1)
    wko_ref[u] = jnp.where(wlane == buf_len - 1, k_col, pltpu.roll(wk, buf_len - 1, 1))
    wvo_ref[u] = jnp.where(wlane == buf_len - 1, v_col, pltpu.roll(wv, buf_len - 1, 1))


def nsa_sample_attn(page_table, pool_k, pool_v, base, q16, gates, kc, vc, ks_new, vs_new,
                    kw_new, vw_new, kwt_new, vwt_new, win_k, win_v, win_base):
    b, n_pages = page_table.shape
    past = n_pages * PAGE_SIZE
    buf_len = win_k.shape[2]
    tk = past + 1
    n_cmp = tk // CMP_BLOCK
    n_sb = -(-tk // SEL_BLOCK)
    k_sel = min(N_SEL, n_sb)
    ratio = SEL_BLOCK // CMP_BLOCK
    pool = (jnp.arange(n_cmp)[:, None] // ratio == jnp.arange(LANE)[None, :]).astype(BF16)
    expand = (jnp.arange(LANE)[:, None] == jnp.arange(past + LANE)[None, :] // SEL_BLOCK).astype(BF16)
    gsel = (jnp.arange(SUBLANE)[:, None] == jnp.arange(NSA_HEADS)[None, :] // NSA_GROUP).astype(BF16)
    gselt = gsel.T
    per_step = NSA_SAMPLE_PER_STEP if b % NSA_SAMPLE_PER_STEP == 0 else 1
    assert win_base % per_step == 0
    pages = lambda: [pl.BlockSpec((1, NSA_KV, PAGE_SIZE),
                                  functools.partial(_page_map_u, j=j, u=u, per_step=per_step, base=base))
                     for u in range(per_step) for j in range(n_pages)]
    per_b = lambda shape: pl.BlockSpec((per_step,) + shape, lambda i, pt: (i, 0, 0))
    const = lambda a: pl.BlockSpec(a.shape, lambda i, pt: (0,) * a.ndim)
    win = pl.BlockSpec((per_step, NSA_KV, buf_len), lambda i, pt: (win_base // per_step + i, 0, 0))
    return pl.pallas_call(
        functools.partial(_nsa_sample_kernel, n_pages=n_pages, per_step=per_step, past=past, buf_len=buf_len,
                          n_cmp=n_cmp, n_sb=n_sb, k_sel=k_sel),
        grid_spec=pltpu.PrefetchScalarGridSpec(
            num_scalar_prefetch=1,
            grid=(b // per_step,),
            in_specs=(pages() + pages()
                      + [per_b((NSA_HEADS, HEAD_DIM)), per_b((NSA_HEADS, 3)),
                         per_b((NSA_KV_HEADS * n_cmp, HEAD_DIM)), per_b((NSA_KV_HEADS * n_cmp, HEAD_DIM)),
                         per_b((1, NSA_KV)), per_b((1, NSA_KV)), per_b((1, NSA_KV)), per_b((1, NSA_KV)),
                         const(kwt_new), const(vwt_new),
                         win, win, const(pool), const(expand), const(gsel), const(gselt)]),
            out_specs=[per_b((NSA_HEADS, HEAD_DIM)), per_b((NSA_KV, buf_len)), per_b((NSA_KV, buf_len))]),
        out_shape=[jax.ShapeDtypeStruct((b, NSA_HEADS, HEAD_DIM), F32),
                   jax.ShapeDtypeStruct((b, NSA_KV, buf_len), F32),
                   jax.ShapeDtypeStruct((b, NSA_KV, buf_len), F32)],
        compiler_params=_params("parallel"),
        name="nsa_sample_attn",
    )(page_table, *([pool_k] * (per_step * n_pages)), *([pool_v] * (per_step * n_pages)), q16, gates, kc, vc,
      ks_new, vs_new, kw_new, vw_new, kwt_new, vwt_new, win_k, win_v, pool, expand, gsel, gselt)


def _diff_lambda(lam_ref, lam_init):
    lv = lam_ref[...]
    a = jnp.sum(lv[0:1] * lv[1:2], axis=-1, keepdims=True)
    c = jnp.sum(lv[2:3] * lv[3:4], axis=-1, keepdims=True)
    return jnp.exp(a) - jnp.exp(c) + lam_init


def _diff_prompt_kernel(q_ref, k_ref, vt_ref, lam_ref, sub_ref, o_ref, *, lam_init):
    si = pl.program_id(1)
    qs = q_ref.shape[0]
    qt = (q_ref[...] * (SCALE * LOG2E)).T.astype(BF16)
    kts = qs
    lane = lax.broadcasted_iota(jnp.int32, (kts, qs), 1)
    krow = lax.broadcasted_iota(jnp.int32, (kts, qs), 0)
    diag_bias = jnp.where(krow <= lane, 0.0, NEG_INF)
    diag_bias = jnp.concatenate([diag_bias] * 2, axis=1)
    lam = _diff_lambda(lam_ref, lam_init)
    vw = 2 * HEAD_DIM

    pieces = []
    for h in range(DIFF_HEADS):
        vrows = slice(h * vw, (h + 1) * vw)
        kcols = slice(2 * h * HEAD_DIM, (2 * h + 2) * HEAD_DIM)
        qt_bd = _block_diag([qt[(2 * h + c) * HEAD_DIM:(2 * h + c + 1) * HEAD_DIM, :] for c in range(2)])

        def tile(kt, carry, bias):
            off = pl.multiple_of(kt * kts, kts)
            k = k_ref[pl.ds(off, kts), kcols].astype(BF16)
            vt = vt_ref[0, vrows, pl.ds(off, kts)].astype(BF16)
            return _flash_step(k, vt, qt_bd, carry, bias, 1)

        carry = lax.fori_loop(0, si, lambda kt, c: tile(kt, c, None), _flash_init(vw, 2 * qs))
        _, l, acc = tile(si, carry, diag_bias)
        outs = acc / l
        o = outs[:, 0:qs] - lam * outs[:, qs:2 * qs]
        o = o * lax.rsqrt(jnp.mean(o * o, axis=0, keepdims=True) + RMS_EPS) * sub_ref[...]
        pieces.append(o * (1.0 - lam_init))
    o_ref[...] = jnp.concatenate(pieces, axis=0).T


def diff_prompt_attn(q, k, vt, lam_vec, sub_norm, lam_init, b, t):
    qs = min(DIFF_Q_BLOCK, t)
    nq = t // qs
    tok = lambda bi, qi: (bi * nq + qi, 0)
    return pl.pallas_call(
        functools.partial(_diff_prompt_kernel, lam_init=lam_init),
        grid=(b, nq),
        in_specs=[pl.BlockSpec((qs, D_MODEL), tok),
                  pl.BlockSpec((t, D_MODEL), lambda bi, qi: (bi, 0)),
                  pl.BlockSpec((1, D_MODEL, t), lambda bi, qi: (bi, 0, 0)),
                  pl.BlockSpec(lam_vec.shape, lambda bi, qi: (0, 0)),
                  pl.BlockSpec((2 * HEAD_DIM, 1), lambda bi, qi: (0, 0))],
        out_specs=pl.BlockSpec((qs, D_MODEL), tok),
        out_shape=jax.ShapeDtypeStruct((b * t, D_MODEL), F32),
        compiler_params=_params("parallel", "arbitrary"),
        name="diff_prompt_attn",
    )(q, k, vt, lam_vec, sub_norm.reshape(2 * HEAD_DIM, 1))


def _diff_sample_kernel(pt_ref, *refs, n_pages, lam_init):
    del pt_ref
    kp = refs[:n_pages]
    vp = refs[n_pages:2 * n_pages]
    q_ref, kn_ref, vn_ref, lam_ref, sub_ref, expm_ref, hmask_ref, o_ref = refs[2 * n_pages:]
    nmap = 2 * DIFF_HEADS
    lam = _diff_lambda(lam_ref, lam_init)
    r = lax.broadcasted_iota(jnp.int32, (nmap, D_MODEL), 0)
    col = lax.broadcasted_iota(jnp.int32, (nmap, D_MODEL), 1)
    own = jnp.where(r < DIFF_HEADS, 2 * r, 2 * (r - DIFF_HEADS) + 1)
    qmat_f = jnp.where(col // HEAD_DIM == own, q_ref[0] * SCALE, 0.0)
    qmat = qmat_f.astype(BF16)
    scores = [_dot(qmat, kp[j][0].astype(BF16)) for j in range(n_pages)]
    s_new = jnp.sum(qmat_f * kn_ref[0], axis=-1, keepdims=True)
    m = s_new
    for s in scores:
        m = jnp.maximum(m, jnp.max(s, axis=-1, keepdims=True))
    p_new = jnp.exp(s_new - m)
    ps = [jnp.exp(s - m) for s in scores]
    l = p_new
    for p in ps:
        l = l + jnp.sum(p, axis=-1, keepdims=True)
    inv = 1.0 / l
    pd_new = (p_new * inv)[0:DIFF_HEADS] - lam * (p_new * inv)[DIFF_HEADS:nmap]
    pds = []
    for j in range(n_pages):
        pn = ps[j] * inv
        pds.append((pn[0:DIFF_HEADS] - lam * pn[DIFF_HEADS:nmap]).astype(BF16))
    expm = expm_ref[...]
    hmask = hmask_ref[...]
    acc = pd_new * vn_ref[0]
    for j in range(n_pages):
        spread = (_dot(pds[j], expm) * hmask).astype(BF16)
        acc = acc + _dot(spread, vp[j][0].astype(BF16))
    o_ref[0] = _rms_rows(acc, sub_ref[...]) * (1.0 - lam_init)


def diff_sample_attn(page_table, pool_k, pool_v, base, q, k_new, v_new, lam_vec, sub_row, lam_init):
    b, n_pages = page_table.shape
    vw = 2 * HEAD_DIM
    rows = PAGE_SIZE * DIFF_HEADS
    expm = (jnp.arange(PAGE_SIZE)[:, None] == jnp.arange(rows)[None, :] // DIFF_HEADS).astype(BF16)
    hmask = (jnp.arange(DIFF_HEADS)[:, None] == jnp.arange(rows)[None, :] % DIFF_HEADS).astype(F32)
    kpage = lambda j: pl.BlockSpec((1, D_MODEL, PAGE_SIZE), functools.partial(_page_map, j=j, base=base))
    vpage = lambda j: pl.BlockSpec((1, rows, vw), functools.partial(_page_map, j=j, base=base))
    per_b = pl.BlockSpec((1, 1, D_MODEL), lambda i, pt: (i, 0, 0))
    per_bh = pl.BlockSpec((1, DIFF_HEADS, vw), lambda i, pt: (i, 0, 0))
    const = lambda a: pl.BlockSpec(a.shape, lambda i, pt: (0, 0))
    return pl.pallas_call(
        functools.partial(_diff_sample_kernel, n_pages=n_pages, lam_init=lam_init),
        grid_spec=pltpu.PrefetchScalarGridSpec(
            num_scalar_prefetch=1,
            grid=(b,),
            in_specs=([kpage(j) for j in range(n_pages)] + [vpage(j) for j in range(n_pages)]
                      + [per_b, per_b, per_bh, const(lam_vec), const(sub_row), const(expm), const(hmask)]),
            out_specs=per_bh),
        out_shape=jax.ShapeDtypeStruct((b, DIFF_HEADS, vw), F32),
        compiler_params=_params("parallel"),
        name="diff_sample_attn",
    )(page_table, *([pool_k] * n_pages), *([pool_v] * n_pages), q, k_new, v_new, lam_vec, sub_row, expm, hmask)


def _shift_rows(x, s, carry, row):
    r = pltpu.roll(x, s, 0)
    for i in range(s):
        r = jnp.where(row == i, carry[SUBLANE - s + i:SUBLANE - s + i + 1, :], r)
    return r


def _conv_silu_seq_kernel(x_ref, w_ref, b_ref, o_ref, carry_ref, *, width):
    @pl.when(pl.program_id(2) == 0)
    def _():
        carry_ref[...] = jnp.zeros(carry_ref.shape, F32)

    x = x_ref[...]
    tm = x.shape[0]
    row = lax.broadcasted_iota(jnp.int32, (tm, 1), 0)
    carry = carry_ref[...]
    acc = x * w_ref[width - 1:width, :]
    for s in range(1, width):
        acc = acc + _shift_rows(x, s, carry, row) * w_ref[width - 1 - s:width - s, :]
    acc = acc + b_ref[...]
    o_ref[...] = _silu(acc)
    carry_ref[...] = x[tm - SUBLANE:tm, :]


def ssd_conv_prompt(proj, conv_w, conv_b, b, t, tm):
    cb = 1024
    nt = t // tm
    c0 = SSD_D_INNER // cb
    return pl.pallas_call(
        functools.partial(_conv_silu_seq_kernel, width=SSD_CONV_W),
        grid=(b, SSD_CONV_DIM // cb, nt),
        in_specs=[pl.BlockSpec((tm, cb), lambda bi, j, ti: (bi * nt + ti, c0 + j)),
                  pl.BlockSpec((SSD_CONV_W, cb), lambda bi, j, ti: (0, j)),
                  pl.BlockSpec((1, cb), lambda bi, j, ti: (0, j))],
        out_specs=pl.BlockSpec((tm, cb), lambda bi, j, ti: (bi * nt + ti, j)),
        out_shape=jax.ShapeDtypeStruct((b * t, SSD_CONV_DIM), F32),
        scratch_shapes=[pltpu.VMEM((SUBLANE, cb), F32)],
        compiler_params=_params("parallel", "parallel", "arbitrary"),
        name="ssd_conv_prompt",
    )(proj, conv_w, conv_b.reshape(1, SSD_CONV_DIM))


def _conv_silu_state_kernel(x_ref, p0_ref, p1_ref, p2_ref, w_ref, b_ref, o_ref):
    acc = (p0_ref[...] * w_ref[0:1, :] + p1_ref[...] * w_ref[1:2, :] + p2_ref[...] * w_ref[2:3, :]
           + x_ref[...] * w_ref[3:4, :] + b_ref[...])
    o_ref[...] = _silu(acc)


def ssd_conv_sample(proj, prev, conv_w, conv_b):
    b = proj.shape[0]
    cb = 1024
    c0 = SSD_D_INNER // cb
    col = lambda j: (0, j)
    return pl.pallas_call(
        _conv_silu_state_kernel,
        grid=(SSD_CONV_DIM // cb,),
        in_specs=[pl.BlockSpec((b, cb), lambda j: (0, c0 + j)),
                  pl.BlockSpec((b, cb), col), pl.BlockSpec((b, cb), col), pl.BlockSpec((b, cb), col),
                  pl.BlockSpec((SSD_CONV_W, cb), col), pl.BlockSpec((1, cb), col)],
        out_specs=pl.BlockSpec((b, cb), col),
        out_shape=jax.ShapeDtypeStruct((b, SSD_CONV_DIM), F32),
        compiler_params=_params("parallel"),
        name="ssd_conv_sample",
    )(proj, prev[0], prev[1], prev[2], conv_w, conv_b.reshape(1, SSD_CONV_DIM))


def _ssd_scan_kernel(xbc_ref, dt_ref, bias_ref, alog_ref, dskip_ref, tril_ref, exp_ref,
                     y_ref, st_ref, state_ref):
    @pl.when(pl.program_id(1) == 0)
    def _():
        state_ref[...] = jnp.zeros(state_ref.shape, F32)

    l = SSD_CHUNK
    di = SSD_D_INNER
    n = SSD_D_STATE
    tril = tril_ref[...]
    expm = exp_ref[...]
    dt = _softplus(dt_ref[...] + bias_ref[...])
    a = dt * (-jnp.exp(alog_ref[...]))
    acs = _lhs_exact_dot(tril, a)
    acs_t = acs.T
    dtx = _dot_exact_rhs(dt, expm)
    eacs = jnp.exp(acs)
    eacsx = _dot_exact_rhs(eacs, expm)
    decx = _dot_exact_rhs(jnp.exp(acs[l - 1:l, :] - acs), expm)
    x = xbc_ref[:, 0:di]
    xdt = x * dtx
    xw = (xdt * decx).astype(BF16)
    xdt_b = xdt.astype(BF16)
    ri = lax.broadcasted_iota(jnp.int32, (l, l), 0)
    ci = lax.broadcasted_iota(jnp.int32, (l, l), 1)
    lower = ri >= ci
    gw = SSD_GROUP_W
    for g in range(SSD_GROUPS):
        bg = xbc_ref[:, di + g * n:di + (g + 1) * n]
        cg = xbc_ref[:, di + SSD_GROUPS * n + g * n:di + SSD_GROUPS * n + (g + 1) * n].astype(BF16)
        cb = _dot_nt(cg, bg.astype(BF16))
        st_g = state_ref[:, g * gw:(g + 1) * gw]
        y_off = _dot(cg, st_g.astype(BF16)) * eacsx[:, g * gw:(g + 1) * gw]
        state_ref[:, g * gw:(g + 1) * gw] = (
            st_g * eacsx[l - 1:l, g * gw:(g + 1) * gw] + _dot(bg.T.astype(BF16), xw[:, g * gw:(g + 1) * gw]))
        for hh in range(SSD_HEADS_PER_GROUP):
            h = g * SSD_HEADS_PER_GROUP + hh
            cols = slice(h * SSD_HEADDIM, (h + 1) * SSD_HEADDIM)
            seg = acs[:, h:h + 1] - acs_t[h:h + 1, :]
            lmat = jnp.exp(jnp.where(lower, seg, NEG_INF))
            yd = _dot((cb * lmat).astype(BF16), xdt_b[:, cols])
            y_ref[:, cols] = (yd + y_off[:, hh * SSD_HEADDIM:(hh + 1) * SSD_HEADDIM]
                              + dskip_ref[:, cols] * x[:, cols])
    st_ref[0] = state_ref[...]


def ssd_scan_prompt(xbc, proj, dt_bias_row, a_log_row, dskip_row, b, t):
    nc = t // SSD_CHUNK
    l = SSD_CHUNK
    tril = (jnp.arange(l)[:, None] >= jnp.arange(l)[None, :]).astype(BF16)
    expm = (jnp.arange(LANE)[:, None] == jnp.arange(SSD_D_INNER)[None, :] // SSD_HEADDIM).astype(BF16)
    dt_blk = (SSD_D_INNER + SSD_CONV_DIM) // LANE
    tok = lambda bi, ci: (bi * nc + ci, 0)
    const = lambda a: pl.BlockSpec(a.shape, lambda bi, ci: (0, 0))
    return pl.pallas_call(
        _ssd_scan_kernel,
        grid=(b, nc),
        in_specs=[pl.BlockSpec((l, SSD_CONV_DIM), tok),
                  pl.BlockSpec((l, LANE), lambda bi, ci: (bi * nc + ci, dt_blk)),
                  const(dt_bias_row), const(a_log_row), const(dskip_row), const(tril), const(expm)],
        out_specs=[pl.BlockSpec((l, SSD_D_INNER), tok),
                   pl.BlockSpec((1, SSD_D_STATE, SSD_D_INNER), lambda bi, ci: (bi, 0, 0))],
        out_shape=[jax.ShapeDtypeStruct((b * t, SSD_D_INNER), F32),
                   jax.ShapeDtypeStruct((b, SSD_D_STATE, SSD_D_INNER), F32)],
        scratch_shapes=[pltpu.VMEM((SSD_D_STATE, SSD_D_INNER), F32)],
        compiler_params=_params("parallel", "arbitrary"),
        name="ssd_scan_prompt",
    )(xbc, proj, dt_bias_row, a_log_row, dskip_row, tril, expm)


def _ssd_step_kernel(xbc_ref, dt_ref, bias_ref, alog_ref, dskip_ref, s_ref, y_ref, so_ref):
    di = SSD_D_INNER
    n = SSD_D_STATE
    p = SSD_HEADDIM
    dt = _softplus(dt_ref[0] + bias_ref[...])
    dec = jnp.exp(dt * (-jnp.exp(alog_ref[...])))
    ri = lax.broadcasted_iota(jnp.int32, (p, p), 0)
    ci = lax.broadcasted_iota(jnp.int32, (p, p), 1)
    eye = ri == ci
    for g in range(SSD_GROUPS):
        bmat = jnp.broadcast_to(xbc_ref[0, :, di + g * n:di + (g + 1) * n], (p, n)).astype(BF16)
        c8 = jnp.broadcast_to(xbc_ref[0, :, di + SSD_GROUPS * n + g * n:di + SSD_GROUPS * n + (g + 1) * n],
                              (SUBLANE, n)).astype(BF16)
        for hh in range(SSD_HEADS_PER_GROUP):
            h = g * SSD_HEADS_PER_GROUP + hh
            cols = slice(h * p, (h + 1) * p)
            xh = xbc_ref[0, :, cols]
            xdt = xh * dt[:, h:h + 1]
            diag = jnp.where(eye, jnp.broadcast_to(xdt, (p, p)), 0.0)
            hi_, lo_ = _split2(diag)
            s_new = s_ref[0, h] * dec[:, h:h + 1] + _dot(hi_, bmat) + _dot(lo_, bmat)
            so_ref[0, h] = s_new
            yh = _dot_nt(c8, s_new.astype(BF16))
            y_ref[0, :, cols] = yh[0:1] + dskip_ref[:, cols] * xh


def ssd_step_sample(xbc, proj, dt_bias_row, a_log_row, dskip_row, state, state_base):
    b = xbc.shape[0]
    dt_blk = (SSD_D_INNER + SSD_CONV_DIM) // LANE
    const = lambda a: pl.BlockSpec(a.shape, lambda i: (0, 0))
    st_shape = (1, SSD_HEADS, SSD_HEADDIM, SSD_D_STATE)
    return pl.pallas_call(
        _ssd_step_kernel,
        grid=(b,),
        in_specs=[pl.BlockSpec((1, 1, SSD_CONV_DIM), lambda i: (i, 0, 0)),
                  pl.BlockSpec((1, 1, LANE), lambda i: (i, 0, dt_blk)),
                  const(dt_bias_row), const(a_log_row), const(dskip_row),
                  pl.BlockSpec(st_shape, lambda i: (state_base + i, 0, 0, 0))],
        out_specs=[pl.BlockSpec((1, 1, SSD_D_INNER), lambda i: (i, 0, 0)),
                   pl.BlockSpec(st_shape, lambda i: (i, 0, 0, 0))],
        out_shape=[jax.ShapeDtypeStruct((b, 1, SSD_D_INNER), F32),
                   jax.ShapeDtypeStruct((b,) + st_shape[1:], F32)],
        compiler_params=_params("parallel"),
        name="ssd_step_sample",
    )(xbc.reshape(b, 1, SSD_CONV_DIM), proj.reshape(b, 1, proj.shape[1]),
      dt_bias_row, a_log_row, dskip_row, state)


def _ssd_out_kernel(y_ref, z_ref, ng_ref, w_ref, r_ref, o_ref):
    gated = y_ref[...] * _silu(z_ref[...])
    parts = []
    for g in range(SSD_GROUPS):
        cols = slice(g * SSD_GROUP_W, (g + 1) * SSD_GROUP_W)
        parts.append(_rms_rows(gated[:, cols], ng_ref[:, cols]).astype(BF16))
    o_ref[...] = r_ref[...] + _dot(jnp.concatenate(parts, axis=1), w_ref[...])


def ssd_out(y, proj, norm_g, w, res, tm):
    m = y.shape[0]
    di = SSD_D_INNER
    return pl.pallas_call(
        _ssd_out_kernel,
        grid=(m // tm,),
        in_specs=[pl.BlockSpec((tm, di), lambda i: (i, 0)),
                  pl.BlockSpec((tm, di), lambda i: (i, 0)),
                  pl.BlockSpec((1, di), lambda i: (0, 0)),
                  pl.BlockSpec((di, D_MODEL), lambda i: (0, 0)),
                  pl.BlockSpec((tm, D_MODEL), lambda i: (i, 0))],
        out_specs=pl.BlockSpec((tm, D_MODEL), lambda i: (i, 0)),
        out_shape=jax.ShapeDtypeStruct((m, D_MODEL), F32),
        compiler_params=_params("parallel"),
        name="ssd_out",
    )(y, proj, norm_g.reshape(1, di), w, res)


def _ffn_seq_kernel(x_ref, g_ref, wup_ref, cw_ref, wdn_ref, o_ref, st_ref, carry_ref):
    @pl.when(pl.program_id(1) == 0)
    def _():
        carry_ref[...] = jnp.zeros(carry_ref.shape, F32)

    x = x_ref[...]
    tm = x.shape[0]
    h = _rms_rows(x, g_ref[...]).astype(BF16)
    row = lax.broadcasted_iota(jnp.int32, (tm, 1), 0)
    fc = FFN_CHUNK

    def conv(u, c0):
        w = cw_ref[:, c0:c0 + fc]
        carry = carry_ref[:, c0:c0 + fc]
        y = u * w[2:3] + _shift_rows(u, 1, carry, row) * w[1:2] + _shift_rows(u, 2, carry, row) * w[0:1]
        carry_ref[:, c0:c0 + fc] = u[tm - SUBLANE:tm, :]
        return y

    acc = jnp.zeros((tm, D_MODEL), F32)
    for c0 in range(0, D_FF, fc):
        u = conv(_dot(h, wup_ref[:, c0:c0 + fc]), c0)
        gate = conv(_dot(h, wup_ref[:, D_FF + c0:D_FF + c0 + fc]), D_FF + c0)
        acc = acc + _dot((_silu(gate) * u).astype(BF16), wdn_ref[c0:c0 + fc, :])
    o_ref[...] = x + acc
    st_ref[0] = carry_ref[...]


def ffn_prompt(x, g, w_up, conv_w, w_down, b, t, tm):
    nt = t // tm
    tok = lambda bi, ti: (bi * nt + ti, 0)
    const = lambda a: pl.BlockSpec(a.shape, lambda bi, ti: (0, 0))
    g = g.reshape(1, D_MODEL)
    return pl.pallas_call(
        _ffn_seq_kernel,
        grid=(b, nt),
        in_specs=[pl.BlockSpec((tm, D_MODEL), tok), const(g), const(w_up), const(conv_w), const(w_down)],
        out_specs=[pl.BlockSpec((tm, D_MODEL), tok),
                   pl.BlockSpec((1, SUBLANE, 2 * D_FF), lambda bi, ti: (bi, 0, 0))],
        out_shape=[jax.ShapeDtypeStruct((b * t, D_MODEL), F32),
                   jax.ShapeDtypeStruct((b, SUBLANE, 2 * D_FF), F32)],
        scratch_shapes=[pltpu.VMEM((SUBLANE, 2 * D_FF), F32)],
        compiler_params=_params("parallel", "arbitrary"),
        name="ffn_prompt",
    )(x, g, w_up, conv_w, w_down)


def _ffn_state_kernel(x_ref, g_ref, wup_ref, cw_ref, wdn_ref, p0_ref, p1_ref, o_ref, up_ref):
    x = x_ref[...]
    h = _rms_rows(x, g_ref[...]).astype(BF16)
    fc = FFN_CHUNK

    def conv(u, c0):
        up_ref[:, c0:c0 + fc] = u
        w = cw_ref[:, c0:c0 + fc]
        return u * w[2:3] + p1_ref[:, c0:c0 + fc] * w[1:2] + p0_ref[:, c0:c0 + fc] * w[0:1]

    acc = jnp.zeros(x.shape, F32)
    for c0 in range(0, D_FF, fc):
        u = conv(_dot(h, wup_ref[:, c0:c0 + fc]), c0)
        gate = conv(_dot(h, wup_ref[:, D_FF + c0:D_FF + c0 + fc]), D_FF + c0)
        acc = acc + _dot((_silu(gate) * u).astype(BF16), wdn_ref[c0:c0 + fc, :])
    o_ref[...] = x + acc


def ffn_sample(x, g, w_up, conv_w, w_down, prev):
    b = x.shape[0]
    full = lambda a: pl.BlockSpec(a.shape, lambda i: (0,) * a.ndim)
    g = g.reshape(1, D_MODEL)
    p0, p1 = prev[:, 0], prev[:, 1]
    return pl.pallas_call(
        _ffn_state_kernel,
        grid=(1,),
        in_specs=[full(x), full(g), full(w_up), full(conv_w), full(w_down), full(p0), full(p1)],
        out_specs=[pl.BlockSpec((b, D_MODEL), lambda i: (0, 0)),
                   pl.BlockSpec((b, 2 * D_FF), lambda i: (0, 0))],
        out_shape=[jax.ShapeDtypeStruct((b, D_MODEL), F32),
                   jax.ShapeDtypeStruct((b, 2 * D_FF), F32)],
        compiler_params=_params("arbitrary"),
        name="ffn_sample",
    )(x, g, w_up, conv_w, w_down, p0, p1)


def _rope_tables(pos, rows):
    half = HEAD_DIM // 2
    inv_freq = ROPE_THETA ** (-jnp.arange(half, dtype=F32) / half)
    ang = pos.astype(F32)[:, None] * inv_freq[None, :]
    cos = jnp.cos(ang)
    sin = jnp.sin(ang)
    cos = jnp.tile(jnp.concatenate([cos, cos], axis=-1), (1, 4))
    sin = jnp.tile(jnp.concatenate([-sin, sin], axis=-1), (1, 4))
    if cos.shape[0] != rows:
        cos = jnp.broadcast_to(cos, (rows, 256))
        sin = jnp.broadcast_to(sin, (rows, 256))
    return cos, sin


def _block_diag_ones():
    i = jnp.arange(256)
    return (i[:, None] // HEAD_DIM == i[None, :] // HEAD_DIM).astype(BF16)


def _pad_cols(w, n):
    return jnp.pad(w, ((0, 0), (0, n - w.shape[1])))


def _nsa_weights(w_in, q_norm, k_norm, cmp_pos, cmp_w1, cmp_w2, w_out):
    gain = jnp.concatenate(
        [jnp.tile(q_norm, NSA_HEADS)]
        + [jnp.tile(k_norm[br], NSA_KV_HEADS) if kv == 0 else jnp.ones((NSA_KV,), F32)
           for br in range(3) for kv in range(2)]).reshape(1, NSA_QKV)
    pos_rows = [jnp.broadcast_to(cmp_pos[i][:, None, :], (CMP_BLOCK, NSA_KV_HEADS, HEAD_DIM))
                .reshape(1, CMP_BLOCK * NSA_KV) for i in range(2)]
    pos_blk = [cmp_pos[i].reshape(1, CMP_BLOCK * HEAD_DIM) for i in range(2)]
    return dict(w_in=_pad_cols(w_in, NSA_IN_PAD).astype(BF16), gain=gain, pos=pos_rows, pos_blk=pos_blk,
                w1=[cmp_w1[i].astype(BF16) for i in range(2)],
                w2=[cmp_w2[i].astype(BF16) for i in range(2)],
                w_out=w_out.astype(BF16))


def _kv_rows_view(a):
    b, _, t = a.shape
    return a.reshape(b, NSA_KV_HEADS, HEAD_DIM, t).transpose(0, 3, 1, 2)


def _nsa_prompt_layer(x, gmix, w, tabs, bd, b, t, tm):
    assert t % CMP_BLOCK == 0
    proj = norm_matmul(x, gmix, w["w_in"], tm)
    q, kc_rows, vc_rows, ks, kw, gates, kct, vct, kst, vst, kwt, vwt = post_proj(
        proj, w["gain"], tabs[0], tabs[1], bd, NSA_PLAN_P, NSA_OUTS_P, tm, b)
    n_cmp = t // CMP_BLOCK
    wide = CMP_BLOCK * NSA_KV
    kc, _ = compress_rows(kc_rows.reshape(b, n_cmp, wide), w["pos"][0], w["w1"][0], w["w2"][0], n_cmp)
    _, vc_t = compress_rows(vc_rows.reshape(b, n_cmp, wide), w["pos"][1], w["w1"][1], w["w2"][1], n_cmp)
    o = nsa_prompt_attn(q, gates, kc, vc_t, ks, vst, kw, vwt, b, t)
    x = matmul_res(o, w["w_out"], x, tm)
    keep = min(WINDOW, t)
    rows = tuple(_kv_rows_view(a) for a in (kct, vct, kst, vst))
    wins = tuple(_kv_rows_view(a[:, :, t - keep:]) for a in (kwt, vwt))
    return x, rows + wins


def _nsa_sample_layer(x, gmix, w, tabs, bd, j, page_table, caches, win_k, win_v):
    b = x.shape[0]
    n_phys = caches[0].shape[1]
    proj = norm_matmul(x, gmix, w["w_in"], b)
    q, kc_new, vc_new, ks_new, vs_new, kw_new, vw_new, gates, kwt_new, vwt_new = post_proj(
        proj, w["gain"], tabs[0], tabs[1], bd, NSA_PLAN_S, NSA_OUTS_S, b, 1)
    fm = lambda c: c.transpose(0, 1, 3, 4, 2).reshape(-1, NSA_KV, c.shape[2])
    pool_ck, pool_cv, pool_sk, pool_sv = (fm(c) for c in caches)
    kc = compress_pages(pool_ck, page_table, j * n_phys, w["pos_blk"][0], w["w1"][0], w["w2"][0])
    vc = compress_pages(pool_cv, page_table, j * n_phys, w["pos_blk"][1], w["w1"][1], w["w2"][1])
    row3 = lambda a: a.reshape(b, 1, NSA_KV)
    o16, wk_out, wv_out = nsa_sample_attn(
        page_table, pool_sk, pool_sv, j * n_phys,
        q.reshape(b, NSA_HEADS, HEAD_DIM), gates[:, :NSA_HEADS * 3].reshape(b, NSA_HEADS, 3), kc, vc,
        row3(ks_new), row3(vs_new), row3(kw_new), row3(vw_new), kwt_new, vwt_new,
        fm(win_k), fm(win_v), j * b)
    x = matmul_res(o16.reshape(b, NSA_HEADS * HEAD_DIM), w["w_out"], x, b)
    kvshape = (b, 1, NSA_KV_HEADS, HEAD_DIM)
    rows = tuple(a.reshape(kvshape) for a in (kc_new, vc_new, ks_new, vs_new))
    wins = tuple(_kv_rows_view(a) for a in (wk_out, wv_out))
    return x, rows + wins


def _diff_weights(w_in, q_norm, k_norm, lam_vec, sub_norm, w_out):
    nmap = 2 * DIFF_HEADS
    gain = jnp.concatenate([jnp.tile(q_norm, nmap), jnp.tile(k_norm, nmap),
                            jnp.ones((D_MODEL,), F32)]).reshape(1, 3 * D_MODEL)
    return dict(w_in=w_in.astype(BF16), gain=gain, lam=lam_vec, sub=sub_norm, w_out=w_out.astype(BF16))


def kernel(x_prompt, x_sample, cache_nsa_cmp_k, cache_nsa_cmp_v, cache_nsa_slc_k, cache_nsa_slc_v, state_nsa_win_k, state_nsa_win_v, cache_diff_k, cache_diff_v, state_ssd_conv, state_ssd_ssm, state_ffn_conv, page_table, norm_mix, norm_ffn, nsa_w_in, nsa_q_norm, nsa_k_norm, nsa_cmp_pos, nsa_cmp_w1, nsa_cmp_w2, nsa_w_out, diff_w_in, diff_q_norm, diff_k_norm, diff_lambda, diff_sub_norm, diff_w_out, ssd_w_in, ssd_conv_w, ssd_conv_b, ssd_dt_bias, ssd_a_log, ssd_d, ssd_norm, ssd_w_out, ffn_w_up, ffn_conv_w, ffn_w_down):
    bp, tp, _ = x_prompt.shape
    bs, ts, _ = x_sample.shape
    assert ts == 1 and tp % Q_BLOCK == 0
    n_pages = page_table.shape[1]
    past = n_pages * PAGE_SIZE
    depth = norm_mix.shape[0]
    tm_p = min(PROMPT_ROW_TILE, tp)
    xp = x_prompt.reshape(bp * tp, D_MODEL)
    xs = x_sample.reshape(bs, D_MODEL)
    bd = _block_diag_ones()
    tabs_p = _rope_tables(jnp.arange(tp, dtype=jnp.int32), tp)
    tabs_s = _rope_tables(jnp.full((1,), past, jnp.int32), bs)
    nsa_p, nsa_s, diff_p, diff_s, ssd_p, ssd_s, ffn_p, ffn_s = [], [], [], [], [], [], [], []
    for i in range(depth):
        kind = i % N_MIXERS
        j = i // N_MIXERS
        if kind == 0:
            w = _nsa_weights(nsa_w_in[j], nsa_q_norm[j], nsa_k_norm[j], nsa_cmp_pos[j], nsa_cmp_w1[j],
                             nsa_cmp_w2[j], nsa_w_out[j])
            xp, outs = _nsa_prompt_layer(xp, norm_mix[i], w, tabs_p, bd, bp, tp, tm_p)
            nsa_p.append(outs)
            xs, outs = _nsa_sample_layer(xs, norm_mix[i], w, tabs_s, bd, j, page_table,
                                         (cache_nsa_cmp_k, cache_nsa_cmp_v, cache_nsa_slc_k, cache_nsa_slc_v),
                                         state_nsa_win_k, state_nsa_win_v)
            nsa_s.append(outs)
        elif kind == 1:
            lam_init = 0.8 - 0.6 * math.exp(-0.3 * i)
            w = _diff_weights(diff_w_in[j], diff_q_norm[j], diff_k_norm[j], diff_lambda[j], diff_sub_norm[j],
                              diff_w_out[j])
            proj = norm_matmul(xp, norm_mix[i], w["w_in"], tm_p)
            q, k, v, kt, vt = post_proj(proj, w["gain"], tabs_p[0], tabs_p[1], bd, DIFF_PLAN_P, DIFF_OUTS_P,
                                        tm_p, bp)
            o = diff_prompt_attn(q, k, vt, w["lam"], w["sub"], lam_init, bp, tp)
            xp = matmul_res(o, w["w_out"], xp, tm_p)
            diff_p.append((kt.reshape(bp, DIFF_HEADS, 2, HEAD_DIM, tp).transpose(0, 4, 1, 2, 3),
                           v.reshape(bp, tp, DIFF_HEADS, 2 * HEAD_DIM)))
            n_phys = cache_diff_k.shape[1]
            proj = norm_matmul(xs, norm_mix[i], w["w_in"], bs)
            q, k, v = post_proj(proj, w["gain"], tabs_s[0], tabs_s[1], bd, DIFF_PLAN_S, DIFF_OUTS_S, bs, 1)
            r3 = lambda a: a.reshape(bs, 1, D_MODEL)
            pool_k = cache_diff_k.transpose(0, 1, 3, 4, 5, 2).reshape(-1, D_MODEL, PAGE_SIZE)
            pool_v = cache_diff_v.reshape(-1, PAGE_SIZE * DIFF_HEADS, 2 * HEAD_DIM)
            o = diff_sample_attn(page_table, pool_k, pool_v, j * n_phys,
                                 r3(q), r3(k), v.reshape(bs, DIFF_HEADS, 2 * HEAD_DIM), w["lam"],
                                 w["sub"].reshape(1, 2 * HEAD_DIM), lam_init)
            xs = matmul_res(o.reshape(bs, D_MODEL), w["w_out"], xs, bs)
            diff_s.append((k.reshape(bs, 1, DIFF_HEADS, 2, HEAD_DIM), v.reshape(bs, 1, DIFF_HEADS, 2 * HEAD_DIM)))
        else:
            w_in = _pad_cols(ssd_w_in[j], SSD_IN_PAD).astype(BF16)
            w_out = ssd_w_out[j].astype(BF16)
            pad_h = lambda a: jnp.pad(a, (0, LANE - SSD_HEADS)).reshape(1, LANE)
            bias_row, alog_row = pad_h(ssd_dt_bias[j]), pad_h(ssd_a_log[j])
            dskip_row = jnp.repeat(ssd_d[j], SSD_HEADDIM).reshape(1, SSD_D_INNER)
            xbc0 = SSD_D_INNER
            proj = norm_matmul(xp, norm_mix[i], w_in, tm_p)
            xbc = ssd_conv_prompt(proj, ssd_conv_w[j], ssd_conv_b[j], bp, tp, tm_p)
            y, st = ssd_scan_prompt(xbc, proj, bias_row, alog_row, dskip_row, bp, tp)
            xp = ssd_out(y, proj, ssd_norm[j], w_out, xp, tm_p)
            conv_new = proj.reshape(bp, tp, SSD_IN_PAD)[:, tp - (SSD_CONV_W - 1):, xbc0:xbc0 + SSD_CONV_DIM]
            ssm_new = st.reshape(bp, SSD_D_STATE, SSD_HEADS, SSD_HEADDIM).transpose(0, 2, 3, 1)
            ssd_p.append((conv_new, ssm_new))
            proj = norm_matmul(xs, norm_mix[i], w_in, bs)
            prev = state_ssd_conv[j]
            xbc = ssd_conv_sample(proj, prev.transpose(1, 0, 2), ssd_conv_w[j], ssd_conv_b[j])
            y, st = ssd_step_sample(xbc, proj, bias_row, alog_row, dskip_row,
                                    state_ssd_ssm.reshape((-1,) + state_ssd_ssm.shape[2:]), j * bs)
            xs = ssd_out(y.reshape(bs, SSD_D_INNER), proj, ssd_norm[j], w_out, xs, bs)
            conv_new = jnp.concatenate([prev[:, 1:], proj[:, None, xbc0:xbc0 + SSD_CONV_DIM]], axis=1)
            ssd_s.append((conv_new, st))
        w_up = ffn_w_up[i].astype(BF16)
        w_dn = ffn_w_down[i].astype(BF16)
        xp, st = ffn_prompt(xp, norm_ffn[i], w_up, ffn_conv_w[i], w_dn, bp, tp, tm_p)
        ffn_p.append(st[:, SUBLANE - (FFN_CONV_W - 1):])
        prev = state_ffn_conv[i]
        xs, up = ffn_sample(xs, norm_ffn[i], w_up, ffn_conv_w[i], w_dn, prev)
        ffn_s.append(jnp.concatenate([prev[:, 1:], up[:, None]], axis=1))
    outs = [xp.reshape(bp, tp, D_MODEL), xs.reshape(bs, ts, D_MODEL)]
    for r in range(6):
        outs.append(jnp.stack([o[r] for o in nsa_p]))
        outs.append(jnp.stack([o[r] for o in nsa_s]))
    for r in range(2):
        outs.append(jnp.stack([o[r] for o in diff_p]))
        outs.append(jnp.stack([o[r] for o in diff_s]))
    for r in range(2):
        outs.append(jnp.stack([o[r] for o in ssd_p]))
        outs.append(jnp.stack([o[r] for o in ssd_s]))
    outs.append(jnp.stack(ffn_p))
    outs.append(jnp.stack(ffn_s))
    return tuple(outs)
```

```python
import functools
import math

import jax
import jax.numpy as jnp
from jax import lax
from jax.experimental import pallas as pl
from jax.experimental.pallas import tpu as pltpu

F32 = jnp.float32
BF16 = jnp.bfloat16

D_MODEL = 1024
DEPTH = 4
PAGE_SIZE = 128
N_MIXERS = 3
RMS_EPS = 1e-6
ROPE_THETA = 10000.0
Q_BLOCK = 128
NEG_INF = -1e30
HEAD_DIM = 64
SCALE = HEAD_DIM ** -0.5
LOG2E = 1.4426950408889634

NSA_HEADS = D_MODEL // HEAD_DIM
NSA_KV_HEADS = 4
NSA_GROUP = NSA_HEADS // NSA_KV_HEADS
CMP_BLOCK = 32
SEL_BLOCK = 64
N_SEL = 8
WINDOW = 512
FORCED_SCORE = 1e4
NSA_KEY_TILE = 256
NSA_KV = NSA_KV_HEADS * HEAD_DIM
NSA_QKV = NSA_HEADS * HEAD_DIM + 6 * NSA_KV
NSA_IN_PAD = NSA_QKV + 128

DIFF_HEADS = D_MODEL // (2 * HEAD_DIM)
DIFF_Q_BLOCK = 512

SSD_D_INNER = 2 * D_MODEL
SSD_HEADDIM = 64
SSD_HEADS = SSD_D_INNER // SSD_HEADDIM
SSD_GROUPS = 4
SSD_D_STATE = 128
SSD_CONV_W = 4
SSD_CHUNK = 128
SSD_CONV_DIM = SSD_D_INNER + 2 * SSD_GROUPS * SSD_D_STATE
SSD_IN_PAD = SSD_D_INNER + SSD_CONV_DIM + 128
SSD_GROUP_W = SSD_D_INNER // SSD_GROUPS
SSD_HEADS_PER_GROUP = SSD_HEADS // SSD_GROUPS

D_FF = 2816
FFN_CONV_W = 3
FFN_CHUNK = 256

PROMPT_ROW_TILE = 512
LANE = 128
SUBLANE = 8
VMEM_LIMIT = 56 * 1024 * 1024
NSA_SAMPLE_PER_STEP = 2
INVALID = -3.2e38

NT_DIMS = (((1,), (1,)), ((), ()))


def _params(*sem):
    return pltpu.CompilerParams(dimension_semantics=sem, vmem_limit_bytes=VMEM_LIMIT)


def _dot(a, b):
    return jnp.dot(a, b, preferred_element_type=F32)


def _dot_nt(a, b):
    return lax.dot_general(a, b, NT_DIMS, preferred_element_type=F32)


def _split2(x):
    hi = x.astype(BF16)
    lo = (x - hi.astype(F32)).astype(BF16)
    return hi, lo


def _split3(x):
    hi = x.astype(BF16)
    r = x - hi.astype(F32)
    mid = r.astype(BF16)
    lo = (r - mid.astype(F32)).astype(BF16)
    return hi, mid, lo


def _dot_exact_rhs(x, m):
    hi, mid, lo = _split3(x)
    return _dot(hi, m) + _dot(mid, m) + _dot(lo, m)


def _lhs_exact_dot(m, x):
    hi, mid, lo = _split3(x)
    return _dot(m, hi) + _dot(m, mid) + _dot(m, lo)


def _sigmoid(x):
    return 1.0 / (1.0 + jnp.exp(-x))


def _silu(x):
    return x * _sigmoid(x)


def _softplus(x):
    return jnp.maximum(x, 0.0) + jnp.log1p(jnp.exp(-jnp.abs(x)))


def _gelu_tanh(x):
    return 0.5 * x * (1.0 + jnp.tanh(math.sqrt(2.0 / math.pi) * (x + 0.044715 * (x * x * x))))


def _rms_rows(x, g):
    return x * lax.rsqrt(jnp.mean(x * x, axis=-1, keepdims=True) + RMS_EPS) * g


def _norm_matmul_kernel(x_ref, g_ref, w_ref, o_ref, *, chunk):
    h = _rms_rows(x_ref[...], g_ref[...]).astype(BF16)
    n = o_ref.shape[1]
    for c in range(0, n, chunk):
        w = min(chunk, n - c)
        o_ref[:, c:c + w] = _dot(h, w_ref[:, c:c + w])


def norm_matmul(x, g, w, tm):
    m, k = x.shape
    n = w.shape[1]
    return pl.pallas_call(
        functools.partial(_norm_matmul_kernel, chunk=512),
        grid=(m // tm,),
        in_specs=[pl.BlockSpec((tm, k), lambda i: (i, 0)),
                  pl.BlockSpec((1, k), lambda i: (0, 0)),
                  pl.BlockSpec((k, n), lambda i: (0, 0))],
        out_specs=pl.BlockSpec((tm, n), lambda i: (i, 0)),
        out_shape=jax.ShapeDtypeStruct((m, n), F32),
        compiler_params=_params("parallel"),
        name="norm_matmul",
    )(x, g.reshape(1, k), w)


def _matmul_res_kernel(a_ref, w_ref, r_ref, o_ref):
    o_ref[...] = r_ref[...] + _dot(a_ref[...].astype(BF16), w_ref[...])


def matmul_res(a, w, res, tm):
    m, k = a.shape
    n = w.shape[1]
    return pl.pallas_call(
        _matmul_res_kernel,
        grid=(m // tm,),
        in_specs=[pl.BlockSpec((tm, k), lambda i: (i, 0)),
                  pl.BlockSpec((k, n), lambda i: (0, 0)),
                  pl.BlockSpec((tm, n), lambda i: (i, 0))],
        out_specs=pl.BlockSpec((tm, n), lambda i: (i, 0)),
        out_shape=jax.ShapeDtypeStruct((m, n), F32),
        compiler_params=_params("parallel"),
        name="matmul_res",
    )(a, w, res)


def _head_norm_rope(x, gain, cos, sin_signed, bd):
    hi, lo = _split2(x * x)
    ss = _dot(hi, bd) + _dot(lo, bd)
    y = x * lax.rsqrt(ss * (1.0 / HEAD_DIM) + RMS_EPS) * gain
    lane = lax.broadcasted_iota(jnp.int32, y.shape, 1)
    half = HEAD_DIM // 2
    width = y.shape[1]
    partner = jnp.where((lane & half) != 0, pltpu.roll(y, half, 1), pltpu.roll(y, width - half, 1))
    return y * cos + partner * sin_signed


def _post_kernel(x_ref, g_ref, w_ref, gain_ref, cos_ref, sin_ref, bd_ref, *rest, plan):
    out_refs, p_ref = rest[:-1], rest[-1]
    _norm_matmul_kernel(x_ref, g_ref, w_ref, p_ref, chunk=512)
    cos = cos_ref[...]
    sin = sin_ref[...]
    bd = bd_ref[...]
    for mode, src, width, dests in plan:
        x = p_ref[:, src:src + width]
        if mode == "rope":
            y = _head_norm_rope(x, gain_ref[:, src:src + width], cos, sin, bd)
        elif mode == "sigmoid":
            y = _sigmoid(x)
        else:
            y = x
        for oi, oc, transposed in dests:
            if transposed:
                out_refs[oi][0, oc:oc + width, :] = y.T
            else:
                out_refs[oi][:, oc:oc + width] = y


def proj_post(x, g, w, gain_row, cos, sin, bd, plan, outs, tm, b):
    m, k = x.shape
    n = w.shape[1]
    n_tab = cos.shape[0] // tm
    nt = m // b // tm
    out_specs, out_shape = [], []
    for width, transposed in outs:
        if transposed:
            out_specs.append(pl.BlockSpec((1, width, tm), lambda i: (i // nt, 0, i % nt)))
            out_shape.append(jax.ShapeDtypeStruct((b, width, m // b), F32))
        else:
            out_specs.append(pl.BlockSpec((tm, width), lambda i: (i, 0)))
            out_shape.append(jax.ShapeDtypeStruct((m, width), F32))
    return pl.pallas_call(
        functools.partial(_post_kernel, plan=plan),
        grid=(m // tm,),
        in_specs=[pl.BlockSpec((tm, k), lambda i: (i, 0)),
                  pl.BlockSpec((1, k), lambda i: (0, 0)),
                  pl.BlockSpec((k, n), lambda i: (0, 0)),
                  pl.BlockSpec(gain_row.shape, lambda i: (0, 0)),
                  pl.BlockSpec((tm, 256), lambda i: (i % n_tab, 0)),
                  pl.BlockSpec((tm, 256), lambda i: (i % n_tab, 0)),
                  pl.BlockSpec((256, 256), lambda i: (0, 0))],
        out_specs=out_specs,
        out_shape=out_shape,
        scratch_shapes=[pltpu.VMEM((tm, n), F32)],
        compiler_params=_params("parallel"),
        name="proj_post",
    )(x, g.reshape(1, k), w, gain_row, cos, sin, bd)


NSA_PLAN_P = tuple(
    [("rope", c * 256, 256, ((0, c * 256, False),)) for c in range(4)]
    + [("rope", 1024, 256, ((1, 0, False), (6, 0, True))),
       ("copy", 1280, 256, ((2, 0, False), (7, 0, True))),
       ("rope", 1536, 256, ((3, 0, False), (8, 0, True))),
       ("copy", 1792, 256, ((9, 0, True),)),
       ("rope", 2048, 256, ((4, 0, False), (10, 0, True))),
       ("copy", 2304, 256, ((11, 0, True),)),
       ("sigmoid", 2560, 128, ((5, 0, False),))])
NSA_OUTS_P = ((1024, False), (256, False), (256, False), (256, False), (256, False), (128, False)) + ((256, True),) * 6
NSA_PLAN_S = tuple(
    [("rope", c * 256, 256, ((0, c * 256, False),)) for c in range(4)]
    + [("rope", 1024, 256, ((1, 0, False),)), ("copy", 1280, 256, ((2, 0, False),)),
       ("rope", 1536, 256, ((3, 0, False),)), ("copy", 1792, 256, ((4, 0, False),)),
       ("rope", 2048, 256, ((5, 0, False), (8, 0, True))),
       ("copy", 2304, 256, ((6, 0, False), (9, 0, True))),
       ("sigmoid", 2560, 128, ((7, 0, False),))])
NSA_OUTS_S = ((1024, False),) + ((256, False),) * 6 + ((128, False), (256, True), (256, True))

DIFF_PLAN_P = tuple(
    [("rope", c * 256, 256, ((0, c * 256, False),)) for c in range(4)]
    + [("rope", 1024 + c * 256, 256, ((1, c * 256, False), (3, c * 256, True))) for c in range(4)]
    + [("copy", 2048 + c * 256, 256, ((2, c * 256, False), (4, c * 256, True))) for c in range(4)])
DIFF_OUTS_P = ((1024, False),) * 3 + ((1024, True),) * 2
DIFF_PLAN_S = tuple(
    [("rope", c * 256, 256, ((0, c * 256, False),)) for c in range(4)]
    + [("rope", 1024 + c * 256, 256, ((1, c * 256, False),)) for c in range(4)]
    + [("copy", 2048 + c * 256, 256, ((2, c * 256, False),)) for c in range(4)])
DIFF_OUTS_S = ((1024, False),) * 3


def _block_rows(piece):
    mats = [jnp.concatenate([piece(t, g) for t in range(CMP_BLOCK)], axis=1) for g in range(NSA_KV_HEADS)]
    return jnp.concatenate(mats, axis=0)


def _compress_rows_kernel(x_ref, pos_ref, w1_ref, w2_ref, w1t_ref, w2t_ref, o_ref, ot_ref):
    xr = x_ref[0] + pos_ref[...]
    xg = _block_rows(lambda t, g: xr[:, t * NSA_KV + g * HEAD_DIM:t * NSA_KV + (g + 1) * HEAD_DIM]).astype(BF16)
    h = _gelu_tanh(_dot(xg, w1_ref[...]))
    o_ref[0] = _dot(h.astype(BF16), w2_ref[...])
    ht = _gelu_tanh(_dot_nt(w1t_ref[...], xg))
    ot_ref[0] = _dot(w2t_ref[...], ht.astype(BF16))


def compress_rows(rows, pos_row, w1, w2, n_blocks):
    b = rows.shape[0]
    wide = CMP_BLOCK * NSA_KV
    nr = NSA_KV_HEADS * n_blocks
    w1t, w2t = w1.T, w2.T
    const = lambda a: pl.BlockSpec(a.shape, lambda i: (0, 0))
    return pl.pallas_call(
        _compress_rows_kernel,
        grid=(b,),
        in_specs=[pl.BlockSpec((1, n_blocks, wide), lambda i: (i, 0, 0)),
                  const(pos_row), const(w1), const(w2), const(w1t), const(w2t)],
        out_specs=[pl.BlockSpec((1, nr, HEAD_DIM), lambda i: (i, 0, 0)),
                   pl.BlockSpec((1, HEAD_DIM, nr), lambda i: (i, 0, 0))],
        out_shape=[jax.ShapeDtypeStruct((b, nr, HEAD_DIM), F32),
                   jax.ShapeDtypeStruct((b, HEAD_DIM, nr), F32)],
        compiler_params=_params("parallel"),
        name="compress_rows",
    )(rows, pos_row, w1, w2, w1t, w2t)


def _page_map(bi, pt, *, j, base):
    return (base + pt[bi, j], 0, 0)


def _page_map_u(i, pt, *, j, u, per_step, base):
    return (base + pt[i * per_step + u, j], 0, 0)


def _compress_pages_kernel(pt_ref, *refs, n_pages):
    del pt_ref
    x_refs = refs[:n_pages]
    pos_ref, w1_ref, w2_ref, perm_ref, o_ref, tok_ref = refs[n_pages:]
    halves = NSA_KV // LANE
    per_half = LANE // HEAD_DIM
    per_page = PAGE_SIZE // CMP_BLOCK
    perm = perm_ref[...]
    for jp in range(n_pages // 2):
        pair = jnp.concatenate([x_refs[2 * jp][0], x_refs[2 * jp + 1][0]], axis=1).astype(BF16)
        xt = _dot_nt(perm, pair).reshape(CMP_BLOCK, 2 * per_page, NSA_KV)
        for c in range(halves):
            tok_ref[c, :, jp * 2 * per_page:(jp + 1) * 2 * per_page, :] = xt[:, :, c * LANE:(c + 1) * LANE]
    ys = [[tok_ref[c, t] for c in range(halves)] for t in range(CMP_BLOCK)]
    xg = _block_rows(lambda t, g: ys[t][g // per_half][:, (g % per_half) * HEAD_DIM:(g % per_half + 1) * HEAD_DIM])
    xg = xg + pos_ref[...]
    h = _gelu_tanh(_dot(xg.astype(BF16), w1_ref[...]))
    o_ref[0] = _dot(h.astype(BF16), w2_ref[...])


def compress_pages(pool, page_table, base, pos_row, w1, w2):
    b, n_pages = page_table.shape
    assert n_pages % 2 == 0
    per_page = PAGE_SIZE // CMP_BLOCK
    n_blocks = n_pages * per_page
    r = jnp.arange(2 * PAGE_SIZE)
    src = ((r % (2 * per_page)) // per_page) * PAGE_SIZE + (r % per_page) * CMP_BLOCK + r // (2 * per_page)
    perm = (src[:, None] == jnp.arange(2 * PAGE_SIZE)[None, :]).astype(BF16)
    page_specs = [pl.BlockSpec((1, NSA_KV, PAGE_SIZE), functools.partial(_page_map, j=j, base=base))
                  for j in range(n_pages)]
    const = lambda a: pl.BlockSpec(a.shape, lambda i, pt: (0, 0))
    return pl.pallas_call(
        functools.partial(_compress_pages_kernel, n_pages=n_pages),
        grid_spec=pltpu.PrefetchScalarGridSpec(
            num_scalar_prefetch=1,
            grid=(b,),
            in_specs=page_specs + [const(pos_row), const(w1), const(w2), const(perm)],
            out_specs=pl.BlockSpec((1, NSA_KV_HEADS * n_blocks, HEAD_DIM), lambda i, pt: (i, 0, 0)),
            scratch_shapes=[pltpu.VMEM((NSA_KV // LANE, CMP_BLOCK, n_blocks, LANE), F32)]),
        out_shape=jax.ShapeDtypeStruct((b, NSA_KV_HEADS * n_blocks, HEAD_DIM), F32),
        compiler_params=_params("parallel"),
        name="compress_pages",
    )(page_table, *([pool] * n_pages), pos_row, w1, w2, perm)


def _select_blocks(imp_b, cur, n_sb, k_sel):
    lane = lax.broadcasted_iota(jnp.int32, imp_b.shape, 1)
    forced = (lane == cur) | (lane == 0)
    score = jnp.where(forced, FORCED_SCORE, imp_b)
    score = jnp.where(lane > cur, NEG_INF, score)
    score = jnp.where(lane >= n_sb, INVALID, score)
    rank = jnp.zeros(imp_b.shape, F32)
    for j in range(n_sb):
        rival = score[:, j:j + 1]
        rank = rank + jnp.where((rival > score) | ((rival == score) & (lane > j)), 1.0, 0.0)
    return jnp.where((rank < k_sel) & (lane < n_sb), 1.0, 0.0)


def _select_blocks_t(imp_b, cur, n_sb, k_sel):
    blk = lax.broadcasted_iota(jnp.int32, imp_b.shape, 0)
    forced = (blk == cur) | (blk == 0)
    score = jnp.where(forced, FORCED_SCORE, imp_b)
    score = jnp.where(blk > cur, NEG_INF, score)
    score = jnp.where(blk >= n_sb, INVALID, score)
    rank = jnp.zeros(imp_b.shape, F32)
    for j in range(n_sb):
        rival = score[j:j + 1, :]
        rank = rank + jnp.where((rival > score) | ((rival == score) & (blk > j)), 1.0, 0.0)
    return jnp.where((rank < k_sel) & (blk < n_sb), 1.0, 0.0)


def _masked_softmax(s, vis, axis, exp_fn=jnp.exp):
    sm = jnp.where(vis, s, NEG_INF)
    m = jnp.max(sm, axis=axis, keepdims=True)
    e = jnp.where(vis, exp_fn(sm - m), 0.0)
    den = jnp.sum(e, axis=axis, keepdims=True)
    return e / jnp.where(den > 0.0, den, 1.0)


def _masked_softmax_rows(s, vis):
    return _masked_softmax(s, vis, -1)


def _block_diag(blocks):
    n = len(blocks)
    r, c = blocks[0].shape
    rows = []
    for i, blk in enumerate(blocks):
        parts = ([jnp.zeros((r, i * c), blk.dtype)] if i else []) + [blk]
        parts += [jnp.zeros((r, (n - 1 - i) * c), blk.dtype)] if i < n - 1 else []
        rows.append(jnp.concatenate(parts, axis=1))
    return jnp.concatenate(rows, axis=0)


def _flash_init(dv, nq):
    return (jnp.full((1, nq), NEG_INF, F32), jnp.zeros((1, nq), F32), jnp.zeros((dv, nq), F32))


def _flash_step(k, vt, qt_bd, carry, bias, n_blocks):
    m, l, acc = carry
    s = _dot(k, qt_bd)
    if bias is not None:
        s = s + bias
    m_new = jnp.maximum(m, jnp.max(s, axis=0, keepdims=True))
    alpha = jnp.exp2(m - m_new)
    p = jnp.exp2(s - m_new)
    l = alpha * l + jnp.sum(p, axis=0, keepdims=True)
    pv = _dot(vt, p.astype(BF16))
    if n_blocks > 1:
        dv = pv.shape[0] // n_blocks
        w = pv.shape[1] // n_blocks
        pv = jnp.concatenate([pv[g * dv:(g + 1) * dv, g * w:(g + 1) * w] for g in range(n_blocks)], axis=1)
    return m_new, l, alpha * acc + pv


def _nsa_prompt_kernel(q_ref, gt_ref, kc_ref, vct_ref, ks_ref, vst_ref, kw_ref, vwt_ref,
                       poolt_ref, expandt_ref, o_ref, selb_ref, winb_ref, *, n_cmp, n_sb, k_sel):
    qi = pl.program_id(1)
    qb = Q_BLOCK
    rep = NSA_GROUP
    t = winb_ref.shape[0]
    qt = (q_ref[...] * (SCALE * LOG2E)).T.astype(BF16)
    gt = gt_ref[...].T
    pq = qi * qb + lax.broadcasted_iota(jnp.int32, (1, qb), 1)
    pq_rep = jnp.concatenate([pq] * rep, axis=1)
    kts = min(NSA_KEY_TILE, t)
    hi_kt = (qi * qb + qb + kts - 1) // kts
    lo_win = jnp.maximum(qi * qb - WINDOW, 0) // kts
    dist = pq - lax.broadcasted_iota(jnp.int32, (t, qb), 0)
    causal = dist >= 0
    winb_ref[...] = jnp.where(causal & (dist < WINDOW), 0.0, NEG_INF)
    poolt = poolt_ref[...]
    expandt = expandt_ref[...]
    qgs, o_cs = [], []
    for g in range(NSA_KV_HEADS):
        qg = jnp.concatenate(
            [qt[(rep * g + r) * HEAD_DIM:(rep * g + r + 1) * HEAD_DIM, :] for r in range(rep)], axis=1)

        kc = kc_ref[0, g * n_cmp:(g + 1) * n_cmp, :].astype(BF16)
        vct = vct_ref[0, :, g * n_cmp:(g + 1) * n_cmp].astype(BF16)
        s_c = _dot(kc, qg)
        nrow = lax.broadcasted_iota(jnp.int32, s_c.shape, 0)
        vis = ((nrow + 1) * CMP_BLOCK - 1) <= pq_rep
        p_c = _masked_softmax(s_c, vis, 0, jnp.exp2)
        o_cs.append(_dot(vct, p_c.astype(BF16)))
        imp = p_c[:, 0:qb]
        for r in range(1, rep):
            imp = imp + p_c[:, r * qb:(r + 1) * qb]
        hi_, lo_ = _split2(imp)
        imp_b = _dot(poolt, hi_) + _dot(poolt, lo_)
        sel = _select_blocks_t(imp_b, pq // SEL_BLOCK, n_sb, k_sel)
        picked = _dot(expandt, sel.astype(BF16)) > 0.5
        selb_ref[g] = jnp.where(picked & causal, 0.0, NEG_INF)
        qgs.append(qg)

    qt_bd = _block_diag(qgs)
    ng = NSA_KV_HEADS
    nq = ng * rep * qb

    def step(k_ref, vt_ref, bias_fn, kt, carry):
        off = pl.multiple_of(kt * kts, kts)
        k = k_ref[pl.ds(off, kts), :].astype(BF16)
        vt = vt_ref[0, :, pl.ds(off, kts)].astype(BF16)
        return _flash_step(k, vt, qt_bd, carry, bias_fn(off), ng)

    def sel_bias(off):
        return jnp.concatenate([selb_ref[g, pl.ds(off, kts), :] for g in range(ng) for _ in range(rep)], axis=1)

    def win_bias(off):
        return jnp.concatenate([winb_ref[pl.ds(off, kts), :]] * (ng * rep), axis=1)

    sel_step = functools.partial(step, ks_ref, vst_ref, sel_bias)
    win_step = functools.partial(step, kw_ref, vwt_ref, win_bias)
    carry_s = lax.fori_loop(0, lo_win, sel_step, _flash_init(HEAD_DIM, nq))
    carry_s, carry_w = lax.fori_loop(
        lo_win, hi_kt, lambda kt, c: (sel_step(kt, c[0]), win_step(kt, c[1])),
        (carry_s, _flash_init(HEAD_DIM, nq)))
    o_s = carry_s[2] / carry_s[1]
    o_w = carry_w[2] / carry_w[1]

    pieces = []
    for g in range(ng):
        for r in range(rep):
            h = rep * g + r
            lanes = slice(r * qb, (r + 1) * qb)
            wide = slice(h * qb, (h + 1) * qb)
            pieces.append(gt[3 * h:3 * h + 1, :] * o_cs[g][:, lanes]
                          + gt[3 * h + 1:3 * h + 2, :] * o_s[:, wide]
                          + gt[3 * h + 2:3 * h + 3, :] * o_w[:, wide])
    o_ref[...] = jnp.concatenate(pieces, axis=0).T


def nsa_prompt_attn(q, gates, kc, vct, ks, vst, kw, vwt, b, t):
    nq = t // Q_BLOCK
    n_cmp = t // CMP_BLOCK
    n_sb = -(-t // SEL_BLOCK)
    nsb_pad = -(-n_sb // SUBLANE) * SUBLANE
    k_sel = min(N_SEL, n_sb)
    ratio = SEL_BLOCK // CMP_BLOCK
    poolt = (jnp.arange(nsb_pad)[:, None] == jnp.arange(n_cmp)[None, :] // ratio).astype(BF16)
    expandt = (jnp.arange(t)[:, None] // SEL_BLOCK == jnp.arange(nsb_pad)[None, :]).astype(BF16)
    tok = lambda bi, qi: (bi * nq + qi, 0)
    seq = lambda bi, qi: (bi, 0)
    seq3 = lambda bi, qi: (bi, 0, 0)
    return pl.pallas_call(
        functools.partial(_nsa_prompt_kernel, n_cmp=n_cmp, n_sb=n_sb, k_sel=k_sel),
        grid=(b, nq),
        in_specs=[pl.BlockSpec((Q_BLOCK, NSA_HEADS * HEAD_DIM), tok),
                  pl.BlockSpec((Q_BLOCK, LANE), tok),
                  pl.BlockSpec((1, NSA_KV_HEADS * n_cmp, HEAD_DIM), seq3),
                  pl.BlockSpec((1, HEAD_DIM, NSA_KV_HEADS * n_cmp), seq3),
                  pl.BlockSpec((t, NSA_KV), seq), pl.BlockSpec((1, NSA_KV, t), seq3),
                  pl.BlockSpec((t, NSA_KV), seq), pl.BlockSpec((1, NSA_KV, t), seq3),
                  pl.BlockSpec(poolt.shape, lambda bi, qi: (0, 0)),
                  pl.BlockSpec(expandt.shape, lambda bi, qi: (0, 0))],
        out_specs=pl.BlockSpec((Q_BLOCK, NSA_HEADS * HEAD_DIM), tok),
        out_shape=jax.ShapeDtypeStruct((b * t, NSA_HEADS * HEAD_DIM), F32),
        scratch_shapes=[pltpu.VMEM((NSA_KV_HEADS, t, Q_BLOCK), F32), pltpu.VMEM((t, Q_BLOCK), F32)],
        compiler_params=_params("parallel", "arbitrary"),
        name="nsa_prompt_attn",
    )(q, gates, kc, vct, ks, vst, kw, vwt, poolt, expandt)


def _fold_groups(x, rowg, width):
    out = jnp.where(rowg == 0, x[:, 0:width], 0.0)
    for g in range(1, NSA_KV_HEADS):
        out = out + jnp.where(rowg == g, x[:, g * width:(g + 1) * width], 0.0)
    return out


def _place_groups(x, rowg):
    return jnp.concatenate([jnp.where(rowg == g, x, 0.0) for g in range(NSA_KV_HEADS)], axis=1)


def _nsa_sample_kernel(pt_ref, *refs, n_pages, per_step, past, buf_len, n_cmp, n_sb, k_sel):
    del pt_ref
    for u in range(per_step):
        _nsa_sample_one(u, refs[u * n_pages:(u + 1) * n_pages],
                        refs[(per_step + u) * n_pages:(per_step + u + 1) * n_pages], refs[2 * per_step * n_pages:],
                        pl.program_id(0) * per_step + u, n_pages, past, buf_len, n_cmp, n_sb, k_sel)


def _nsa_sample_one(u, ksp, vsp, refs, bi, n_pages, past, buf_len, n_cmp, n_sb, k_sel):
    (q_ref, gt_ref, kc_ref, vc_ref, ksn_ref, vsn_ref, kwn_ref, vwn_ref, kwnt_ref, vwnt_ref, wk_ref, wv_ref,
     pool_ref, expand_ref, gsel_ref, gselt_ref, o_ref, wko_ref, wvo_ref) = refs
    nh = NSA_HEADS
    rowg = lax.broadcasted_iota(jnp.int32, (nh, 1), 0) // NSA_GROUP
    q16 = q_ref[u] * SCALE
    qmat_f = _place_groups(q16, rowg)
    qmat = qmat_f.astype(BF16)

    s_all = _dot_nt(q16.astype(BF16), kc_ref[u].astype(BF16))
    s_c = _fold_groups(s_all, rowg, n_cmp)
    ncol = lax.broadcasted_iota(jnp.int32, s_c.shape, 1)
    vis = ((ncol + 1) * CMP_BLOCK - 1) <= past
    p_c = _masked_softmax_rows(s_c, vis)
    o_c = _dot(_place_groups(p_c, rowg).astype(BF16), vc_ref[u].astype(BF16))

    gsel = gsel_ref[...]
    hi_, lo_ = _split2(p_c)
    imp = _dot(gsel, hi_) + _dot(gsel, lo_)
    hi_, lo_ = _split2(imp)
    imp_b = _dot(hi_, pool_ref[...]) + _dot(lo_, pool_ref[...])
    sel = _select_blocks(imp_b, past // SEL_BLOCK, n_sb, k_sel)
    sel16 = _dot(gselt_ref[...], sel.astype(BF16))
    maskfull = _dot(sel16.astype(BF16), expand_ref[...])

    def attend(scores, valids, s_new, valid_new, values, v_new):
        m = s_new if valid_new is None else jnp.where(valid_new > 0.5, s_new, NEG_INF)
        for s, vd in zip(scores, valids):
            m = jnp.maximum(m, jnp.max(jnp.where(vd > 0.5, s, NEG_INF), axis=-1, keepdims=True))
        p_new = jnp.exp(s_new - m)
        if valid_new is not None:
            p_new = p_new * valid_new
        l = p_new
        acc = p_new * v_new
        for s, vd, v in zip(scores, valids, values):
            p = jnp.exp(jnp.where(vd > 0.5, s, NEG_INF) - m) * vd
            l = l + jnp.sum(p, axis=-1, keepdims=True)
            acc = acc + _dot_nt(p.astype(BF16), v)
        return acc / l

    kt_all = jnp.concatenate([ksp[j][0].astype(BF16) for j in range(n_pages)], axis=1)
    vt_all = jnp.concatenate([vsp[j][0].astype(BF16) for j in range(n_pages)], axis=1)
    s_new = jnp.sum(qmat_f * ksn_ref[u], axis=-1, keepdims=True)
    o_s = attend([_dot(qmat, kt_all)], [maskfull[:, 0:past]], s_new, maskfull[:, past:past + 1],
                 [vt_all], vsn_ref[u])
    o_s = _fold_groups(o_s, rowg, HEAD_DIM)

    wk = wk_ref[u]
    wv = wv_ref[u]
    s_w = _dot(qmat, wk.astype(BF16))
    wcol = lax.broadcasted_iota(jnp.int32, s_w.shape, 1)
    pos_w = past - buf_len + wcol
    valid_w = jnp.where((past - pos_w < WINDOW) & (pos_w >= 0), 1.0, 0.0)
    s_new = jnp.sum(qmat_f * kwn_ref[u], axis=-1, keepdims=True)
    o_w = attend([s_w], [valid_w], s_new, None, [wv.astype(BF16)], vwn_ref[u])
    o_w = _fold_groups(o_w, rowg, HEAD_DIM)

    gt = gt_ref[u]
    o_ref[u] = gt[:, 0:1] * o_c + gt[:, 1:2] * o_s + gt[:, 2:3] * o_w

    bsel = lax.broadcasted_iota(jnp.int32, kwnt_ref.shape[1:], 1) == bi
    k_col = jnp.sum(jnp.where(bsel, kwnt_ref[0], 0.0), axis=-1, keepdims=True)
    v_col = jnp.sum(jnp.where(bsel, vwnt_ref[0], 0.0), axis=-1, keepdims=True)
    wlane = lax.broadcasted_iota(jnp.int32, (1, buf_len), 1)
    wko_ref[u] = jnp.where(wlane == buf_len - 1, k_col, pltpu.roll(wk, buf_len - 1, 1))
    wvo_ref[u] = jnp.where(wlane == buf_len - 1, v_col, pltpu.roll(wv, buf_len - 1, 1))


def nsa_sample_attn(page_table, pool_k, pool_v, base, q16, gates, kc, vc, ks_new, vs_new,
                    kw_new, vw_new, kwt_new, vwt_new, win_k, win_v, win_base):
    b, n_pages = page_table.shape
    past = n_pages * PAGE_SIZE
    buf_len = win_k.shape[2]
    tk = past + 1
    n_cmp = tk // CMP_BLOCK
    n_sb = -(-tk // SEL_BLOCK)
    k_sel = min(N_SEL, n_sb)
    ratio = SEL_BLOCK // CMP_BLOCK
    pool = (jnp.arange(n_cmp)[:, None] // ratio == jnp.arange(LANE)[None, :]).astype(BF16)
    expand = (jnp.arange(LANE)[:, None] == jnp.arange(past + LANE)[None, :] // SEL_BLOCK).astype(BF16)
    gsel = (jnp.arange(SUBLANE)[:, None] == jnp.arange(NSA_HEADS)[None, :] // NSA_GROUP).astype(BF16)
    gselt = gsel.T
    per_step = NSA_SAMPLE_PER_STEP if b % NSA_SAMPLE_PER_STEP == 0 else 1
    assert win_base % per_step == 0
    pages = lambda: [pl.BlockSpec((1, NSA_KV, PAGE_SIZE),
                                  functools.partial(_page_map_u, j=j, u=u, per_step=per_step, base=base))
                     for u in range(per_step) for j in range(n_pages)]
    per_b = lambda shape: pl.BlockSpec((per_step,) + shape, lambda i, pt: (i, 0, 0))
    const = lambda a: pl.BlockSpec(a.shape, lambda i, pt: (0,) * a.ndim)
    win = pl.BlockSpec((per_step, NSA_KV, buf_len), lambda i, pt: (win_base // per_step + i, 0, 0))
    return pl.pallas_call(
        functools.partial(_nsa_sample_kernel, n_pages=n_pages, per_step=per_step, past=past, buf_len=buf_len,
                          n_cmp=n_cmp, n_sb=n_sb, k_sel=k_sel),
        grid_spec=pltpu.PrefetchScalarGridSpec(
            num_scalar_prefetch=1,
            grid=(b // per_step,),
            in_specs=(pages() + pages()
                      + [per_b((NSA_HEADS, HEAD_DIM)), per_b((NSA_HEADS, 3)),
                         per_b((NSA_KV_HEADS * n_cmp, HEAD_DIM)), per_b((NSA_KV_HEADS * n_cmp, HEAD_DIM)),
                         per_b((1, NSA_KV)), per_b((1, NSA_KV)), per_b((1, NSA_KV)), per_b((1, NSA_KV)),
                         const(kwt_new), const(vwt_new),
                         win, win, const(pool), const(expand), const(gsel), const(gselt)]),
            out_specs=[per_b((NSA_HEADS, HEAD_DIM)), per_b((NSA_KV, buf_len)), per_b((NSA_KV, buf_len))]),
        out_shape=[jax.ShapeDtypeStruct((b, NSA_HEADS, HEAD_DIM), F32),
                   jax.ShapeDtypeStruct((b, NSA_KV, buf_len), F32),
                   jax.ShapeDtypeStruct((b, NSA_KV, buf_len), F32)],
        compiler_params=_params("parallel"),
        name="nsa_sample_attn",
    )(page_table, *([pool_k] * (per_step * n_pages)), *([pool_v] * (per_step * n_pages)), q16, gates, kc, vc,
      ks_new, vs_new, kw_new, vw_new, kwt_new, vwt_new, win_k, win_v, pool, expand, gsel, gselt)


def _diff_lambda(lam_ref, lam_init):
    lv = lam_ref[...]
    a = jnp.sum(lv[0:1] * lv[1:2], axis=-1, keepdims=True)
    c = jnp.sum(lv[2:3] * lv[3:4], axis=-1, keepdims=True)
    return jnp.exp(a) - jnp.exp(c) + lam_init


def _diff_prompt_kernel(q_ref, k_ref, vt_ref, lam_ref, sub_ref, o_ref, *, lam_init):
    si = pl.program_id(1)
    qs = q_ref.shape[0]
    qt = (q_ref[...] * (SCALE * LOG2E)).T.astype(BF16)
    kts = qs
    lane = lax.broadcasted_iota(jnp.int32, (kts, qs), 1)
    krow = lax.broadcasted_iota(jnp.int32, (kts, qs), 0)
    diag_bias = jnp.where(krow <= lane, 0.0, NEG_INF)
    diag_bias = jnp.concatenate([diag_bias] * 2, axis=1)
    lam = _diff_lambda(lam_ref, lam_init)
    vw = 2 * HEAD_DIM

    pieces = []
    for h in range(DIFF_HEADS):
        vrows = slice(h * vw, (h + 1) * vw)
        kcols = slice(2 * h * HEAD_DIM, (2 * h + 2) * HEAD_DIM)
        qt_bd = _block_diag([qt[(2 * h + c) * HEAD_DIM:(2 * h + c + 1) * HEAD_DIM, :] for c in range(2)])

        def tile(kt, carry, bias):
            off = pl.multiple_of(kt * kts, kts)
            k = k_ref[pl.ds(off, kts), kcols].astype(BF16)
            vt = vt_ref[0, vrows, pl.ds(off, kts)].astype(BF16)
            return _flash_step(k, vt, qt_bd, carry, bias, 1)

        carry = lax.fori_loop(0, si, lambda kt, c: tile(kt, c, None), _flash_init(vw, 2 * qs))
        _, l, acc = tile(si, carry, diag_bias)
        outs = acc / l
        o = outs[:, 0:qs] - lam * outs[:, qs:2 * qs]
        o = o * lax.rsqrt(jnp.mean(o * o, axis=0, keepdims=True) + RMS_EPS) * sub_ref[...]
        pieces.append(o * (1.0 - lam_init))
    o_ref[...] = jnp.concatenate(pieces, axis=0).T


def diff_prompt_attn(q, k, vt, lam_vec, sub_norm, lam_init, b, t):
    qs = min(DIFF_Q_BLOCK, t)
    nq = t // qs
    tok = lambda bi, qi: (bi * nq + qi, 0)
    return pl.pallas_call(
        functools.partial(_diff_prompt_kernel, lam_init=lam_init),
        grid=(b, nq),
        in_specs=[pl.BlockSpec((qs, D_MODEL), tok),
                  pl.BlockSpec((t, D_MODEL), lambda bi, qi: (bi, 0)),
                  pl.BlockSpec((1, D_MODEL, t), lambda bi, qi: (bi, 0, 0)),
                  pl.BlockSpec(lam_vec.shape, lambda bi, qi: (0, 0)),
                  pl.BlockSpec((2 * HEAD_DIM, 1), lambda bi, qi: (0, 0))],
        out_specs=pl.BlockSpec((qs, D_MODEL), tok),
        out_shape=jax.ShapeDtypeStruct((b * t, D_MODEL), F32),
        compiler_params=_params("parallel", "arbitrary"),
        name="diff_prompt_attn",
    )(q, k, vt, lam_vec, sub_norm.reshape(2 * HEAD_DIM, 1))


def _diff_sample_kernel(pt_ref, *refs, n_pages, lam_init):
    del pt_ref
    kp = refs[:n_pages]
    vp = refs[n_pages:2 * n_pages]
    q_ref, kn_ref, vn_ref, lam_ref, sub_ref, expm_ref, hmask_ref, o_ref = refs[2 * n_pages:]
    nmap = 2 * DIFF_HEADS
    lam = _diff_lambda(lam_ref, lam_init)
    r = lax.broadcasted_iota(jnp.int32, (nmap, D_MODEL), 0)
    col = lax.broadcasted_iota(jnp.int32, (nmap, D_MODEL), 1)
    own = jnp.where(r < DIFF_HEADS, 2 * r, 2 * (r - DIFF_HEADS) + 1)
    qmat_f = jnp.where(col // HEAD_DIM == own, q_ref[0] * SCALE, 0.0)
    qmat = qmat_f.astype(BF16)
    scores = [_dot(qmat, kp[j][0].astype(BF16)) for j in range(n_pages)]
    s_new = jnp.sum(qmat_f * kn_ref[0], axis=-1, keepdims=True)
    m = s_new
    for s in scores:
        m = jnp.maximum(m, jnp.max(s, axis=-1, keepdims=True))
    p_new = jnp.exp(s_new - m)
    ps = [jnp.exp(s - m) for s in scores]
    l = p_new
    for p in ps:
        l = l + jnp.sum(p, axis=-1, keepdims=True)
    inv = 1.0 / l
    pd_new = (p_new * inv)[0:DIFF_HEADS] - lam * (p_new * inv)[DIFF_HEADS:nmap]
    pds = []
    for j in range(n_pages):
        pn = ps[j] * inv
        pds.append((pn[0:DIFF_HEADS] - lam * pn[DIFF_HEADS:nmap]).astype(BF16))
    expm = expm_ref[...]
    hmask = hmask_ref[...]
    acc = pd_new * vn_ref[0]
    for j in range(n_pages):
        spread = (_dot(pds[j], expm) * hmask).astype(BF16)
        acc = acc + _dot(spread, vp[j][0].astype(BF16))
    o_ref[0] = _rms_rows(acc, sub_ref[...]) * (1.0 - lam_init)


def diff_sample_attn(page_table, pool_k, pool_v, base, q, k_new, v_new, lam_vec, sub_row, lam_init):
    b, n_pages = page_table.shape
    vw = 2 * HEAD_DIM
    rows = PAGE_SIZE * DIFF_HEADS
    expm = (jnp.arange(PAGE_SIZE)[:, None] == jnp.arange(rows)[None, :] // DIFF_HEADS).astype(BF16)
    hmask = (jnp.arange(DIFF_HEADS)[:, None] == jnp.arange(rows)[None, :] % DIFF_HEADS).astype(F32)
    kpage = lambda j: pl.BlockSpec((1, D_MODEL, PAGE_SIZE), functools.partial(_page_map, j=j, base=base))
    vpage = lambda j: pl.BlockSpec((1, rows, vw), functools.partial(_page_map, j=j, base=base))
    per_b = pl.BlockSpec((1, 1, D_MODEL), lambda i, pt: (i, 0, 0))
    per_bh = pl.BlockSpec((1, DIFF_HEADS, vw), lambda i, pt: (i, 0, 0))
    const = lambda a: pl.BlockSpec(a.shape, lambda i, pt: (0, 0))
    return pl.pallas_call(
        functools.partial(_diff_sample_kernel, n_pages=n_pages, lam_init=lam_init),
        grid_spec=pltpu.PrefetchScalarGridSpec(
            num_scalar_prefetch=1,
            grid=(b,),
            in_specs=([kpage(j) for j in range(n_pages)] + [vpage(j) for j in range(n_pages)]
                      + [per_b, per_b, per_bh, const(lam_vec), const(sub_row), const(expm), const(hmask)]),
            out_specs=per_bh),
        out_shape=jax.ShapeDtypeStruct((b, DIFF_HEADS, vw), F32),
        compiler_params=_params("parallel"),
        name="diff_sample_attn",
    )(page_table, *([pool_k] * n_pages), *([pool_v] * n_pages), q, k_new, v_new, lam_vec, sub_row, expm, hmask)


def _shift_rows(x, s, carry, row):
    r = pltpu.roll(x, s, 0)
    top = r[0:SUBLANE]
    for i in range(s):
        top = jnp.where(row[0:SUBLANE] == i, carry[SUBLANE - s + i:SUBLANE - s + i + 1, :], top)
    return jnp.concatenate([top, r[SUBLANE:]], axis=0)


def _conv_silu_seq_kernel(x_ref, w_ref, b_ref, o_ref, carry_ref, *, width):
    @pl.when(pl.program_id(2) == 0)
    def _():
        carry_ref[...] = jnp.zeros(carry_ref.shape, F32)

    x = x_ref[...]
    tm = x.shape[0]
    row = lax.broadcasted_iota(jnp.int32, (tm, 1), 0)
    carry = carry_ref[...]
    acc = x * w_ref[width - 1:width, :]
    for s in range(1, width):
        acc = acc + _shift_rows(x, s, carry, row) * w_ref[width - 1 - s:width - s, :]
    acc = acc + b_ref[...]
    o_ref[...] = _silu(acc)
    carry_ref[...] = x[tm - SUBLANE:tm, :]


def ssd_conv_prompt(proj, conv_w, conv_b, b, t, tm):
    cb = 1024
    nt = t // tm
    c0 = SSD_D_INNER // cb
    return pl.pallas_call(
        functools.partial(_conv_silu_seq_kernel, width=SSD_CONV_W),
        grid=(b, SSD_CONV_DIM // cb, nt),
        in_specs=[pl.BlockSpec((tm, cb), lambda bi, j, ti: (bi * nt + ti, c0 + j)),
                  pl.BlockSpec((SSD_CONV_W, cb), lambda bi, j, ti: (0, j)),
                  pl.BlockSpec((1, cb), lambda bi, j, ti: (0, j))],
        out_specs=pl.BlockSpec((tm, cb), lambda bi, j, ti: (bi * nt + ti, j)),
        out_shape=jax.ShapeDtypeStruct((b * t, SSD_CONV_DIM), F32),
        scratch_shapes=[pltpu.VMEM((SUBLANE, cb), F32)],
        compiler_params=_params("parallel", "parallel", "arbitrary"),
        name="ssd_conv_prompt",
    )(proj, conv_w, conv_b.reshape(1, SSD_CONV_DIM))


def _conv_silu_state_kernel(x_ref, p0_ref, p1_ref, p2_ref, w_ref, b_ref, o_ref):
    acc = (p0_ref[...] * w_ref[0:1, :] + p1_ref[...] * w_ref[1:2, :] + p2_ref[...] * w_ref[2:3, :]
           + x_ref[...] * w_ref[3:4, :] + b_ref[...])
    o_ref[...] = _silu(acc)


def ssd_conv_sample(proj, prev, conv_w, conv_b):
    b = proj.shape[0]
    cb = 1024
    c0 = SSD_D_INNER // cb
    col = lambda j: (0, j)
    return pl.pallas_call(
        _conv_silu_state_kernel,
        grid=(SSD_CONV_DIM // cb,),
        in_specs=[pl.BlockSpec((b, cb), lambda j: (0, c0 + j)),
                  pl.BlockSpec((b, cb), col), pl.BlockSpec((b, cb), col), pl.BlockSpec((b, cb), col),
                  pl.BlockSpec((SSD_CONV_W, cb), col), pl.BlockSpec((1, cb), col)],
        out_specs=pl.BlockSpec((b, cb), col),
        out_shape=jax.ShapeDtypeStruct((b, SSD_CONV_DIM), F32),
        compiler_params=_params("parallel"),
        name="ssd_conv_sample",
    )(proj, prev[0], prev[1], prev[2], conv_w, conv_b.reshape(1, SSD_CONV_DIM))


def _ssd_scan_kernel(xbc_ref, dt_ref, bias_ref, alog_ref, dskip_ref, tril_ref, exp_ref,
                     y_ref, st_ref, state_ref):
    @pl.when(pl.program_id(1) == 0)
    def _():
        state_ref[...] = jnp.zeros(state_ref.shape, F32)

    l = SSD_CHUNK
    di = SSD_D_INNER
    n = SSD_D_STATE
    tril = tril_ref[...]
    expm = exp_ref[...]
    dt = _softplus(dt_ref[...] + bias_ref[...])
    a = dt * (-jnp.exp(alog_ref[...]))
    acs = _lhs_exact_dot(tril, a)
    acs_t = acs.T
    dtx = _dot_exact_rhs(dt, expm)
    eacs = jnp.exp(acs)
    eacsx = _dot_exact_rhs(eacs, expm)
    decx = _dot_exact_rhs(jnp.exp(acs[l - 1:l, :] - acs), expm)
    x = xbc_ref[:, 0:di]
    xdt = x * dtx
    xw = (xdt * decx).astype(BF16)
    xdt_b = xdt.astype(BF16)
    ri = lax.broadcasted_iota(jnp.int32, (l, l), 0)
    ci = lax.broadcasted_iota(jnp.int32, (l, l), 1)
    lower = ri >= ci
    gw = SSD_GROUP_W
    for g in range(SSD_GROUPS):
        bg = xbc_ref[:, di + g * n:di + (g + 1) * n]
        cg = xbc_ref[:, di + SSD_GROUPS * n + g * n:di + SSD_GROUPS * n + (g + 1) * n].astype(BF16)
        cb = _dot_nt(cg, bg.astype(BF16))
        st_g = state_ref[:, g * gw:(g + 1) * gw]
        y_off = _dot(cg, st_g.astype(BF16)) * eacsx[:, g * gw:(g + 1) * gw]
        state_ref[:, g * gw:(g + 1) * gw] = (
            st_g * eacsx[l - 1:l, g * gw:(g + 1) * gw] + _dot(bg.T.astype(BF16), xw[:, g * gw:(g + 1) * gw]))
        for hh in range(SSD_HEADS_PER_GROUP):
            h = g * SSD_HEADS_PER_GROUP + hh
            cols = slice(h * SSD_HEADDIM, (h + 1) * SSD_HEADDIM)
            seg = acs[:, h:h + 1] - acs_t[h:h + 1, :]
            lmat = jnp.exp(jnp.where(lower, seg, NEG_INF))
            yd = _dot((cb * lmat).astype(BF16), xdt_b[:, cols])
            y_ref[:, cols] = (yd + y_off[:, hh * SSD_HEADDIM:(hh + 1) * SSD_HEADDIM]
                              + dskip_ref[:, cols] * x[:, cols])
    st_ref[0] = state_ref[...]


def ssd_scan_prompt(xbc, proj, dt_bias_row, a_log_row, dskip_row, b, t):
    nc = t // SSD_CHUNK
    l = SSD_CHUNK
    tril = (jnp.arange(l)[:, None] >= jnp.arange(l)[None, :]).astype(BF16)
    expm = (jnp.arange(LANE)[:, None] == jnp.arange(SSD_D_INNER)[None, :] // SSD_HEADDIM).astype(BF16)
    dt_blk = (SSD_D_INNER + SSD_CONV_DIM) // LANE
    tok = lambda bi, ci: (bi * nc + ci, 0)
    const = lambda a: pl.BlockSpec(a.shape, lambda bi, ci: (0, 0))
    return pl.pallas_call(
        _ssd_scan_kernel,
        grid=(b, nc),
        in_specs=[pl.BlockSpec((l, SSD_CONV_DIM), tok),
                  pl.BlockSpec((l, LANE), lambda bi, ci: (bi * nc + ci, dt_blk)),
                  const(dt_bias_row), const(a_log_row), const(dskip_row), const(tril), const(expm)],
        out_specs=[pl.BlockSpec((l, SSD_D_INNER), tok),
                   pl.BlockSpec((1, SSD_D_STATE, SSD_D_INNER), lambda bi, ci: (bi, 0, 0))],
        out_shape=[jax.ShapeDtypeStruct((b * t, SSD_D_INNER), F32),
                   jax.ShapeDtypeStruct((b, SSD_D_STATE, SSD_D_INNER), F32)],
        scratch_shapes=[pltpu.VMEM((SSD_D_STATE, SSD_D_INNER), F32)],
        compiler_params=_params("parallel", "arbitrary"),
        name="ssd_scan_prompt",
    )(xbc, proj, dt_bias_row, a_log_row, dskip_row, tril, expm)


def _ssd_step_kernel(xbc_ref, dt_ref, bias_ref, alog_ref, dskip_ref, s_ref, y_ref, so_ref):
    di = SSD_D_INNER
    n = SSD_D_STATE
    p = SSD_HEADDIM
    dt = _softplus(dt_ref[0] + bias_ref[...])
    dec = jnp.exp(dt * (-jnp.exp(alog_ref[...])))
    ri = lax.broadcasted_iota(jnp.int32, (p, p), 0)
    ci = lax.broadcasted_iota(jnp.int32, (p, p), 1)
    eye = ri == ci
    for g in range(SSD_GROUPS):
        bmat = jnp.broadcast_to(xbc_ref[0, :, di + g * n:di + (g + 1) * n], (p, n)).astype(BF16)
        c8 = jnp.broadcast_to(xbc_ref[0, :, di + SSD_GROUPS * n + g * n:di + SSD_GROUPS * n + (g + 1) * n],
                              (SUBLANE, n)).astype(BF16)
        for hh in range(SSD_HEADS_PER_GROUP):
            h = g * SSD_HEADS_PER_GROUP + hh
            cols = slice(h * p, (h + 1) * p)
            xh = xbc_ref[0, :, cols]
            xdt = xh * dt[:, h:h + 1]
            diag = jnp.where(eye, jnp.broadcast_to(xdt, (p, p)), 0.0)
            hi_, lo_ = _split2(diag)
            s_new = s_ref[0, h] * dec[:, h:h + 1] + _dot(hi_, bmat) + _dot(lo_, bmat)
            so_ref[0, h] = s_new
            yh = _dot_nt(c8, s_new.astype(BF16))
            y_ref[0, :, cols] = yh[0:1] + dskip_ref[:, cols] * xh


def ssd_step_sample(xbc, proj, dt_bias_row, a_log_row, dskip_row, state, state_base):
    b = xbc.shape[0]
    dt_blk = (SSD_D_INNER + SSD_CONV_DIM) // LANE
    const = lambda a: pl.BlockSpec(a.shape, lambda i: (0, 0))
    st_shape = (1, SSD_HEADS, SSD_HEADDIM, SSD_D_STATE)
    return pl.pallas_call(
        _ssd_step_kernel,
        grid=(b,),
        in_specs=[pl.BlockSpec((1, 1, SSD_CONV_DIM), lambda i: (i, 0, 0)),
                  pl.BlockSpec((1, 1, LANE), lambda i: (i, 0, dt_blk)),
                  const(dt_bias_row), const(a_log_row), const(dskip_row),
                  pl.BlockSpec(st_shape, lambda i: (state_base + i, 0, 0, 0))],
        out_specs=[pl.BlockSpec((1, 1, SSD_D_INNER), lambda i: (i, 0, 0)),
                   pl.BlockSpec(st_shape, lambda i: (i, 0, 0, 0))],
        out_shape=[jax.ShapeDtypeStruct((b, 1, SSD_D_INNER), F32),
                   jax.ShapeDtypeStruct((b,) + st_shape[1:], F32)],
        compiler_params=_params("parallel"),
        name="ssd_step_sample",
    )(xbc.reshape(b, 1, SSD_CONV_DIM), proj.reshape(b, 1, proj.shape[1]),
      dt_bias_row, a_log_row, dskip_row, state)


def _ssd_out_kernel(y_ref, z_ref, ng_ref, w_ref, r_ref, o_ref):
    gated = y_ref[...] * _silu(z_ref[...])
    parts = []
    for g in range(SSD_GROUPS):
        cols = slice(g * SSD_GROUP_W, (g + 1) * SSD_GROUP_W)
        parts.append(_rms_rows(gated[:, cols], ng_ref[:, cols]).astype(BF16))
    o_ref[...] = r_ref[...] + _dot(jnp.concatenate(parts, axis=1), w_ref[...])


def ssd_out(y, proj, norm_g, w, res, tm):
    m = y.shape[0]
    di = SSD_D_INNER
    return pl.pallas_call(
        _ssd_out_kernel,
        grid=(m // tm,),
        in_specs=[pl.BlockSpec((tm, di), lambda i: (i, 0)),
                  pl.BlockSpec((tm, di), lambda i: (i, 0)),
                  pl.BlockSpec((1, di), lambda i: (0, 0)),
                  pl.BlockSpec((di, D_MODEL), lambda i: (0, 0)),
                  pl.BlockSpec((tm, D_MODEL), lambda i: (i, 0))],
        out_specs=pl.BlockSpec((tm, D_MODEL), lambda i: (i, 0)),
        out_shape=jax.ShapeDtypeStruct((m, D_MODEL), F32),
        compiler_params=_params("parallel"),
        name="ssd_out",
    )(y, proj, norm_g.reshape(1, di), w, res)


def _ffn_seq_kernel(x_ref, g_ref, wup_ref, cw_ref, wdn_ref, o_ref, st_ref, carry_ref):
    @pl.when(pl.program_id(1) == 0)
    def _():
        carry_ref[...] = jnp.zeros(carry_ref.shape, F32)

    x = x_ref[...]
    tm = x.shape[0]
    h = _rms_rows(x, g_ref[...]).astype(BF16)
    row = lax.broadcasted_iota(jnp.int32, (tm, 1), 0)
    fc = FFN_CHUNK

    def conv(u, c0):
        w = cw_ref[:, c0:c0 + fc]
        carry = carry_ref[:, c0:c0 + fc]
        y = u * w[2:3] + _shift_rows(u, 1, carry, row) * w[1:2] + _shift_rows(u, 2, carry, row) * w[0:1]
        carry_ref[:, c0:c0 + fc] = u[tm - SUBLANE:tm, :]
        return y

    acc = jnp.zeros((tm, D_MODEL), F32)
    for c0 in range(0, D_FF, fc):
        u = conv(_dot(h, wup_ref[:, c0:c0 + fc]), c0)
        gate = conv(_dot(h, wup_ref[:, D_FF + c0:D_FF + c0 + fc]), D_FF + c0)
        acc = acc + _dot((_silu(gate) * u).astype(BF16), wdn_ref[c0:c0 + fc, :])
    o_ref[...] = x + acc
    st_ref[0] = carry_ref[...]


def ffn_prompt(x, g, w_up, conv_w, w_down, b, t, tm):
    nt = t // tm
    tok = lambda bi, ti: (bi * nt + ti, 0)
    const = lambda a: pl.BlockSpec(a.shape, lambda bi, ti: (0, 0))
    g = g.reshape(1, D_MODEL)
    return pl.pallas_call(
        _ffn_seq_kernel,
        grid=(b, nt),
        in_specs=[pl.BlockSpec((tm, D_MODEL), tok), const(g), const(w_up), const(conv_w), const(w_down)],
        out_specs=[pl.BlockSpec((tm, D_MODEL), tok),
                   pl.BlockSpec((1, SUBLANE, 2 * D_FF), lambda bi, ti: (bi, 0, 0))],
        out_shape=[jax.ShapeDtypeStruct((b * t, D_MODEL), F32),
                   jax.ShapeDtypeStruct((b, SUBLANE, 2 * D_FF), F32)],
        scratch_shapes=[pltpu.VMEM((SUBLANE, 2 * D_FF), F32)],
        compiler_params=_params("parallel", "arbitrary"),
        name="ffn_prompt",
    )(x, g, w_up, conv_w, w_down)


def _ffn_state_kernel(x_ref, g_ref, wup_ref, cw_ref, wdn_ref, p0_ref, p1_ref, o_ref, up_ref):
    x = x_ref[...]
    h = _rms_rows(x, g_ref[...]).astype(BF16)
    fc = FFN_CHUNK

    def conv(u, c0):
        up_ref[:, c0:c0 + fc] = u
        w = cw_ref[:, c0:c0 + fc]
        return u * w[2:3] + p1_ref[:, c0:c0 + fc] * w[1:2] + p0_ref[:, c0:c0 + fc] * w[0:1]

    acc = jnp.zeros(x.shape, F32)
    for c0 in range(0, D_FF, fc):
        u = conv(_dot(h, wup_ref[:, c0:c0 + fc]), c0)
        gate = conv(_dot(h, wup_ref[:, D_FF + c0:D_FF + c0 + fc]), D_FF + c0)
        acc = acc + _dot((_silu(gate) * u).astype(BF16), wdn_ref[c0:c0 + fc, :])
    o_ref[...] = x + acc


def ffn_sample(x, g, w_up, conv_w, w_down, prev):
    b = x.shape[0]
    full = lambda a: pl.BlockSpec(a.shape, lambda i: (0,) * a.ndim)
    g = g.reshape(1, D_MODEL)
    p0, p1 = prev[:, 0], prev[:, 1]
    return pl.pallas_call(
        _ffn_state_kernel,
        grid=(1,),
        in_specs=[full(x), full(g), full(w_up), full(conv_w), full(w_down), full(p0), full(p1)],
        out_specs=[pl.BlockSpec((b, D_MODEL), lambda i: (0, 0)),
                   pl.BlockSpec((b, 2 * D_FF), lambda i: (0, 0))],
        out_shape=[jax.ShapeDtypeStruct((b, D_MODEL), F32),
                   jax.ShapeDtypeStruct((b, 2 * D_FF), F32)],
        compiler_params=_params("arbitrary"),
        name="ffn_sample",
    )(x, g, w_up, conv_w, w_down, p0, p1)


def _rope_tables(pos, rows):
    half = HEAD_DIM // 2
    inv_freq = ROPE_THETA ** (-jnp.arange(half, dtype=F32) / half)
    ang = pos.astype(F32)[:, None] * inv_freq[None, :]
    cos = jnp.cos(ang)
    sin = jnp.sin(ang)
    cos = jnp.tile(jnp.concatenate([cos, cos], axis=-1), (1, 4))
    sin = jnp.tile(jnp.concatenate([-sin, sin], axis=-1), (1, 4))
    if cos.shape[0] != rows:
        cos = jnp.broadcast_to(cos, (rows, 256))
        sin = jnp.broadcast_to(sin, (rows, 256))
    return cos, sin


def _block_diag_ones():
    i = jnp.arange(256)
    return (i[:, None] // HEAD_DIM == i[None, :] // HEAD_DIM).astype(BF16)


def _pad_cols(w, n):
    return jnp.pad(w, ((0, 0), (0, n - w.shape[1])))


def _nsa_weights(w_in, q_norm, k_norm, cmp_pos, cmp_w1, cmp_w2, w_out):
    gain = jnp.concatenate(
        [jnp.tile(q_norm, NSA_HEADS)]
        + [jnp.tile(k_norm[br], NSA_KV_HEADS) if kv == 0 else jnp.ones((NSA_KV,), F32)
           for br in range(3) for kv in range(2)]).reshape(1, NSA_QKV)
    pos_rows = [jnp.broadcast_to(cmp_pos[i][:, None, :], (CMP_BLOCK, NSA_KV_HEADS, HEAD_DIM))
                .reshape(1, CMP_BLOCK * NSA_KV) for i in range(2)]
    pos_blk = [cmp_pos[i].reshape(1, CMP_BLOCK * HEAD_DIM) for i in range(2)]
    return dict(w_in=_pad_cols(w_in, NSA_IN_PAD).astype(BF16), gain=gain, pos=pos_rows, pos_blk=pos_blk,
                w1=[cmp_w1[i].astype(BF16) for i in range(2)],
                w2=[cmp_w2[i].astype(BF16) for i in range(2)],
                w_out=w_out.astype(BF16))


def _kv_rows_view(a):
    b, _, t = a.shape
    return a.reshape(b, NSA_KV_HEADS, HEAD_DIM, t).transpose(0, 3, 1, 2)


def _nsa_prompt_layer(x, gmix, w, tabs, bd, b, t, tm):
    assert t % CMP_BLOCK == 0
    q, kc_rows, vc_rows, ks, kw, gates, kct, vct, kst, vst, kwt, vwt = proj_post(
        x, gmix, w["w_in"], w["gain"], tabs[0], tabs[1], bd, NSA_PLAN_P, NSA_OUTS_P, tm, b)
    n_cmp = t // CMP_BLOCK
    wide = CMP_BLOCK * NSA_KV
    kc, _ = compress_rows(kc_rows.reshape(b, n_cmp, wide), w["pos"][0], w["w1"][0], w["w2"][0], n_cmp)
    _, vc_t = compress_rows(vc_rows.reshape(b, n_cmp, wide), w["pos"][1], w["w1"][1], w["w2"][1], n_cmp)
    o = nsa_prompt_attn(q, gates, kc, vc_t, ks, vst, kw, vwt, b, t)
    x = matmul_res(o, w["w_out"], x, tm)
    keep = min(WINDOW, t)
    rows = tuple(_kv_rows_view(a) for a in (kct, vct, kst, vst))
    wins = tuple(_kv_rows_view(a[:, :, t - keep:]) for a in (kwt, vwt))
    return x, rows + wins


def _nsa_sample_layer(x, gmix, w, tabs, bd, j, page_table, caches, win_k, win_v):
    b = x.shape[0]
    n_phys = caches[0].shape[1]
    q, kc_new, vc_new, ks_new, vs_new, kw_new, vw_new, gates, kwt_new, vwt_new = proj_post(
        x, gmix, w["w_in"], w["gain"], tabs[0], tabs[1], bd, NSA_PLAN_S, NSA_OUTS_S, b, 1)
    fm = lambda c: c.transpose(0, 1, 3, 4, 2).reshape(-1, NSA_KV, c.shape[2])
    pool_ck, pool_cv, pool_sk, pool_sv = (fm(c) for c in caches)
    kc = compress_pages(pool_ck, page_table, j * n_phys, w["pos_blk"][0], w["w1"][0], w["w2"][0])
    vc = compress_pages(pool_cv, page_table, j * n_phys, w["pos_blk"][1], w["w1"][1], w["w2"][1])
    row3 = lambda a: a.reshape(b, 1, NSA_KV)
    o16, wk_out, wv_out = nsa_sample_attn(
        page_table, pool_sk, pool_sv, j * n_phys,
        q.reshape(b, NSA_HEADS, HEAD_DIM), gates[:, :NSA_HEADS * 3].reshape(b, NSA_HEADS, 3), kc, vc,
        row3(ks_new), row3(vs_new), row3(kw_new), row3(vw_new), kwt_new, vwt_new,
        fm(win_k), fm(win_v), j * b)
    x = matmul_res(o16.reshape(b, NSA_HEADS * HEAD_DIM), w["w_out"], x, b)
    kvshape = (b, 1, NSA_KV_HEADS, HEAD_DIM)
    rows = tuple(a.reshape(kvshape) for a in (kc_new, vc_new, ks_new, vs_new))
    wins = tuple(_kv_rows_view(a) for a in (wk_out, wv_out))
    return x, rows + wins


def _diff_weights(w_in, q_norm, k_norm, lam_vec, sub_norm, w_out):
    nmap = 2 * DIFF_HEADS
    gain = jnp.concatenate([jnp.tile(q_norm, nmap), jnp.tile(k_norm, nmap),
                            jnp.ones((D_MODEL,), F32)]).reshape(1, 3 * D_MODEL)
    return dict(w_in=w_in.astype(BF16), gain=gain, lam=lam_vec, sub=sub_norm, w_out=w_out.astype(BF16))


def kernel(x_prompt, x_sample, cache_nsa_cmp_k, cache_nsa_cmp_v, cache_nsa_slc_k, cache_nsa_slc_v, state_nsa_win_k, state_nsa_win_v, cache_diff_k, cache_diff_v, state_ssd_conv, state_ssd_ssm, state_ffn_conv, page_table, norm_mix, norm_ffn, nsa_w_in, nsa_q_norm, nsa_k_norm, nsa_cmp_pos, nsa_cmp_w1, nsa_cmp_w2, nsa_w_out, diff_w_in, diff_q_norm, diff_k_norm, diff_lambda, diff_sub_norm, diff_w_out, ssd_w_in, ssd_conv_w, ssd_conv_b, ssd_dt_bias, ssd_a_log, ssd_d, ssd_norm, ssd_w_out, ffn_w_up, ffn_conv_w, ffn_w_down):
    bp, tp, _ = x_prompt.shape
    bs, ts, _ = x_sample.shape
    assert ts == 1 and tp % Q_BLOCK == 0
    n_pages = page_table.shape[1]
    past = n_pages * PAGE_SIZE
    depth = norm_mix.shape[0]
    tm_p = min(PROMPT_ROW_TILE, tp)
    xp = x_prompt.reshape(bp * tp, D_MODEL)
    xs = x_sample.reshape(bs, D_MODEL)
    bd = _block_diag_ones()
    tabs_p = _rope_tables(jnp.arange(tp, dtype=jnp.int32), tp)
    tabs_s = _rope_tables(jnp.full((1,), past, jnp.int32), bs)
    nsa_p, nsa_s, diff_p, diff_s, ssd_p, ssd_s, ffn_p, ffn_s = [], [], [], [], [], [], [], []
    for i in range(depth):
        kind = i % N_MIXERS
        j = i // N_MIXERS
        if kind == 0:
            w = _nsa_weights(nsa_w_in[j], nsa_q_norm[j], nsa_k_norm[j], nsa_cmp_pos[j], nsa_cmp_w1[j],
                             nsa_cmp_w2[j], nsa_w_out[j])
            xp, outs = _nsa_prompt_layer(xp, norm_mix[i], w, tabs_p, bd, bp, tp, tm_p)
            nsa_p.append(outs)
            xs, outs = _nsa_sample_layer(xs, norm_mix[i], w, tabs_s, bd, j, page_table,
                                         (cache_nsa_cmp_k, cache_nsa_cmp_v, cache_nsa_slc_k, cache_nsa_slc_v),
                                         state_nsa_win_k, state_nsa_win_v)
            nsa_s.append(outs)
        elif kind == 1:
            lam_init = 0.8 - 0.6 * math.exp(-0.3 * i)
            w = _diff_weights(diff_w_in[j], diff_q_norm[j], diff_k_norm[j], diff_lambda[j], diff_sub_norm[j],
                              diff_w_out[j])
            q, k, v, kt, vt = proj_post(xp, norm_mix[i], w["w_in"], w["gain"], tabs_p[0], tabs_p[1], bd,
                                        DIFF_PLAN_P, DIFF_OUTS_P, tm_p, bp)
            o = diff_prompt_attn(q, k, vt, w["lam"], w["sub"], lam_init, bp, tp)
            xp = matmul_res(o, w["w_out"], xp, tm_p)
            diff_p.append((kt.reshape(bp, DIFF_HEADS, 2, HEAD_DIM, tp).transpose(0, 4, 1, 2, 3),
                           v.reshape(bp, tp, DIFF_HEADS, 2 * HEAD_DIM)))
            n_phys = cache_diff_k.shape[1]
            q, k, v = proj_post(xs, norm_mix[i], w["w_in"], w["gain"], tabs_s[0], tabs_s[1], bd,
                                DIFF_PLAN_S, DIFF_OUTS_S, bs, 1)
            r3 = lambda a: a.reshape(bs, 1, D_MODEL)
            pool_k = cache_diff_k.transpose(0, 1, 3, 4, 5, 2).reshape(-1, D_MODEL, PAGE_SIZE)
            pool_v = cache_diff_v.reshape(-1, PAGE_SIZE * DIFF_HEADS, 2 * HEAD_DIM)
            o = diff_sample_attn(page_table, pool_k, pool_v, j * n_phys,
                                 r3(q), r3(k), v.reshape(bs, DIFF_HEADS, 2 * HEAD_DIM), w["lam"],
                                 w["sub"].reshape(1, 2 * HEAD_DIM), lam_init)
            xs = matmul_res(o.reshape(bs, D_MODEL), w["w_out"], xs, bs)
            diff_s.append((k.reshape(bs, 1, DIFF_HEADS, 2, HEAD_DIM), v.reshape(bs, 1, DIFF_HEADS, 2 * HEAD_DIM)))
        else:
            w_in = _pad_cols(ssd_w_in[j], SSD_IN_PAD).astype(BF16)
            w_out = ssd_w_out[j].astype(BF16)
            pad_h = lambda a: jnp.pad(a, (0, LANE - SSD_HEADS)).reshape(1, LANE)
            bias_row, alog_row = pad_h(ssd_dt_bias[j]), pad_h(ssd_a_log[j])
            dskip_row = jnp.repeat(ssd_d[j], SSD_HEADDIM).reshape(1, SSD_D_INNER)
            xbc0 = SSD_D_INNER
            proj = norm_matmul(xp, norm_mix[i], w_in, tm_p)
            xbc = ssd_conv_prompt(proj, ssd_conv_w[j], ssd_conv_b[j], bp, tp, tm_p)
            y, st = ssd_scan_prompt(xbc, proj, bias_row, alog_row, dskip_row, bp, tp)
            xp = ssd_out(y, proj, ssd_norm[j], w_out, xp, tm_p)
            conv_new = proj.reshape(bp, tp, SSD_IN_PAD)[:, tp - (SSD_CONV_W - 1):, xbc0:xbc0 + SSD_CONV_DIM]
            ssm_new = st.reshape(bp, SSD_D_STATE, SSD_HEADS, SSD_HEADDIM).transpose(0, 2, 3, 1)
            ssd_p.append((conv_new, ssm_new))
            proj = norm_matmul(xs, norm_mix[i], w_in, bs)
            prev = state_ssd_conv[j]
            xbc = ssd_conv_sample(proj, prev.transpose(1, 0, 2), ssd_conv_w[j], ssd_conv_b[j])
            y, st = ssd_step_sample(xbc, proj, bias_row, alog_row, dskip_row,
                                    state_ssd_ssm.reshape((-1,) + state_ssd_ssm.shape[2:]), j * bs)
            xs = ssd_out(y.reshape(bs, SSD_D_INNER), proj, ssd_norm[j], w_out, xs, bs)
            conv_new = jnp.concatenate([prev[:, 1:], proj[:, None, xbc0:xbc0 + SSD_CONV_DIM]], axis=1)
            ssd_s.append((conv_new, st))
        w_up = ffn_w_up[i].astype(BF16)
        w_dn = ffn_w_down[i].astype(BF16)
        xp, st = ffn_prompt(xp, norm_ffn[i], w_up, ffn_conv_w[i], w_dn, bp, tp, tm_p)
        ffn_p.append(st[:, SUBLANE - (FFN_CONV_W - 1):])
        prev = state_ffn_conv[i]
        xs, up = ffn_sample(xs, norm_ffn[i], w_up, ffn_conv_w[i], w_dn, prev)
        ffn_s.append(jnp.concatenate([prev[:, 1:], up[:, None]], axis=1))
    outs = [xp.reshape(bp, tp, D_MODEL), xs.reshape(bs, ts, D_MODEL)]
    for r in range(6):
        outs.append(jnp.stack([o[r] for o in nsa_p]))
        outs.append(jnp.stack([o[r] for o in nsa_s]))
    for r in range(2):
        outs.append(jnp.stack([o[r] for o in diff_p]))
        outs.append(jnp.stack([o[r] for o in diff_s]))
    for r in range(2):
        outs.append(jnp.stack([o[r] for o in ssd_p]))
        outs.append(jnp.stack([o[r] for o in ssd_s]))
    outs.append(jnp.stack(ffn_p))
    outs.append(jnp.stack(ffn_s))
    return tuple(outs)
```

```python
import functools
import math

import jax
import jax.numpy as jnp
from jax import lax
from jax.experimental import pallas as pl
from jax.experimental.pallas import tpu as pltpu

F32 = jnp.float32
BF16 = jnp.bfloat16

D_MODEL = 1024
DEPTH = 4
PAGE_SIZE = 128
N_MIXERS = 3
RMS_EPS = 1e-6
ROPE_THETA = 10000.0
Q_BLOCK = 128
NEG_INF = -1e30
HEAD_DIM = 64
SCALE = HEAD_DIM ** -0.5
LOG2E = 1.4426950408889634

NSA_HEADS = D_MODEL // HEAD_DIM
NSA_KV_HEADS = 4
NSA_GROUP = NSA_HEADS // NSA_KV_HEADS
CMP_BLOCK = 32
SEL_BLOCK = 64
N_SEL = 8
WINDOW = 512
FORCED_SCORE = 1e4
NSA_KEY_TILE = 256
NSA_KV = NSA_KV_HEADS * HEAD_DIM
NSA_QKV = NSA_HEADS * HEAD_DIM + 6 * NSA_KV
NSA_IN_PAD = NSA_QKV + 128

DIFF_HEADS = D_MODEL // (2 * HEAD_DIM)
DIFF_Q_BLOCK = 512

SSD_D_INNER = 2 * D_MODEL
SSD_HEADDIM = 64
SSD_HEADS = SSD_D_INNER // SSD_HEADDIM
SSD_GROUPS = 4
SSD_D_STATE = 128
SSD_CONV_W = 4
SSD_CHUNK = 128
SSD_CONV_DIM = SSD_D_INNER + 2 * SSD_GROUPS * SSD_D_STATE
SSD_IN_PAD = SSD_D_INNER + SSD_CONV_DIM + 128
SSD_GROUP_W = SSD_D_INNER // SSD_GROUPS
SSD_HEADS_PER_GROUP = SSD_HEADS // SSD_GROUPS

D_FF = 2816
FFN_CONV_W = 3
FFN_CHUNK = 1408

PROMPT_ROW_TILE = 512
LANE = 128
SUBLANE = 8
VMEM_LIMIT = 56 * 1024 * 1024
NSA_SAMPLE_PER_STEP = 2
INVALID = -3.2e38

NT_DIMS = (((1,), (1,)), ((), ()))


def _params(*sem):
    return pltpu.CompilerParams(dimension_semantics=sem, vmem_limit_bytes=VMEM_LIMIT)


def _dot(a, b):
    return jnp.dot(a, b, preferred_element_type=F32)


def _dot_nt(a, b):
    return lax.dot_general(a, b, NT_DIMS, preferred_element_type=F32)


def _split2(x):
    hi = x.astype(BF16)
    lo = (x - hi.astype(F32)).astype(BF16)
    return hi, lo


def _split3(x):
    hi = x.astype(BF16)
    r = x - hi.astype(F32)
    mid = r.astype(BF16)
    lo = (r - mid.astype(F32)).astype(BF16)
    return hi, mid, lo


def _dot_exact_rhs(x, m):
    hi, mid, lo = _split3(x)
    return _dot(hi, m) + _dot(mid, m) + _dot(lo, m)


def _lhs_exact_dot(m, x):
    hi, mid, lo = _split3(x)
    return _dot(m, hi) + _dot(m, mid) + _dot(m, lo)


def _sigmoid(x):
    return 1.0 / (1.0 + jnp.exp(-x))


def _silu(x):
    return x * _sigmoid(x)


def _softplus(x):
    return jnp.maximum(x, 0.0) + jnp.log1p(jnp.exp(-jnp.abs(x)))


def _gelu_tanh(x):
    return 0.5 * x * (1.0 + jnp.tanh(math.sqrt(2.0 / math.pi) * (x + 0.044715 * (x * x * x))))


def _rms_rows(x, g):
    return x * lax.rsqrt(jnp.mean(x * x, axis=-1, keepdims=True) + RMS_EPS) * g


def _norm_matmul_kernel(x_ref, g_ref, w_ref, o_ref, *, chunk):
    h = _rms_rows(x_ref[...], g_ref[...]).astype(BF16)
    n = o_ref.shape[1]
    for c in range(0, n, chunk):
        w = min(chunk, n - c)
        o_ref[:, c:c + w] = _dot(h, w_ref[:, c:c + w])


def norm_matmul(x, g, w, tm):
    m, k = x.shape
    n = w.shape[1]
    return pl.pallas_call(
        functools.partial(_norm_matmul_kernel, chunk=512),
        grid=(m // tm,),
        in_specs=[pl.BlockSpec((tm, k), lambda i: (i, 0)),
                  pl.BlockSpec((1, k), lambda i: (0, 0)),
                  pl.BlockSpec((k, n), lambda i: (0, 0))],
        out_specs=pl.BlockSpec((tm, n), lambda i: (i, 0)),
        out_shape=jax.ShapeDtypeStruct((m, n), F32),
        compiler_params=_params("parallel"),
        name="norm_matmul",
    )(x, g.reshape(1, k), w)


def _matmul_res_kernel(a_ref, w_ref, r_ref, o_ref):
    o_ref[...] = r_ref[...] + _dot(a_ref[...].astype(BF16), w_ref[...])


def matmul_res(a, w, res, tm):
    m, k = a.shape
    n = w.shape[1]
    return pl.pallas_call(
        _matmul_res_kernel,
        grid=(m // tm,),
        in_specs=[pl.BlockSpec((tm, k), lambda i: (i, 0)),
                  pl.BlockSpec((k, n), lambda i: (0, 0)),
                  pl.BlockSpec((tm, n), lambda i: (i, 0))],
        out_specs=pl.BlockSpec((tm, n), lambda i: (i, 0)),
        out_shape=jax.ShapeDtypeStruct((m, n), F32),
        compiler_params=_params("parallel"),
        name="matmul_res",
    )(a, w, res)


def _head_norm_rope(x, gain, cos, sin_signed, bd):
    hi, lo = _split2(x * x)
    ss = _dot(hi, bd) + _dot(lo, bd)
    y = x * lax.rsqrt(ss * (1.0 / HEAD_DIM) + RMS_EPS) * gain
    lane = lax.broadcasted_iota(jnp.int32, y.shape, 1)
    half = HEAD_DIM // 2
    width = y.shape[1]
    partner = jnp.where((lane & half) != 0, pltpu.roll(y, half, 1), pltpu.roll(y, width - half, 1))
    return y * cos + partner * sin_signed


def _post_kernel(x_ref, g_ref, w_ref, gain_ref, cos_ref, sin_ref, bd_ref, *rest, plan):
    out_refs, p_ref = rest[:-1], rest[-1]
    _norm_matmul_kernel(x_ref, g_ref, w_ref, p_ref, chunk=512)
    cos = cos_ref[...]
    sin = sin_ref[...]
    bd = bd_ref[...]
    for mode, src, width, dests in plan:
        x = p_ref[:, src:src + width]
        if mode == "rope":
            y = _head_norm_rope(x, gain_ref[:, src:src + width], cos, sin, bd)
        elif mode == "sigmoid":
            y = _sigmoid(x)
        else:
            y = x
        for oi, oc, transposed in dests:
            if transposed:
                out_refs[oi][0, oc:oc + width, :] = y.T
            else:
                out_refs[oi][:, oc:oc + width] = y


def proj_post(x, g, w, gain_row, cos, sin, bd, plan, outs, tm, b):
    m, k = x.shape
    n = w.shape[1]
    n_tab = cos.shape[0] // tm
    nt = m // b // tm
    out_specs, out_shape = [], []
    for width, transposed in outs:
        if transposed:
            out_specs.append(pl.BlockSpec((1, width, tm), lambda i: (i // nt, 0, i % nt)))
            out_shape.append(jax.ShapeDtypeStruct((b, width, m // b), F32))
        else:
            out_specs.append(pl.BlockSpec((tm, width), lambda i: (i, 0)))
            out_shape.append(jax.ShapeDtypeStruct((m, width), F32))
    return pl.pallas_call(
        functools.partial(_post_kernel, plan=plan),
        grid=(m // tm,),
        in_specs=[pl.BlockSpec((tm, k), lambda i: (i, 0)),
                  pl.BlockSpec((1, k), lambda i: (0, 0)),
                  pl.BlockSpec((k, n), lambda i: (0, 0)),
                  pl.BlockSpec(gain_row.shape, lambda i: (0, 0)),
                  pl.BlockSpec((tm, 256), lambda i: (i % n_tab, 0)),
                  pl.BlockSpec((tm, 256), lambda i: (i % n_tab, 0)),
                  pl.BlockSpec((256, 256), lambda i: (0, 0))],
        out_specs=out_specs,
        out_shape=out_shape,
        scratch_shapes=[pltpu.VMEM((tm, n), F32)],
        compiler_params=_params("parallel"),
        name="proj_post",
    )(x, g.reshape(1, k), w, gain_row, cos, sin, bd)


NSA_PLAN_P = tuple(
    [("rope", c * 256, 256, ((0, c * 256, False),)) for c in range(4)]
    + [("rope", 1024, 256, ((1, 0, False), (6, 0, True))),
       ("copy", 1280, 256, ((2, 0, False), (7, 0, True))),
       ("rope", 1536, 256, ((3, 0, False), (8, 0, True))),
       ("copy", 1792, 256, ((9, 0, True),)),
       ("rope", 2048, 256, ((4, 0, False), (10, 0, True))),
       ("copy", 2304, 256, ((11, 0, True),)),
       ("sigmoid", 2560, 128, ((5, 0, False),))])
NSA_OUTS_P = ((1024, False), (256, False), (256, False), (256, False), (256, False), (128, False)) + ((256, True),) * 6
NSA_PLAN_S = tuple(
    [("rope", c * 256, 256, ((0, c * 256, False),)) for c in range(4)]
    + [("rope", 1024, 256, ((1, 0, False),)), ("copy", 1280, 256, ((2, 0, False),)),
       ("rope", 1536, 256, ((3, 0, False),)), ("copy", 1792, 256, ((4, 0, False),)),
       ("rope", 2048, 256, ((5, 0, False), (8, 0, True))),
       ("copy", 2304, 256, ((6, 0, False), (9, 0, True))),
       ("sigmoid", 2560, 128, ((7, 0, False),))])
NSA_OUTS_S = ((1024, False),) + ((256, False),) * 6 + ((128, False), (256, True), (256, True))

DIFF_PLAN_P = tuple(
    [("rope", c * 256, 256, ((0, c * 256, False),)) for c in range(4)]
    + [("rope", 1024 + c * 256, 256, ((1, c * 256, False), (3, c * 256, True))) for c in range(4)]
    + [("copy", 2048 + c * 256, 256, ((2, c * 256, False), (4, c * 256, True))) for c in range(4)])
DIFF_OUTS_P = ((1024, False),) * 3 + ((1024, True),) * 2
DIFF_PLAN_S = tuple(
    [("rope", c * 256, 256, ((0, c * 256, False),)) for c in range(4)]
    + [("rope", 1024 + c * 256, 256, ((1, c * 256, False),)) for c in range(4)]
    + [("copy", 2048 + c * 256, 256, ((2, c * 256, False),)) for c in range(4)])
DIFF_OUTS_S = ((1024, False),) * 3


def _block_rows(piece):
    mats = [jnp.concatenate([piece(t, g) for t in range(CMP_BLOCK)], axis=1) for g in range(NSA_KV_HEADS)]
    return jnp.concatenate(mats, axis=0)


def _compress_rows_kernel(x_ref, pos_ref, w1_ref, w2_ref, w1t_ref, w2t_ref, o_ref, ot_ref):
    xr = x_ref[0] + pos_ref[...]
    xg = _block_rows(lambda t, g: xr[:, t * NSA_KV + g * HEAD_DIM:t * NSA_KV + (g + 1) * HEAD_DIM]).astype(BF16)
    h = _gelu_tanh(_dot(xg, w1_ref[...]))
    o_ref[0] = _dot(h.astype(BF16), w2_ref[...])
    ht = _gelu_tanh(_dot_nt(w1t_ref[...], xg))
    ot_ref[0] = _dot(w2t_ref[...], ht.astype(BF16))


def compress_rows(rows, pos_row, w1, w2, n_blocks):
    b = rows.shape[0]
    wide = CMP_BLOCK * NSA_KV
    nr = NSA_KV_HEADS * n_blocks
    w1t, w2t = w1.T, w2.T
    const = lambda a: pl.BlockSpec(a.shape, lambda i: (0, 0))
    return pl.pallas_call(
        _compress_rows_kernel,
        grid=(b,),
        in_specs=[pl.BlockSpec((1, n_blocks, wide), lambda i: (i, 0, 0)),
                  const(pos_row), const(w1), const(w2), const(w1t), const(w2t)],
        out_specs=[pl.BlockSpec((1, nr, HEAD_DIM), lambda i: (i, 0, 0)),
                   pl.BlockSpec((1, HEAD_DIM, nr), lambda i: (i, 0, 0))],
        out_shape=[jax.ShapeDtypeStruct((b, nr, HEAD_DIM), F32),
                   jax.ShapeDtypeStruct((b, HEAD_DIM, nr), F32)],
        compiler_params=_params("parallel"),
        name="compress_rows",
    )(rows, pos_row, w1, w2, w1t, w2t)


def _page_map(bi, pt, *, j, base):
    return (base + pt[bi, j], 0, 0)


def _page_map_u(i, pt, *, j, u, per_step, base):
    return (base + pt[i * per_step + u, j], 0, 0)


def _compress_pages_kernel(pt_ref, *refs, n_pages):
    del pt_ref
    x_refs = refs[:n_pages]
    pos_ref, w1_ref, w2_ref, perm_ref, o_ref, tok_ref = refs[n_pages:]
    halves = NSA_KV // LANE
    per_half = LANE // HEAD_DIM
    per_page = PAGE_SIZE // CMP_BLOCK
    perm = perm_ref[...]
    for jp in range(n_pages // 2):
        pair = jnp.concatenate([x_refs[2 * jp][0], x_refs[2 * jp + 1][0]], axis=1).astype(BF16)
        xt = _dot_nt(perm, pair).reshape(CMP_BLOCK, 2 * per_page, NSA_KV)
        for c in range(halves):
            tok_ref[c, :, jp * 2 * per_page:(jp + 1) * 2 * per_page, :] = xt[:, :, c * LANE:(c + 1) * LANE]
    ys = [[tok_ref[c, t] for c in range(halves)] for t in range(CMP_BLOCK)]
    xg = _block_rows(lambda t, g: ys[t][g // per_half][:, (g % per_half) * HEAD_DIM:(g % per_half + 1) * HEAD_DIM])
    xg = xg + pos_ref[...]
    h = _gelu_tanh(_dot(xg.astype(BF16), w1_ref[...]))
    o_ref[0] = _dot(h.astype(BF16), w2_ref[...])


def compress_pages(pool, page_table, base, pos_row, w1, w2):
    b, n_pages = page_table.shape
    assert n_pages % 2 == 0
    per_page = PAGE_SIZE // CMP_BLOCK
    n_blocks = n_pages * per_page
    r = jnp.arange(2 * PAGE_SIZE)
    src = ((r % (2 * per_page)) // per_page) * PAGE_SIZE + (r % per_page) * CMP_BLOCK + r // (2 * per_page)
    perm = (src[:, None] == jnp.arange(2 * PAGE_SIZE)[None, :]).astype(BF16)
    page_specs = [pl.BlockSpec((1, NSA_KV, PAGE_SIZE), functools.partial(_page_map, j=j, base=base))
                  for j in range(n_pages)]
    const = lambda a: pl.BlockSpec(a.shape, lambda i, pt: (0, 0))
    return pl.pallas_call(
        functools.partial(_compress_pages_kernel, n_pages=n_pages),
        grid_spec=pltpu.PrefetchScalarGridSpec(
            num_scalar_prefetch=1,
            grid=(b,),
            in_specs=page_specs + [const(pos_row), const(w1), const(w2), const(perm)],
            out_specs=pl.BlockSpec((1, NSA_KV_HEADS * n_blocks, HEAD_DIM), lambda i, pt: (i, 0, 0)),
            scratch_shapes=[pltpu.VMEM((NSA_KV // LANE, CMP_BLOCK, n_blocks, LANE), F32)]),
        out_shape=jax.ShapeDtypeStruct((b, NSA_KV_HEADS * n_blocks, HEAD_DIM), F32),
        compiler_params=_params("parallel"),
        name="compress_pages",
    )(page_table, *([pool] * n_pages), pos_row, w1, w2, perm)


def _select_blocks(imp_b, cur, n_sb, k_sel):
    lane = lax.broadcasted_iota(jnp.int32, imp_b.shape, 1)
    forced = (lane == cur) | (lane == 0)
    score = jnp.where(forced, FORCED_SCORE, imp_b)
    score = jnp.where(lane > cur, NEG_INF, score)
    score = jnp.where(lane >= n_sb, INVALID, score)
    rank = jnp.zeros(imp_b.shape, F32)
    for j in range(n_sb):
        rival = score[:, j:j + 1]
        rank = rank + jnp.where((rival > score) | ((rival == score) & (lane > j)), 1.0, 0.0)
    return jnp.where((rank < k_sel) & (lane < n_sb), 1.0, 0.0)


def _select_blocks_t(imp_b, cur, n_sb, k_sel):
    blk = lax.broadcasted_iota(jnp.int32, imp_b.shape, 0)
    forced = (blk == cur) | (blk == 0)
    score = jnp.where(forced, FORCED_SCORE, imp_b)
    score = jnp.where(blk > cur, NEG_INF, score)
    score = jnp.where(blk >= n_sb, INVALID, score)
    rank = jnp.zeros(imp_b.shape, F32)
    for j in range(n_sb):
        rival = score[j:j + 1, :]
        rank = rank + jnp.where((rival > score) | ((rival == score) & (blk > j)), 1.0, 0.0)
    return jnp.where((rank < k_sel) & (blk < n_sb), 1.0, 0.0)


def _masked_softmax(s, vis, axis, exp_fn=jnp.exp):
    sm = jnp.where(vis, s, NEG_INF)
    m = jnp.max(sm, axis=axis, keepdims=True)
    e = jnp.where(vis, exp_fn(sm - m), 0.0)
    den = jnp.sum(e, axis=axis, keepdims=True)
    return e / jnp.where(den > 0.0, den, 1.0)


def _masked_softmax_rows(s, vis):
    return _masked_softmax(s, vis, -1)


def _block_diag(blocks):
    n = len(blocks)
    r, c = blocks[0].shape
    rows = []
    for i, blk in enumerate(blocks):
        parts = ([jnp.zeros((r, i * c), blk.dtype)] if i else []) + [blk]
        parts += [jnp.zeros((r, (n - 1 - i) * c), blk.dtype)] if i < n - 1 else []
        rows.append(jnp.concatenate(parts, axis=1))
    return jnp.concatenate(rows, axis=0)


def _flash_init(dv, nq):
    return (jnp.full((1, nq), NEG_INF, F32), jnp.zeros((1, nq), F32), jnp.zeros((dv, nq), F32))


def _flash_step(k, vt, qt_bd, carry, bias, n_blocks):
    m, l, acc = carry
    s = _dot(k, qt_bd)
    if bias is not None:
        s = s + bias
    m_new = jnp.maximum(m, jnp.max(s, axis=0, keepdims=True))
    alpha = jnp.exp2(m - m_new)
    p = jnp.exp2(s - m_new)
    l = alpha * l + jnp.sum(p, axis=0, keepdims=True)
    pb = p.astype(BF16)
    if n_blocks > 1:
        dv = vt.shape[0] // n_blocks
        w = pb.shape[1] // n_blocks
        pv = jnp.concatenate([_dot(vt[g * dv:(g + 1) * dv, :], pb[:, g * w:(g + 1) * w])
                              for g in range(n_blocks)], axis=1)
    else:
        pv = _dot(vt, pb)
    return m_new, l, alpha * acc + pv


def _nsa_prompt_kernel(q_ref, gt_ref, kc_ref, vct_ref, ks_ref, vst_ref, kw_ref, vwt_ref,
                       poolt_ref, expandt_ref, o_ref, selb_ref, winb_ref, *, n_cmp, n_sb, k_sel):
    qi = pl.program_id(1)
    qb = Q_BLOCK
    rep = NSA_GROUP
    t = winb_ref.shape[0]
    qt = (q_ref[...] * (SCALE * LOG2E)).T.astype(BF16)
    gt = gt_ref[...].T
    pq = qi * qb + lax.broadcasted_iota(jnp.int32, (1, qb), 1)
    pq_rep = jnp.concatenate([pq] * rep, axis=1)
    kts = min(NSA_KEY_TILE, t)
    hi_kt = (qi * qb + qb + kts - 1) // kts
    lo_win = jnp.maximum(qi * qb - WINDOW, 0) // kts
    dist = pq - lax.broadcasted_iota(jnp.int32, (t, qb), 0)
    causal = dist >= 0
    winb_ref[...] = jnp.where(causal & (dist < WINDOW), 0.0, NEG_INF)
    poolt = poolt_ref[...]
    expandt = expandt_ref[...]
    qgs, o_cs = [], []
    for g in range(NSA_KV_HEADS):
        qg = jnp.concatenate(
            [qt[(rep * g + r) * HEAD_DIM:(rep * g + r + 1) * HEAD_DIM, :] for r in range(rep)], axis=1)

        kc = kc_ref[0, g * n_cmp:(g + 1) * n_cmp, :].astype(BF16)
        vct = vct_ref[0, :, g * n_cmp:(g + 1) * n_cmp].astype(BF16)
        s_c = _dot(kc, qg)
        nrow = lax.broadcasted_iota(jnp.int32, s_c.shape, 0)
        vis = ((nrow + 1) * CMP_BLOCK - 1) <= pq_rep
        p_c = _masked_softmax(s_c, vis, 0, jnp.exp2)
        o_cs.append(_dot(vct, p_c.astype(BF16)))
        imp = p_c[:, 0:qb]
        for r in range(1, rep):
            imp = imp + p_c[:, r * qb:(r + 1) * qb]
        hi_, lo_ = _split2(imp)
        imp_b = _dot(poolt, hi_) + _dot(poolt, lo_)
        sel = _select_blocks_t(imp_b, pq // SEL_BLOCK, n_sb, k_sel)
        picked = _dot(expandt, sel.astype(BF16)) > 0.5
        selb_ref[g] = jnp.where(picked & causal, 0.0, NEG_INF)
        qgs.append(qg)

    qt_bd = _block_diag(qgs)
    ng = NSA_KV_HEADS
    nq = ng * rep * qb

    def step(k_ref, vt_ref, bias_fn, kt, carry):
        off = pl.multiple_of(kt * kts, kts)
        k = k_ref[pl.ds(off, kts), :].astype(BF16)
        vt = vt_ref[0, :, pl.ds(off, kts)].astype(BF16)
        return _flash_step(k, vt, qt_bd, carry, bias_fn(off), ng)

    def sel_bias(off):
        return jnp.concatenate([selb_ref[g, pl.ds(off, kts), :] for g in range(ng) for _ in range(rep)], axis=1)

    def win_bias(off):
        return jnp.concatenate([winb_ref[pl.ds(off, kts), :]] * (ng * rep), axis=1)

    sel_step = functools.partial(step, ks_ref, vst_ref, sel_bias)
    win_step = functools.partial(step, kw_ref, vwt_ref, win_bias)
    carry_s = lax.fori_loop(0, lo_win, sel_step, _flash_init(HEAD_DIM, nq))
    carry_s, carry_w = lax.fori_loop(
        lo_win, hi_kt, lambda kt, c: (sel_step(kt, c[0]), win_step(kt, c[1])),
        (carry_s, _flash_init(HEAD_DIM, nq)))
    o_s = carry_s[2] / carry_s[1]
    o_w = carry_w[2] / carry_w[1]

    pieces = []
    for g in range(ng):
        for r in range(rep):
            h = rep * g + r
            lanes = slice(r * qb, (r + 1) * qb)
            wide = slice(h * qb, (h + 1) * qb)
            pieces.append(gt[3 * h:3 * h + 1, :] * o_cs[g][:, lanes]
                          + gt[3 * h + 1:3 * h + 2, :] * o_s[:, wide]
                          + gt[3 * h + 2:3 * h + 3, :] * o_w[:, wide])
    o_ref[...] = jnp.concatenate(pieces, axis=0).T


def nsa_prompt_attn(q, gates, kc, vct, ks, vst, kw, vwt, b, t):
    nq = t // Q_BLOCK
    n_cmp = t // CMP_BLOCK
    n_sb = -(-t // SEL_BLOCK)
    nsb_pad = -(-n_sb // SUBLANE) * SUBLANE
    k_sel = min(N_SEL, n_sb)
    ratio = SEL_BLOCK // CMP_BLOCK
    poolt = (jnp.arange(nsb_pad)[:, None] == jnp.arange(n_cmp)[None, :] // ratio).astype(BF16)
    expandt = (jnp.arange(t)[:, None] // SEL_BLOCK == jnp.arange(nsb_pad)[None, :]).astype(BF16)
    tok = lambda bi, qi: (bi * nq + qi, 0)
    seq = lambda bi, qi: (bi, 0)
    seq3 = lambda bi, qi: (bi, 0, 0)
    return pl.pallas_call(
        functools.partial(_nsa_prompt_kernel, n_cmp=n_cmp, n_sb=n_sb, k_sel=k_sel),
        grid=(b, nq),
        in_specs=[pl.BlockSpec((Q_BLOCK, NSA_HEADS * HEAD_DIM), tok),
                  pl.BlockSpec((Q_BLOCK, LANE), tok),
                  pl.BlockSpec((1, NSA_KV_HEADS * n_cmp, HEAD_DIM), seq3),
                  pl.BlockSpec((1, HEAD_DIM, NSA_KV_HEADS * n_cmp), seq3),
                  pl.BlockSpec((t, NSA_KV), seq), pl.BlockSpec((1, NSA_KV, t), seq3),
                  pl.BlockSpec((t, NSA_KV), seq), pl.BlockSpec((1, NSA_KV, t), seq3),
                  pl.BlockSpec(poolt.shape, lambda bi, qi: (0, 0)),
                  pl.BlockSpec(expandt.shape, lambda bi, qi: (0, 0))],
        out_specs=pl.BlockSpec((Q_BLOCK, NSA_HEADS * HEAD_DIM), tok),
        out_shape=jax.ShapeDtypeStruct((b * t, NSA_HEADS * HEAD_DIM), F32),
        scratch_shapes=[pltpu.VMEM((NSA_KV_HEADS, t, Q_BLOCK), F32), pltpu.VMEM((t, Q_BLOCK), F32)],
        compiler_params=_params("parallel", "arbitrary"),
        name="nsa_prompt_attn",
    )(q, gates, kc, vct, ks, vst, kw, vwt, poolt, expandt)


def _fold_groups(x, rowg, width):
    out = jnp.where(rowg == 0, x[:, 0:width], 0.0)
    for g in range(1, NSA_KV_HEADS):
        out = out + jnp.where(rowg == g, x[:, g * width:(g + 1) * width], 0.0)
    return out


def _place_groups(x, rowg):
    return jnp.concatenate([jnp.where(rowg == g, x, 0.0) for g in range(NSA_KV_HEADS)], axis=1)


def _nsa_sample_kernel(pt_ref, *refs, n_pages, per_step, past, buf_len, n_cmp, n_sb, k_sel):
    del pt_ref
    for u in range(per_step):
        _nsa_sample_one(u, refs[u * n_pages:(u + 1) * n_pages],
                        refs[(per_step + u) * n_pages:(per_step + u + 1) * n_pages], refs[2 * per_step * n_pages:],
                        pl.program_id(0) * per_step + u, n_pages, past, buf_len, n_cmp, n_sb, k_sel)


def _nsa_sample_one(u, ksp, vsp, refs, bi, n_pages, past, buf_len, n_cmp, n_sb, k_sel):
    (q_ref, gt_ref, kc_ref, vc_ref, ksn_ref, vsn_ref, kwn_ref, vwn_ref, kwnt_ref, vwnt_ref, wk_ref, wv_ref,
     pool_ref, expand_ref, gsel_ref, gselt_ref, o_ref, wko_ref, wvo_ref) = refs
    nh = NSA_HEADS
    rowg = lax.broadcasted_iota(jnp.int32, (nh, 1), 0) // NSA_GROUP
    q16 = q_ref[u] * SCALE
    qmat_f = _place_groups(q16, rowg)
    qmat = qmat_f.astype(BF16)

    s_all = _dot_nt(q16.astype(BF16), kc_ref[u].astype(BF16))
    s_c = _fold_groups(s_all, rowg, n_cmp)
    ncol = lax.broadcasted_iota(jnp.int32, s_c.shape, 1)
    vis = ((ncol + 1) * CMP_BLOCK - 1) <= past
    p_c = _masked_softmax_rows(s_c, vis)
    o_c = _dot(_place_groups(p_c, rowg).astype(BF16), vc_ref[u].astype(BF16))

    gsel = gsel_ref[...]
    hi_, lo_ = _split2(p_c)
    imp = _dot(gsel, hi_) + _dot(gsel, lo_)
    hi_, lo_ = _split2(imp)
    imp_b = _dot(hi_, pool_ref[...]) + _dot(lo_, pool_ref[...])
    sel = _select_blocks(imp_b, past // SEL_BLOCK, n_sb, k_sel)
    sel16 = _dot(gselt_ref[...], sel.astype(BF16))
    maskfull = _dot(sel16.astype(BF16), expand_ref[...])

    def attend(scores, valids, s_new, valid_new, values, v_new):
        m = s_new if valid_new is None else jnp.where(valid_new > 0.5, s_new, NEG_INF)
        for s, vd in zip(scores, valids):
            m = jnp.maximum(m, jnp.max(jnp.where(vd > 0.5, s, NEG_INF), axis=-1, keepdims=True))
        p_new = jnp.exp(s_new - m)
        if valid_new is not None:
            p_new = p_new * valid_new
        l = p_new
        acc = p_new * v_new
        for s, vd, v in zip(scores, valids, values):
            p = jnp.exp(jnp.where(vd > 0.5, s, NEG_INF) - m) * vd
            l = l + jnp.sum(p, axis=-1, keepdims=True)
            acc = acc + _dot_nt(p.astype(BF16), v)
        return acc / l

    kt_all = jnp.concatenate([ksp[j][0].astype(BF16) for j in range(n_pages)], axis=1)
    vt_all = jnp.concatenate([vsp[j][0].astype(BF16) for j in range(n_pages)], axis=1)
    s_new = jnp.sum(qmat_f * ksn_ref[u], axis=-1, keepdims=True)
    o_s = attend([_dot(qmat, kt_all)], [maskfull[:, 0:past]], s_new, maskfull[:, past:past + 1],
                 [vt_all], vsn_ref[u])
    o_s = _fold_groups(o_s, rowg, HEAD_DIM)

    wk = wk_ref[u]
    wv = wv_ref[u]
    s_w = _dot(qmat, wk.astype(BF16))
    wcol = lax.broadcasted_iota(jnp.int32, s_w.shape, 1)
    pos_w = past - buf_len + wcol
    valid_w = jnp.where((past - pos_w < WINDOW) & (pos_w >= 0), 1.0, 0.0)
    s_new = jnp.sum(qmat_f * kwn_ref[u], axis=-1, keepdims=True)
    o_w = attend([s_w], [valid_w], s_new, None, [wv.astype(BF16)], vwn_ref[u])
    o_w = _fold_groups(o_w, rowg, HEAD_DIM)

    gt = gt_ref[u]
    o_ref[u] = gt[:, 0:1] * o_c + gt[:, 1:2] * o_s + gt[:, 2:3] * o_w

    bsel = lax.broadcasted_iota(jnp.int32, kwnt_ref.shape[1:], 1) == bi
    k_col = jnp.sum(jnp.where(bsel, kwnt_ref[0], 0.0), axis=-1, keepdims=True)
    v_col = jnp.sum(jnp.where(bsel, vwnt_ref[0], 0.0), axis=-1, keepdims=True)
    wlane = lax.broadcasted_iota(jnp.int32, (1, buf_len), 1)
    wko_ref[u] = jnp.where(wlane == buf_len - 1, k_col, pltpu.roll(wk, buf_len - 1, 1))
    wvo_ref[u] = jnp.where(wlane == buf_len - 1, v_col, pltpu.roll(wv, buf_len - 1, 1))


def nsa_sample_attn(page_table, pool_k, pool_v, base, q16, gates, kc, vc, ks_new, vs_new,
                    kw_new, vw_new, kwt_new, vwt_new, win_k, win_v, win_base):
    b, n_pages = page_table.shape
    past = n_pages * PAGE_SIZE
    buf_len = win_k.shape[2]
    tk = past + 1
    n_cmp = tk // CMP_BLOCK
    n_sb = -(-tk // SEL_BLOCK)
    k_sel = min(N_SEL, n_sb)
    ratio = SEL_BLOCK // CMP_BLOCK
    pool = (jnp.arange(n_cmp)[:, None] // ratio == jnp.arange(LANE)[None, :]).astype(BF16)
    expand = (jnp.arange(LANE)[:, None] == jnp.arange(past + LANE)[None, :] // SEL_BLOCK).astype(BF16)
    gsel = (jnp.arange(SUBLANE)[:, None] == jnp.arange(NSA_HEADS)[None, :] // NSA_GROUP).astype(BF16)
    gselt = gsel.T
    per_step = NSA_SAMPLE_PER_STEP if b % NSA_SAMPLE_PER_STEP == 0 else 1
    assert win_base % per_step == 0
    pages = lambda: [pl.BlockSpec((1, NSA_KV, PAGE_SIZE),
                                  functools.partial(_page_map_u, j=j, u=u, per_step=per_step, base=base))
                     for u in range(per_step) for j in range(n_pages)]
    per_b = lambda shape: pl.BlockSpec((per_step,) + shape, lambda i, pt: (i, 0, 0))
    const = lambda a: pl.BlockSpec(a.shape, lambda i, pt: (0,) * a.ndim)
    win = pl.BlockSpec((per_step, NSA_KV, buf_len), lambda i, pt: (win_base // per_step + i, 0, 0))
    return pl.pallas_call(
        functools.partial(_nsa_sample_kernel, n_pages=n_pages, per_step=per_step, past=past, buf_len=buf_len,
                          n_cmp=n_cmp, n_sb=n_sb, k_sel=k_sel),
        grid_spec=pltpu.PrefetchScalarGridSpec(
            num_scalar_prefetch=1,
            grid=(b // per_step,),
            in_specs=(pages() + pages()
                      + [per_b((NSA_HEADS, HEAD_DIM)), per_b((NSA_HEADS, 3)),
                         per_b((NSA_KV_HEADS * n_cmp, HEAD_DIM)), per_b((NSA_KV_HEADS * n_cmp, HEAD_DIM)),
                         per_b((1, NSA_KV)), per_b((1, NSA_KV)), per_b((1, NSA_KV)), per_b((1, NSA_KV)),
                         const(kwt_new), const(vwt_new),
                         win, win, const(pool), const(expand), const(gsel), const(gselt)]),
            out_specs=[per_b((NSA_HEADS, HEAD_DIM)), per_b((NSA_KV, buf_len)), per_b((NSA_KV, buf_len))]),
        out_shape=[jax.ShapeDtypeStruct((b, NSA_HEADS, HEAD_DIM), F32),
                   jax.ShapeDtypeStruct((b, NSA_KV, buf_len), F32),
                   jax.ShapeDtypeStruct((b, NSA_KV, buf_len), F32)],
        compiler_params=_params("parallel"),
        name="nsa_sample_attn",
    )(page_table, *([pool_k] * (per_step * n_pages)), *([pool_v] * (per_step * n_pages)), q16, gates, kc, vc,
      ks_new, vs_new, kw_new, vw_new, kwt_new, vwt_new, win_k, win_v, pool, expand, gsel, gselt)


def _diff_lambda(lam_ref, lam_init):
    lv = lam_ref[...]
    a = jnp.sum(lv[0:1] * lv[1:2], axis=-1, keepdims=True)
    c = jnp.sum(lv[2:3] * lv[3:4], axis=-1, keepdims=True)
    return jnp.exp(a) - jnp.exp(c) + lam_init


def _diff_prompt_kernel(q_ref, k_ref, vt_ref, lam_ref, sub_ref, o_ref, *, lam_init):
    si = pl.program_id(1)
    qs = q_ref.shape[0]
    qt = (q_ref[...] * (SCALE * LOG2E)).T.astype(BF16)
    kts = qs
    lane = lax.broadcasted_iota(jnp.int32, (kts, qs), 1)
    krow = lax.broadcasted_iota(jnp.int32, (kts, qs), 0)
    diag_bias = jnp.where(krow <= lane, 0.0, NEG_INF)
    diag_bias = jnp.concatenate([diag_bias] * 2, axis=1)
    lam = _diff_lambda(lam_ref, lam_init)
    vw = 2 * HEAD_DIM

    pieces = []
    for h in range(DIFF_HEADS):
        vrows = slice(h * vw, (h + 1) * vw)
        kcols = slice(2 * h * HEAD_DIM, (2 * h + 2) * HEAD_DIM)
        qt_bd = _block_diag([qt[(2 * h + c) * HEAD_DIM:(2 * h + c + 1) * HEAD_DIM, :] for c in range(2)])

        def tile(kt, carry, bias):
            off = pl.multiple_of(kt * kts, kts)
            k = k_ref[pl.ds(off, kts), kcols].astype(BF16)
            vt = vt_ref[0, vrows, pl.ds(off, kts)].astype(BF16)
            return _flash_step(k, vt, qt_bd, carry, bias, 1)

        carry = lax.fori_loop(0, si, lambda kt, c: tile(kt, c, None), _flash_init(vw, 2 * qs))
        _, l, acc = tile(si, carry, diag_bias)
        outs = acc / l
        o = outs[:, 0:qs] - lam * outs[:, qs:2 * qs]
        o = o * lax.rsqrt(jnp.mean(o * o, axis=0, keepdims=True) + RMS_EPS) * sub_ref[...]
        pieces.append(o * (1.0 - lam_init))
    o_ref[...] = jnp.concatenate(pieces, axis=0).T


def diff_prompt_attn(q, k, vt, lam_vec, sub_norm, lam_init, b, t):
    qs = min(DIFF_Q_BLOCK, t)
    nq = t // qs
    tok = lambda bi, qi: (bi * nq + qi, 0)
    return pl.pallas_call(
        functools.partial(_diff_prompt_kernel, lam_init=lam_init),
        grid=(b, nq),
        in_specs=[pl.BlockSpec((qs, D_MODEL), tok),
                  pl.BlockSpec((t, D_MODEL), lambda bi, qi: (bi, 0)),
                  pl.BlockSpec((1, D_MODEL, t), lambda bi, qi: (bi, 0, 0)),
                  pl.BlockSpec(lam_vec.shape, lambda bi, qi: (0, 0)),
                  pl.BlockSpec((2 * HEAD_DIM, 1), lambda bi, qi: (0, 0))],
        out_specs=pl.BlockSpec((qs, D_MODEL), tok),
        out_shape=jax.ShapeDtypeStruct((b * t, D_MODEL), F32),
        compiler_params=_params("parallel", "arbitrary"),
        name="diff_prompt_attn",
    )(q, k, vt, lam_vec, sub_norm.reshape(2 * HEAD_DIM, 1))


def _diff_sample_kernel(pt_ref, *refs, n_pages, lam_init):
    del pt_ref
    kp = refs[:n_pages]
    vp = refs[n_pages:2 * n_pages]
    q_ref, kn_ref, vn_ref, lam_ref, sub_ref, expm_ref, hmask_ref, o_ref = refs[2 * n_pages:]
    nmap = 2 * DIFF_HEADS
    lam = _diff_lambda(lam_ref, lam_init)
    r = lax.broadcasted_iota(jnp.int32, (nmap, D_MODEL), 0)
    col = lax.broadcasted_iota(jnp.int32, (nmap, D_MODEL), 1)
    own = jnp.where(r < DIFF_HEADS, 2 * r, 2 * (r - DIFF_HEADS) + 1)
    qmat_f = jnp.where(col // HEAD_DIM == own, q_ref[0] * SCALE, 0.0)
    qmat = qmat_f.astype(BF16)
    scores = [_dot(qmat, kp[j][0].astype(BF16)) for j in range(n_pages)]
    s_new = jnp.sum(qmat_f * kn_ref[0], axis=-1, keepdims=True)
    m = s_new
    for s in scores:
        m = jnp.maximum(m, jnp.max(s, axis=-1, keepdims=True))
    p_new = jnp.exp(s_new - m)
    ps = [jnp.exp(s - m) for s in scores]
    l = p_new
    for p in ps:
        l = l + jnp.sum(p, axis=-1, keepdims=True)
    inv = 1.0 / l
    pd_new = (p_new * inv)[0:DIFF_HEADS] - lam * (p_new * inv)[DIFF_HEADS:nmap]
    pds = []
    for j in range(n_pages):
        pn = ps[j] * inv
        pds.append((pn[0:DIFF_HEADS] - lam * pn[DIFF_HEADS:nmap]).astype(BF16))
    expm = expm_ref[...]
    hmask = hmask_ref[...]
    acc = pd_new * vn_ref[0]
    for j in range(n_pages):
        spread = (_dot(pds[j], expm) * hmask).astype(BF16)
        acc = acc + _dot(spread, vp[j][0].astype(BF16))
    o_ref[0] = _rms_rows(acc, sub_ref[...]) * (1.0 - lam_init)


def diff_sample_attn(page_table, pool_k, pool_v, base, q, k_new, v_new, lam_vec, sub_row, lam_init):
    b, n_pages = page_table.shape
    vw = 2 * HEAD_DIM
    rows = PAGE_SIZE * DIFF_HEADS
    expm = (jnp.arange(PAGE_SIZE)[:, None] == jnp.arange(rows)[None, :] // DIFF_HEADS).astype(BF16)
    hmask = (jnp.arange(DIFF_HEADS)[:, None] == jnp.arange(rows)[None, :] % DIFF_HEADS).astype(F32)
    kpage = lambda j: pl.BlockSpec((1, D_MODEL, PAGE_SIZE), functools.partial(_page_map, j=j, base=base))
    vpage = lambda j: pl.BlockSpec((1, rows, vw), functools.partial(_page_map, j=j, base=base))
    per_b = pl.BlockSpec((1, 1, D_MODEL), lambda i, pt: (i, 0, 0))
    per_bh = pl.BlockSpec((1, DIFF_HEADS, vw), lambda i, pt: (i, 0, 0))
    const = lambda a: pl.BlockSpec(a.shape, lambda i, pt: (0, 0))
    return pl.pallas_call(
        functools.partial(_diff_sample_kernel, n_pages=n_pages, lam_init=lam_init),
        grid_spec=pltpu.PrefetchScalarGridSpec(
            num_scalar_prefetch=1,
            grid=(b,),
            in_specs=([kpage(j) for j in range(n_pages)] + [vpage(j) for j in range(n_pages)]
                      + [per_b, per_b, per_bh, const(lam_vec), const(sub_row), const(expm), const(hmask)]),
            out_specs=per_bh),
        out_shape=jax.ShapeDtypeStruct((b, DIFF_HEADS, vw), F32),
        compiler_params=_params("parallel"),
        name="diff_sample_attn",
    )(page_table, *([pool_k] * n_pages), *([pool_v] * n_pages), q, k_new, v_new, lam_vec, sub_row, expm, hmask)


def _shift_rows(x, s, carry, row):
    r = pltpu.roll(x, s, 0)
    top = r[0:SUBLANE]
    for i in range(s):
        top = jnp.where(row[0:SUBLANE] == i, carry[SUBLANE - s + i:SUBLANE - s + i + 1, :], top)
    return jnp.concatenate([top, r[SUBLANE:]], axis=0)


def _conv_silu_seq_kernel(x_ref, w_ref, b_ref, o_ref, carry_ref, *, width):
    @pl.when(pl.program_id(2) == 0)
    def _():
        carry_ref[...] = jnp.zeros(carry_ref.shape, F32)

    x = x_ref[...]
    tm = x.shape[0]
    row = lax.broadcasted_iota(jnp.int32, (tm, 1), 0)
    carry = carry_ref[...]
    acc = x * w_ref[width - 1:width, :]
    for s in range(1, width):
        acc = acc + _shift_rows(x, s, carry, row) * w_ref[width - 1 - s:width - s, :]
    acc = acc + b_ref[...]
    o_ref[...] = _silu(acc)
    carry_ref[...] = x[tm - SUBLANE:tm, :]


def ssd_conv_prompt(proj, conv_w, conv_b, b, t, tm):
    cb = 1024
    nt = t // tm
    c0 = SSD_D_INNER // cb
    return pl.pallas_call(
        functools.partial(_conv_silu_seq_kernel, width=SSD_CONV_W),
        grid=(b, SSD_CONV_DIM // cb, nt),
        in_specs=[pl.BlockSpec((tm, cb), lambda bi, j, ti: (bi * nt + ti, c0 + j)),
                  pl.BlockSpec((SSD_CONV_W, cb), lambda bi, j, ti: (0, j)),
                  pl.BlockSpec((1, cb), lambda bi, j, ti: (0, j))],
        out_specs=pl.BlockSpec((tm, cb), lambda bi, j, ti: (bi * nt + ti, j)),
        out_shape=jax.ShapeDtypeStruct((b * t, SSD_CONV_DIM), F32),
        scratch_shapes=[pltpu.VMEM((SUBLANE, cb), F32)],
        compiler_params=_params("parallel", "parallel", "arbitrary"),
        name="ssd_conv_prompt",
    )(proj, conv_w, conv_b.reshape(1, SSD_CONV_DIM))


def _conv_silu_state_kernel(x_ref, p0_ref, p1_ref, p2_ref, w_ref, b_ref, o_ref):
    acc = (p0_ref[...] * w_ref[0:1, :] + p1_ref[...] * w_ref[1:2, :] + p2_ref[...] * w_ref[2:3, :]
           + x_ref[...] * w_ref[3:4, :] + b_ref[...])
    o_ref[...] = _silu(acc)


def ssd_conv_sample(proj, prev, conv_w, conv_b):
    b = proj.shape[0]
    cb = 1024
    c0 = SSD_D_INNER // cb
    col = lambda j: (0, j)
    return pl.pallas_call(
        _conv_silu_state_kernel,
        grid=(SSD_CONV_DIM // cb,),
        in_specs=[pl.BlockSpec((b, cb), lambda j: (0, c0 + j)),
                  pl.BlockSpec((b, cb), col), pl.BlockSpec((b, cb), col), pl.BlockSpec((b, cb), col),
                  pl.BlockSpec((SSD_CONV_W, cb), col), pl.BlockSpec((1, cb), col)],
        out_specs=pl.BlockSpec((b, cb), col),
        out_shape=jax.ShapeDtypeStruct((b, SSD_CONV_DIM), F32),
        compiler_params=_params("parallel"),
        name="ssd_conv_sample",
    )(proj, prev[0], prev[1], prev[2], conv_w, conv_b.reshape(1, SSD_CONV_DIM))


def _ssd_scan_kernel(xbc_ref, dt_ref, bias_ref, alog_ref, dskip_ref, tril_ref, exp_ref,
                     y_ref, st_ref, state_ref):
    @pl.when(pl.program_id(1) == 0)
    def _():
        state_ref[...] = jnp.zeros(state_ref.shape, F32)

    l = SSD_CHUNK
    di = SSD_D_INNER
    n = SSD_D_STATE
    tril = tril_ref[...]
    expm = exp_ref[...]
    dt = _softplus(dt_ref[...] + bias_ref[...])
    a = dt * (-jnp.exp(alog_ref[...]))
    acs = _lhs_exact_dot(tril, a)
    acs_t = acs.T
    dtx = _dot_exact_rhs(dt, expm)
    eacs = jnp.exp(acs)
    eacsx = _dot_exact_rhs(eacs, expm)
    decx = _dot_exact_rhs(jnp.exp(acs[l - 1:l, :] - acs), expm)
    x = xbc_ref[:, 0:di]
    xdt = x * dtx
    xw = (xdt * decx).astype(BF16)
    xdt_b = xdt.astype(BF16)
    ri = lax.broadcasted_iota(jnp.int32, (l, l), 0)
    ci = lax.broadcasted_iota(jnp.int32, (l, l), 1)
    lower = ri >= ci
    gw = SSD_GROUP_W
    for g in range(SSD_GROUPS):
        bg = xbc_ref[:, di + g * n:di + (g + 1) * n]
        cg = xbc_ref[:, di + SSD_GROUPS * n + g * n:di + SSD_GROUPS * n + (g + 1) * n].astype(BF16)
        cb = _dot_nt(cg, bg.astype(BF16))
        st_g = state_ref[:, g * gw:(g + 1) * gw]
        y_off = _dot(cg, st_g.astype(BF16)) * eacsx[:, g * gw:(g + 1) * gw]
        state_ref[:, g * gw:(g + 1) * gw] = (
            st_g * eacsx[l - 1:l, g * gw:(g + 1) * gw] + _dot(bg.T.astype(BF16), xw[:, g * gw:(g + 1) * gw]))
        for hh in range(SSD_HEADS_PER_GROUP):
            h = g * SSD_HEADS_PER_GROUP + hh
            cols = slice(h * SSD_HEADDIM, (h + 1) * SSD_HEADDIM)
            seg = acs[:, h:h + 1] - acs_t[h:h + 1, :]
            lmat = jnp.exp(jnp.where(lower, seg, NEG_INF))
            yd = _dot((cb * lmat).astype(BF16), xdt_b[:, cols])
            y_ref[:, cols] = (yd + y_off[:, hh * SSD_HEADDIM:(hh + 1) * SSD_HEADDIM]
                              + dskip_ref[:, cols] * x[:, cols])
    st_ref[0] = state_ref[...]


def ssd_scan_prompt(xbc, proj, dt_bias_row, a_log_row, dskip_row, b, t):
    nc = t // SSD_CHUNK
    l = SSD_CHUNK
    tril = (jnp.arange(l)[:, None] >= jnp.arange(l)[None, :]).astype(BF16)
    expm = (jnp.arange(LANE)[:, None] == jnp.arange(SSD_D_INNER)[None, :] // SSD_HEADDIM).astype(BF16)
    dt_blk = (SSD_D_INNER + SSD_CONV_DIM) // LANE
    tok = lambda bi, ci: (bi * nc + ci, 0)
    const = lambda a: pl.BlockSpec(a.shape, lambda bi, ci: (0, 0))
    return pl.pallas_call(
        _ssd_scan_kernel,
        grid=(b, nc),
        in_specs=[pl.BlockSpec((l, SSD_CONV_DIM), tok),
                  pl.BlockSpec((l, LANE), lambda bi, ci: (bi * nc + ci, dt_blk)),
                  const(dt_bias_row), const(a_log_row), const(dskip_row), const(tril), const(expm)],
        out_specs=[pl.BlockSpec((l, SSD_D_INNER), tok),
                   pl.BlockSpec((1, SSD_D_STATE, SSD_D_INNER), lambda bi, ci: (bi, 0, 0))],
        out_shape=[jax.ShapeDtypeStruct((b * t, SSD_D_INNER), F32),
                   jax.ShapeDtypeStruct((b, SSD_D_STATE, SSD_D_INNER), F32)],
        scratch_shapes=[pltpu.VMEM((SSD_D_STATE, SSD_D_INNER), F32)],
        compiler_params=_params("parallel", "arbitrary"),
        name="ssd_scan_prompt",
    )(xbc, proj, dt_bias_row, a_log_row, dskip_row, tril, expm)


def _ssd_step_kernel(xbc_ref, dt_ref, bias_ref, alog_ref, dskip_ref, s_ref, y_ref, so_ref):
    di = SSD_D_INNER
    n = SSD_D_STATE
    p = SSD_HEADDIM
    dt = _softplus(dt_ref[0] + bias_ref[...])
    dec = jnp.exp(dt * (-jnp.exp(alog_ref[...])))
    ri = lax.broadcasted_iota(jnp.int32, (p, p), 0)
    ci = lax.broadcasted_iota(jnp.int32, (p, p), 1)
    eye = ri == ci
    for g in range(SSD_GROUPS):
        bmat = jnp.broadcast_to(xbc_ref[0, :, di + g * n:di + (g + 1) * n], (p, n)).astype(BF16)
        c8 = jnp.broadcast_to(xbc_ref[0, :, di + SSD_GROUPS * n + g * n:di + SSD_GROUPS * n + (g + 1) * n],
                              (SUBLANE, n)).astype(BF16)
        for hh in range(SSD_HEADS_PER_GROUP):
            h = g * SSD_HEADS_PER_GROUP + hh
            cols = slice(h * p, (h + 1) * p)
            xh = xbc_ref[0, :, cols]
            xdt = xh * dt[:, h:h + 1]
            diag = jnp.where(eye, jnp.broadcast_to(xdt, (p, p)), 0.0)
            hi_, lo_ = _split2(diag)
            s_new = s_ref[0, h] * dec[:, h:h + 1] + _dot(hi_, bmat) + _dot(lo_, bmat)
            so_ref[0, h] = s_new
            yh = _dot_nt(c8, s_new.astype(BF16))
            y_ref[0, :, cols] = yh[0:1] + dskip_ref[:, cols] * xh


def ssd_step_sample(xbc, proj, dt_bias_row, a_log_row, dskip_row, state, state_base):
    b = xbc.shape[0]
    dt_blk = (SSD_D_INNER + SSD_CONV_DIM) // LANE
    const = lambda a: pl.BlockSpec(a.shape, lambda i: (0, 0))
    st_shape = (1, SSD_HEADS, SSD_HEADDIM, SSD_D_STATE)
    return pl.pallas_call(
        _ssd_step_kernel,
        grid=(b,),
        in_specs=[pl.BlockSpec((1, 1, SSD_CONV_DIM), lambda i: (i, 0, 0)),
                  pl.BlockSpec((1, 1, LANE), lambda i: (i, 0, dt_blk)),
                  const(dt_bias_row), const(a_log_row), const(dskip_row),
                  pl.BlockSpec(st_shape, lambda i: (state_base + i, 0, 0, 0))],
        out_specs=[pl.BlockSpec((1, 1, SSD_D_INNER), lambda i: (i, 0, 0)),
                   pl.BlockSpec(st_shape, lambda i: (i, 0, 0, 0))],
        out_shape=[jax.ShapeDtypeStruct((b, 1, SSD_D_INNER), F32),
                   jax.ShapeDtypeStruct((b,) + st_shape[1:], F32)],
        compiler_params=_params("parallel"),
        name="ssd_step_sample",
    )(xbc.reshape(b, 1, SSD_CONV_DIM), proj.reshape(b, 1, proj.shape[1]),
      dt_bias_row, a_log_row, dskip_row, state)


def _ssd_out_kernel(y_ref, z_ref, ng_ref, w_ref, r_ref, o_ref):
    gated = y_ref[...] * _silu(z_ref[...])
    parts = []
    for g in range(SSD_GROUPS):
        cols = slice(g * SSD_GROUP_W, (g + 1) * SSD_GROUP_W)
        parts.append(_rms_rows(gated[:, cols], ng_ref[:, cols]).astype(BF16))
    o_ref[...] = r_ref[...] + _dot(jnp.concatenate(parts, axis=1), w_ref[...])


def ssd_out(y, proj, norm_g, w, res, tm):
    m = y.shape[0]
    di = SSD_D_INNER
    return pl.pallas_call(
        _ssd_out_kernel,
        grid=(m // tm,),
        in_specs=[pl.BlockSpec((tm, di), lambda i: (i, 0)),
                  pl.BlockSpec((tm, di), lambda i: (i, 0)),
                  pl.BlockSpec((1, di), lambda i: (0, 0)),
                  pl.BlockSpec((di, D_MODEL), lambda i: (0, 0)),
                  pl.BlockSpec((tm, D_MODEL), lambda i: (i, 0))],
        out_specs=pl.BlockSpec((tm, D_MODEL), lambda i: (i, 0)),
        out_shape=jax.ShapeDtypeStruct((m, D_MODEL), F32),
        compiler_params=_params("parallel"),
        name="ssd_out",
    )(y, proj, norm_g.reshape(1, di), w, res)


def _ffn_seq_kernel(x_ref, g_ref, wup_ref, cw_ref, wdn_ref, o_ref, st_ref, carry_ref):
    @pl.when(pl.program_id(1) == 0)
    def _():
        carry_ref[...] = jnp.zeros(carry_ref.shape, F32)

    x = x_ref[...]
    tm = x.shape[0]
    h = _rms_rows(x, g_ref[...]).astype(BF16)
    row = lax.broadcasted_iota(jnp.int32, (tm, 1), 0)
    fc = FFN_CHUNK

    def conv(u, c0):
        w = cw_ref[:, c0:c0 + fc]
        carry = carry_ref[:, c0:c0 + fc]
        y = u * w[2:3] + _shift_rows(u, 1, carry, row) * w[1:2] + _shift_rows(u, 2, carry, row) * w[0:1]
        carry_ref[:, c0:c0 + fc] = u[tm - SUBLANE:tm, :]
        return y

    acc = jnp.zeros((tm, D_MODEL), F32)
    for c0 in range(0, D_FF, fc):
        u = conv(_dot(h, wup_ref[:, c0:c0 + fc]), c0)
        gate = conv(_dot(h, wup_ref[:, D_FF + c0:D_FF + c0 + fc]), D_FF + c0)
        acc = acc + _dot((_silu(gate) * u).astype(BF16), wdn_ref[c0:c0 + fc, :])
    o_ref[...] = x + acc
    st_ref[0] = carry_ref[...]


def ffn_prompt(x, g, w_up, conv_w, w_down, b, t, tm):
    nt = t // tm
    tok = lambda bi, ti: (bi * nt + ti, 0)
    const = lambda a: pl.BlockSpec(a.shape, lambda bi, ti: (0, 0))
    g = g.reshape(1, D_MODEL)
    return pl.pallas_call(
        _ffn_seq_kernel,
        grid=(b, nt),
        in_specs=[pl.BlockSpec((tm, D_MODEL), tok), const(g), const(w_up), const(conv_w), const(w_down)],
        out_specs=[pl.BlockSpec((tm, D_MODEL), tok),
                   pl.BlockSpec((1, SUBLANE, 2 * D_FF), lambda bi, ti: (bi, 0, 0))],
        out_shape=[jax.ShapeDtypeStruct((b * t, D_MODEL), F32),
                   jax.ShapeDtypeStruct((b, SUBLANE, 2 * D_FF), F32)],
        scratch_shapes=[pltpu.VMEM((SUBLANE, 2 * D_FF), F32)],
        compiler_params=_params("parallel", "arbitrary"),
        name="ffn_prompt",
    )(x, g, w_up, conv_w, w_down)


def _ffn_state_kernel(x_ref, g_ref, wup_ref, cw_ref, wdn_ref, p0_ref, p1_ref, o_ref, up_ref):
    x = x_ref[...]
    h = _rms_rows(x, g_ref[...]).astype(BF16)
    fc = FFN_CHUNK

    def conv(u, c0):
        up_ref[:, c0:c0 + fc] = u
        w = cw_ref[:, c0:c0 + fc]
        return u * w[2:3] + p1_ref[:, c0:c0 + fc] * w[1:2] + p0_ref[:, c0:c0 + fc] * w[0:1]

    acc = jnp.zeros(x.shape, F32)
    for c0 in range(0, D_FF, fc):
        u = conv(_dot(h, wup_ref[:, c0:c0 + fc]), c0)
        gate = conv(_dot(h, wup_ref[:, D_FF + c0:D_FF + c0 + fc]), D_FF + c0)
        acc = acc + _dot((_silu(gate) * u).astype(BF16), wdn_ref[c0:c0 + fc, :])
    o_ref[...] = x + acc


def ffn_sample(x, g, w_up, conv_w, w_down, prev):
    b = x.shape[0]
    full = lambda a: pl.BlockSpec(a.shape, lambda i: (0,) * a.ndim)
    g = g.reshape(1, D_MODEL)
    p0, p1 = prev[:, 0], prev[:, 1]
    return pl.pallas_call(
        _ffn_state_kernel,
        grid=(1,),
        in_specs=[full(x), full(g), full(w_up), full(conv_w), full(w_down), full(p0), full(p1)],
        out_specs=[pl.BlockSpec((b, D_MODEL), lambda i: (0, 0)),
                   pl.BlockSpec((b, 2 * D_FF), lambda i: (0, 0))],
        out_shape=[jax.ShapeDtypeStruct((b, D_MODEL), F32),
                   jax.ShapeDtypeStruct((b, 2 * D_FF), F32)],
        compiler_params=_params("arbitrary"),
        name="ffn_sample",
    )(x, g, w_up, conv_w, w_down, p0, p1)


def _rope_tables(pos, rows):
    half = HEAD_DIM // 2
    inv_freq = ROPE_THETA ** (-jnp.arange(half, dtype=F32) / half)
    ang = pos.astype(F32)[:, None] * inv_freq[None, :]
    cos = jnp.cos(ang)
    sin = jnp.sin(ang)
    cos = jnp.tile(jnp.concatenate([cos, cos], axis=-1), (1, 4))
    sin = jnp.tile(jnp.concatenate([-sin, sin], axis=-1), (1, 4))
    if cos.shape[0] != rows:
        cos = jnp.broadcast_to(cos, (rows, 256))
        sin = jnp.broadcast_to(sin, (rows, 256))
    return cos, sin


def _block_diag_ones():
    i = jnp.arange(256)
    return (i[:, None] // HEAD_DIM == i[None, :] // HEAD_DIM).astype(BF16)


def _pad_cols(w, n):
    return jnp.pad(w, ((0, 0), (0, n - w.shape[1])))


def _nsa_weights(w_in, q_norm, k_norm, cmp_pos, cmp_w1, cmp_w2, w_out):
    gain = jnp.concatenate(
        [jnp.tile(q_norm, NSA_HEADS)]
        + [jnp.tile(k_norm[br], NSA_KV_HEADS) if kv == 0 else jnp.ones((NSA_KV,), F32)
           for br in range(3) for kv in range(2)]).reshape(1, NSA_QKV)
    pos_rows = [jnp.broadcast_to(cmp_pos[i][:, None, :], (CMP_BLOCK, NSA_KV_HEADS, HEAD_DIM))
                .reshape(1, CMP_BLOCK * NSA_KV) for i in range(2)]
    pos_blk = [cmp_pos[i].reshape(1, CMP_BLOCK * HEAD_DIM) for i in range(2)]
    return dict(w_in=_pad_cols(w_in, NSA_IN_PAD).astype(BF16), gain=gain, pos=pos_rows, pos_blk=pos_blk,
                w1=[cmp_w1[i].astype(BF16) for i in range(2)],
                w2=[cmp_w2[i].astype(BF16) for i in range(2)],
                w_out=w_out.astype(BF16))


def _kv_rows_view(a):
    b, _, t = a.shape
    return a.reshape(b, NSA_KV_HEADS, HEAD_DIM, t).transpose(0, 3, 1, 2)


def _nsa_prompt_layer(x, gmix, w, tabs, bd, b, t, tm):
    assert t % CMP_BLOCK == 0
    q, kc_rows, vc_rows, ks, kw, gates, kct, vct, kst, vst, kwt, vwt = proj_post(
        x, gmix, w["w_in"], w["gain"], tabs[0], tabs[1], bd, NSA_PLAN_P, NSA_OUTS_P, tm, b)
    n_cmp = t // CMP_BLOCK
    wide = CMP_BLOCK * NSA_KV
    kc, _ = compress_rows(kc_rows.reshape(b, n_cmp, wide), w["pos"][0], w["w1"][0], w["w2"][0], n_cmp)
    _, vc_t = compress_rows(vc_rows.reshape(b, n_cmp, wide), w["pos"][1], w["w1"][1], w["w2"][1], n_cmp)
    o = nsa_prompt_attn(q, gates, kc, vc_t, ks, vst, kw, vwt, b, t)
    x = matmul_res(o, w["w_out"], x, tm)
    keep = min(WINDOW, t)
    rows = tuple(_kv_rows_view(a) for a in (kct, vct, kst, vst))
    wins = tuple(_kv_rows_view(a[:, :, t - keep:]) for a in (kwt, vwt))
    return x, rows + wins


def _nsa_sample_layer(x, gmix, w, tabs, bd, j, page_table, caches, win_k, win_v):
    b = x.shape[0]
    n_phys = caches[0].shape[1]
    q, kc_new, vc_new, ks_new, vs_new, kw_new, vw_new, gates, kwt_new, vwt_new = proj_post(
        x, gmix, w["w_in"], w["gain"], tabs[0], tabs[1], bd, NSA_PLAN_S, NSA_OUTS_S, b, 1)
    fm = lambda c: c.transpose(0, 1, 3, 4, 2).reshape(-1, NSA_KV, c.shape[2])
    pool_ck, pool_cv, pool_sk, pool_sv = (fm(c) for c in caches)
    kc = compress_pages(pool_ck, page_table, j * n_phys, w["pos_blk"][0], w["w1"][0], w["w2"][0])
    vc = compress_pages(pool_cv, page_table, j * n_phys, w["pos_blk"][1], w["w1"][1], w["w2"][1])
    row3 = lambda a: a.reshape(b, 1, NSA_KV)
    o16, wk_out, wv_out = nsa_sample_attn(
        page_table, pool_sk, pool_sv, j * n_phys,
        q.reshape(b, NSA_HEADS, HEAD_DIM), gates[:, :NSA_HEADS * 3].reshape(b, NSA_HEADS, 3), kc, vc,
        row3(ks_new), row3(vs_new), row3(kw_new), row3(vw_new), kwt_new, vwt_new,
        fm(win_k), fm(win_v), j * b)
    x = matmul_res(o16.reshape(b, NSA_HEADS * HEAD_DIM), w["w_out"], x, b)
    kvshape = (b, 1, NSA_KV_HEADS, HEAD_DIM)
    rows = tuple(a.reshape(kvshape) for a in (kc_new, vc_new, ks_new, vs_new))
    wins = tuple(_kv_rows_view(a) for a in (wk_out, wv_out))
    return x, rows + wins


def _diff_weights(w_in, q_norm, k_norm, lam_vec, sub_norm, w_out):
    nmap = 2 * DIFF_HEADS
    gain = jnp.concatenate([jnp.tile(q_norm, nmap), jnp.tile(k_norm, nmap),
                            jnp.ones((D_MODEL,), F32)]).reshape(1, 3 * D_MODEL)
    return dict(w_in=w_in.astype(BF16), gain=gain, lam=lam_vec, sub=sub_norm, w_out=w_out.astype(BF16))


def kernel(x_prompt, x_sample, cache_nsa_cmp_k, cache_nsa_cmp_v, cache_nsa_slc_k, cache_nsa_slc_v, state_nsa_win_k, state_nsa_win_v, cache_diff_k, cache_diff_v, state_ssd_conv, state_ssd_ssm, state_ffn_conv, page_table, norm_mix, norm_ffn, nsa_w_in, nsa_q_norm, nsa_k_norm, nsa_cmp_pos, nsa_cmp_w1, nsa_cmp_w2, nsa_w_out, diff_w_in, diff_q_norm, diff_k_norm, diff_lambda, diff_sub_norm, diff_w_out, ssd_w_in, ssd_conv_w, ssd_conv_b, ssd_dt_bias, ssd_a_log, ssd_d, ssd_norm, ssd_w_out, ffn_w_up, ffn_conv_w, ffn_w_down):
    bp, tp, _ = x_prompt.shape
    bs, ts, _ = x_sample.shape
    assert ts == 1 and tp % Q_BLOCK == 0
    n_pages = page_table.shape[1]
    past = n_pages * PAGE_SIZE
    depth = norm_mix.shape[0]
    tm_p = min(PROMPT_ROW_TILE, tp)
    xp = x_prompt.reshape(bp * tp, D_MODEL)
    xs = x_sample.reshape(bs, D_MODEL)
    bd = _block_diag_ones()
    tabs_p = _rope_tables(jnp.arange(tp, dtype=jnp.int32), tp)
    tabs_s = _rope_tables(jnp.full((1,), past, jnp.int32), bs)
    nsa_p, nsa_s, diff_p, diff_s, ssd_p, ssd_s, ffn_p, ffn_s = [], [], [], [], [], [], [], []
    for i in range(depth):
        kind = i % N_MIXERS
        j = i // N_MIXERS
        if kind == 0:
            w = _nsa_weights(nsa_w_in[j], nsa_q_norm[j], nsa_k_norm[j], nsa_cmp_pos[j], nsa_cmp_w1[j],
                             nsa_cmp_w2[j], nsa_w_out[j])
            xp, outs = _nsa_prompt_layer(xp, norm_mix[i], w, tabs_p, bd, bp, tp, tm_p)
            nsa_p.append(outs)
            xs, outs = _nsa_sample_layer(xs, norm_mix[i], w, tabs_s, bd, j, page_table,
                                         (cache_nsa_cmp_k, cache_nsa_cmp_v, cache_nsa_slc_k, cache_nsa_slc_v),
                                         state_nsa_win_k, state_nsa_win_v)
            nsa_s.append(outs)
        elif kind == 1:
            lam_init = 0.8 - 0.6 * math.exp(-0.3 * i)
            w = _diff_weights(diff_w_in[j], diff_q_norm[j], diff_k_norm[j], diff_lambda[j], diff_sub_norm[j],
                              diff_w_out[j])
            q, k, v, kt, vt = proj_post(xp, norm_mix[i], w["w_in"], w["gain"], tabs_p[0], tabs_p[1], bd,
                                        DIFF_PLAN_P, DIFF_OUTS_P, tm_p, bp)
            o = diff_prompt_attn(q, k, vt, w["lam"], w["sub"], lam_init, bp, tp)
            xp = matmul_res(o, w["w_out"], xp, tm_p)
            diff_p.append((kt.reshape(bp, DIFF_HEADS, 2, HEAD_DIM, tp).transpose(0, 4, 1, 2, 3),
                           v.reshape(bp, tp, DIFF_HEADS, 2 * HEAD_DIM)))
            n_phys = cache_diff_k.shape[1]
            q, k, v = proj_post(xs, norm_mix[i], w["w_in"], w["gain"], tabs_s[0], tabs_s[1], bd,
                                DIFF_PLAN_S, DIFF_OUTS_S, bs, 1)
            r3 = lambda a: a.reshape(bs, 1, D_MODEL)
            pool_k = cache_diff_k.transpose(0, 1, 3, 4, 5, 2).reshape(-1, D_MODEL, PAGE_SIZE)
            pool_v = cache_diff_v.reshape(-1, PAGE_SIZE * DIFF_HEADS, 2 * HEAD_DIM)
            o = diff_sample_attn(page_table, pool_k, pool_v, j * n_phys,
                                 r3(q), r3(k), v.reshape(bs, DIFF_HEADS, 2 * HEAD_DIM), w["lam"],
                                 w["sub"].reshape(1, 2 * HEAD_DIM), lam_init)
            xs = matmul_res(o.reshape(bs, D_MODEL), w["w_out"], xs, bs)
            diff_s.append((k.reshape(bs, 1, DIFF_HEADS, 2, HEAD_DIM), v.reshape(bs, 1, DIFF_HEADS, 2 * HEAD_DIM)))
        else:
            w_in = _pad_cols(ssd_w_in[j], SSD_IN_PAD).astype(BF16)
            w_out = ssd_w_out[j].astype(BF16)
            pad_h = lambda a: jnp.pad(a, (0, LANE - SSD_HEADS)).reshape(1, LANE)
            bias_row, alog_row = pad_h(ssd_dt_bias[j]), pad_h(ssd_a_log[j])
            dskip_row = jnp.repeat(ssd_d[j], SSD_HEADDIM).reshape(1, SSD_D_INNER)
            xbc0 = SSD_D_INNER
            proj = norm_matmul(xp, norm_mix[i], w_in, tm_p)
            xbc = ssd_conv_prompt(proj, ssd_conv_w[j], ssd_conv_b[j], bp, tp, tm_p)
            y, st = ssd_scan_prompt(xbc, proj, bias_row, alog_row, dskip_row, bp, tp)
            xp = ssd_out(y, proj, ssd_norm[j], w_out, xp, tm_p)
            conv_new = proj.reshape(bp, tp, SSD_IN_PAD)[:, tp - (SSD_CONV_W - 1):, xbc0:xbc0 + SSD_CONV_DIM]
            ssm_new = st.reshape(bp, SSD_D_STATE, SSD_HEADS, SSD_HEADDIM).transpose(0, 2, 3, 1)
            ssd_p.append((conv_new, ssm_new))
            proj = norm_matmul(xs, norm_mix[i], w_in, bs)
            prev = state_ssd_conv[j]
            xbc = ssd_conv_sample(proj, prev.transpose(1, 0, 2), ssd_conv_w[j], ssd_conv_b[j])
            y, st = ssd_step_sample(xbc, proj, bias_row, alog_row, dskip_row,
                                    state_ssd_ssm.reshape((-1,) + state_ssd_ssm.shape[2:]), j * bs)
            xs = ssd_out(y.reshape(bs, SSD_D_INNER), proj, ssd_norm[j], w_out, xs, bs)
            conv_new = jnp.concatenate([prev[:, 1:], proj[:, None, xbc0:xbc0 + SSD_CONV_DIM]], axis=1)
            ssd_s.append((conv_new, st))
        w_up = ffn_w_up[i].astype(BF16)
        w_dn = ffn_w_down[i].astype(BF16)
        xp, st = ffn_prompt(xp, norm_ffn[i], w_up, ffn_conv_w[i], w_dn, bp, tp, tm_p)
        ffn_p.append(st[:, SUBLANE - (FFN_CONV_W - 1):])
        prev = state_ffn_conv[i]
        xs, up = ffn_sample(xs, norm_ffn[i], w_up, ffn_conv_w[i], w_dn, prev)
        ffn_s.append(jnp.concatenate([prev[:, 1:], up[:, None]], axis=1))
    outs = [xp.reshape(bp, tp, D_MODEL), xs.reshape(bs, ts, D_MODEL)]
    for r in range(6):
        outs.append(jnp.stack([o[r] for o in nsa_p]))
        outs.append(jnp.stack([o[r] for o in nsa_s]))
    for r in range(2):
        outs.append(jnp.stack([o[r] for o in diff_p]))
        outs.append(jnp.stack([o[r] for o in diff_s]))
    for r in range(2):
        outs.append(jnp.stack([o[r] for o in ssd_p]))
        outs.append(jnp.stack([o[r] for o in ssd_s]))
    outs.append(jnp.stack(ffn_p))
    outs.append(jnp.stack(ffn_s))
    return tuple(outs)
```
